```python
import math, functools
import jax, jax.numpy as jnp
from jax import lax
import numpy as np

D_MODEL = 4096
BATCH = 4
SEQ = 2048
DEPTH = 2
DEC_BATCH = 8
DEC_SEQ = 1
PAST_LEN = 16384
PAGE_SIZE = 128

N_MOD = 6
EPS = 1e-6
N_BRANCH = 4

GDN_HEADS = 8
GDN_DK = 128
GDN_DV = 128
CONV_W = 4
GDN_CHUNK = 64
GDN_QK = GDN_HEADS * GDN_DK
GDN_V = GDN_HEADS * GDN_DV
GDN_CONV_COLS = 2 * GDN_QK + GDN_V
GDN_COLS = GDN_CONV_COLS + GDN_V + 2 * GDN_HEADS

GLA_HEADS = 4
GLA_DK = 128
GLA_DV = 256
GLA_RANK = 16
GLA_TAU = 16.0
GLA_CHUNK = 64
GLA_QK = GLA_HEADS * GLA_DK
GLA_V = GLA_HEADS * GLA_DV
GLA_COLS = 2 * GLA_QK + 2 * GLA_V + GLA_RANK

ATT_HEADS = 8
ATT_KV_HEADS = 2
ATT_HD = 128
IDX_HEADS = 8
IDX_HD = 64
TOPK_MAX = 256
Q_BLOCK = 128
ROPE_THETA = 500000.0
ROPE_DIV = 4
ATT_Q = ATT_HEADS * ATT_HD
ATT_KV = ATT_KV_HEADS * ATT_HD
ATT_COLS = ATT_Q + 2 * ATT_KV + IDX_HEADS * IDX_HD + IDX_HD + IDX_HEADS

RWKV_HEADS = 16
RWKV_HD = 64
RWKV_W_RANK = 64
RWKV_A_RANK = 64
RWKV_G_RANK = 128
RWKV_GN_EPS = 64e-5
RWKV_W = RWKV_HEADS * RWKV_HD
RWKV_COLS = 3 * RWKV_W + RWKV_W_RANK + RWKV_A_RANK + RWKV_G_RANK

GATE_COLS = N_BRANCH * D_MODEL
IN_SPLITS = (GDN_COLS, GLA_COLS, ATT_COLS, RWKV_COLS, GATE_COLS)
IN_COLS = sum(IN_SPLITS)
BRANCH_SPLITS = (GDN_V, GLA_V, ATT_Q, RWKV_W)
MIX_WIDTH = sum(BRANCH_SPLITS)

PEER_HEADS = 8
PEER_NKEYS = 128
PEER_DKEY = 256
PEER_TOPK = 16
PEER_N = PEER_NKEYS * PEER_NKEYS
PEER_BLOCK = 64

kernel_name = 'hybrid_gdn_gla_dsa_rwkv7_peer_step'


def _split(a, sizes, axis=-1):
    return jnp.split(a, [int(s) for s in np.cumsum(sizes)[:-1]], axis=axis)


def rmsnorm(x, gain):
    xf = x.astype(jnp.float32)
    y = xf * lax.rsqrt(jnp.mean(xf * xf, axis=-1, keepdims=True) + EPS)
    return (y * gain.astype(jnp.float32)).astype(x.dtype)


def l2norm(x):
    xf = x.astype(jnp.float32)
    return xf * lax.rsqrt(jnp.sum(xf * xf, axis=-1, keepdims=True) + EPS)


def partial_rope(x, pos):
    rd = x.shape[-1] // ROPE_DIV
    half = rd // 2
    inv_freq = ROPE_THETA ** (-jnp.arange(half, dtype=jnp.float32) * 2.0 / rd)
    ang = pos.astype(jnp.float32)[:, None] * inv_freq[None, :]
    cos = jnp.cos(ang)[:, None, :]
    sin = jnp.sin(ang)[:, None, :]
    xf = x[..., :rd].astype(jnp.float32)
    x1, x2 = xf[..., :half], xf[..., half:]
    rot = jnp.concatenate([x1 * cos - x2 * sin, x2 * cos + x1 * sin], axis=-1)
    return jnp.concatenate([rot.astype(x.dtype), x[..., rd:]], axis=-1)


def causal_conv(x, buf, w):
    T = x.shape[1]
    xp = jnp.concatenate([buf.astype(x.dtype), x], axis=1)
    y = xp[:, 0:T] * w[0]
    for i in range(1, CONV_W):
        y = y + xp[:, i:i + T] * w[i]
    return jax.nn.silu(y), xp[:, T:]


def _to_chunks(a, C):
    B, T = a.shape[:2]
    pad = (-T) % C
    a = jnp.pad(a.astype(jnp.float32), [(0, 0), (0, pad)] + [(0, 0)] * (a.ndim - 2))
    n = (T + pad) // C
    a = a.reshape((B, n, C) + a.shape[2:])
    return jnp.transpose(a, (1, 0, 3, 2) + tuple(range(4, a.ndim)))


def _from_chunks(o, T):
    n, B, H, C, X = o.shape
    return jnp.transpose(o, (1, 0, 3, 2, 4)).reshape(B, n * C, H, X)[:, :T]


def gated_delta_rule(q, k, v, beta, g, s0):
    T = q.shape[1]
    DV = v.shape[-1]
    C = min(GDN_CHUNK, T)
    qc, kc, vc = _to_chunks(q, C), _to_chunks(k, C), _to_chunks(v, C)
    bc = _to_chunks(beta, C)
    gc = jnp.cumsum(_to_chunks(g, C), axis=-1)
    tri = jnp.tril(jnp.ones((C, C), bool))
    strict = jnp.tril(jnp.ones((C, C), bool), -1)
    decay = jnp.exp(jnp.where(tri, gc[..., :, None] - gc[..., None, :], -jnp.inf))
    kb = kc * bc[..., None]
    a_mat = jnp.where(strict, jnp.einsum('nbhik,nbhjk->nbhij', kb, kc) * decay, 0.0) + jnp.eye(C, dtype=jnp.float32)
    rhs = jnp.concatenate([vc * bc[..., None], kb * jnp.exp(gc)[..., None]], axis=-1)
    sol = lax.linalg.triangular_solve(a_mat, rhs, left_side=True, lower=True, unit_diagonal=True)
    u, w = sol[..., :DV], sol[..., DV:]
    qk = jnp.einsum('nbhik,nbhjk->nbhij', qc, kc) * decay
    q_dec = qc * jnp.exp(gc)[..., None]
    k_dec = kc * jnp.exp(gc[..., -1:] - gc)[..., None]
    g_last = jnp.exp(gc[..., -1])

    def step(S, xs):
        u_n, w_n, qk_n, qd_n, kd_n, gl_n = xs
        v_new = u_n - jnp.einsum('bhck,bhkv->bhcv', w_n, S)
        o = jnp.einsum('bhck,bhkv->bhcv', qd_n, S) + jnp.einsum('bhij,bhjv->bhiv', qk_n, v_new)
        S = S * gl_n[..., None, None] + jnp.einsum('bhck,bhcv->bhkv', kd_n, v_new)
        return S, o

    S, o = lax.scan(step, s0.astype(jnp.float32), (u, w, qk, q_dec, k_dec, g_last))
    return _from_chunks(o, T), S


def gla_chunked(q, k, v, log_a, s0):
    T = q.shape[1]
    C = min(GLA_CHUNK, T)
    qc, kc, vc = _to_chunks(q, C), _to_chunks(k, C), _to_chunks(v, C)
    bc = jnp.cumsum(_to_chunks(log_a, C), axis=-2)
    tri = jnp.tril(jnp.ones((C, C), bool))[..., None]

    def step(S, xs):
        q_n, k_n, v_n, b_n = xs
        rel = jnp.exp(jnp.where(tri, b_n[..., :, None, :] - b_n[..., None, :, :], -jnp.inf))
        att = jnp.einsum('bhik,bhijk,bhjk->bhij', q_n, rel, k_n)
        o = jnp.einsum('bhik,bhkv->bhiv', q_n * jnp.exp(b_n), S) + jnp.einsum('bhij,bhjv->bhiv', att, v_n)
        b_last = b_n[..., -1:, :]
        S = S * jnp.exp(b_last)[..., 0, :, None] + jnp.einsum('bhck,bhcv->bhkv', k_n * jnp.exp(b_last - b_n), v_n)
        return S, o

    S, o = lax.scan(step, s0.astype(jnp.float32), (qc, kc, vc, bc))
    return _from_chunks(o, T), S


def rwkv7_scan(r, w, k, v, a, b, s0):
    def step(S, xs):
        r_t, w_t, k_t, v_t, a_t, b_t = xs
        sa = jnp.einsum('bhij,bhj->bhi', S, a_t)
        S = S * w_t[:, :, None, :] + sa[..., None] * b_t[:, :, None, :] + v_t[..., None] * k_t[:, :, None, :]
        return S, jnp.einsum('bhij,bhj->bhi', S, r_t)
    xs = tuple(jnp.moveaxis(t.astype(jnp.float32), 1, 0) for t in (r, w, k, v, a, b))
    S, y = lax.scan(step, s0.astype(jnp.float32), xs)
    return jnp.moveaxis(y, 0, 1), S


def head_group_norm(y, w, b):
    B, T, H, N = y.shape
    mu = jnp.mean(y, axis=-1, keepdims=True)
    var = jnp.mean(jnp.square(y - mu), axis=-1, keepdims=True)
    yn = ((y - mu) * lax.rsqrt(var + RWKV_GN_EPS)).reshape(B, T, H * N)
    return yn * w.astype(jnp.float32) + b.astype(jnp.float32)


def indexer_scores(qi, wi, ki):
    dots = jnp.einsum('bqhd,bsd->bqhs', qi, ki).astype(jnp.float32)
    return jnp.einsum('bqh,bqhs->bqs', wi.astype(jnp.float32), jax.nn.relu(dots))


def select_keys(scores, qpos, ktop):
    S = scores.shape[-1]
    allowed = jnp.arange(S)[None, None, :] <= qpos[None, :, None]
    _, idx = lax.top_k(jnp.where(allowed, scores, -jnp.inf), ktop)
    return idx, idx <= qpos[None, :, None]


def sparse_attend(q, k_sel, v_sel, valid):
    B, Q, HQ, HD = q.shape
    qg = q.reshape(B, Q, ATT_KV_HEADS, HQ // ATT_KV_HEADS, HD)
    logits = jnp.einsum('bqhgd,bqkhd->bqhgk', qg, k_sel).astype(jnp.float32) * (HD ** -0.5)
    logits = jnp.where(valid[:, :, None, None, :], logits, -jnp.inf)
    p = jax.nn.softmax(logits, axis=-1).astype(v_sel.dtype)
    return jnp.einsum('bqhgk,bqkhd->bqhgd', p, v_sel).reshape(B, Q, HQ, HD)


def dsa_prompt(q, k, v, qi, ki, wi):
    B, T = q.shape[:2]
    ktop = min(TOPK_MAX, T // 4)
    qb = min(Q_BLOCK, T)
    nb = T // qb
    gather = jax.vmap(lambda rows, idx: rows[idx])

    def block(args):
        q_b, qi_b, wi_b, start = args
        qpos = start + jnp.arange(qb)
        idx, valid = select_keys(indexer_scores(qi_b, wi_b, ki), qpos, ktop)
        return sparse_attend(q_b, gather(k, idx), gather(v, idx), valid)

    blocks = lambda a: jnp.moveaxis(a.reshape((B, nb, qb) + a.shape[2:]), 1, 0)
    o = lax.map(block, (blocks(q), blocks(qi), blocks(wi), jnp.arange(nb) * qb))
    return jnp.moveaxis(o, 0, 1).reshape(q.shape)


def dsa_sample(q, k, v, qi, ki, wi, cache_k, cache_v, cache_ki, page_table):
    DB, DS = q.shape[:2]
    past = page_table.shape[1] * PAGE_SIZE
    ktop = min(TOPK_MAX, (past + DS) // 4)
    ki_past = cache_ki[page_table].reshape(DB, past, IDX_HD).astype(ki.dtype)
    ki_all = jnp.concatenate([ki_past, ki], axis=1)
    qpos = past + jnp.arange(DS)
    idx, valid = select_keys(indexer_scores(qi, wi, ki_all), qpos, ktop)
    is_new = idx >= past
    pidx = jnp.minimum(idx, past - 1)
    phys = jax.vmap(lambda pt, i: pt[i])(page_table, pidx // PAGE_SIZE)
    off = pidx % PAGE_SIZE
    nidx = jnp.clip(idx - past, 0, DS - 1)
    gather = jax.vmap(lambda rows, i: rows[i])
    sel = lambda cache, new: jnp.where(is_new[..., None, None], gather(new, nidx), cache[phys, off].astype(new.dtype))
    return sparse_attend(q, sel(cache_k, k), sel(cache_v, v), valid)


def gdn_branch(p, conv_buf, s0, lp):
    B, T, _ = p.shape
    conv_in, z, b_raw, a_raw = _split(p, (GDN_CONV_COLS, GDN_V, GDN_HEADS, GDN_HEADS))
    conv_out, new_buf = causal_conv(conv_in, conv_buf, lp['gdn_conv_w'])
    q, k, v = _split(conv_out, (GDN_QK, GDN_QK, GDN_V))
    q = l2norm(q.reshape(B, T, GDN_HEADS, GDN_DK)) * (GDN_DK ** -0.5)
    k = l2norm(k.reshape(B, T, GDN_HEADS, GDN_DK))
    v = v.reshape(B, T, GDN_HEADS, GDN_DV)
    beta = jax.nn.sigmoid(b_raw.astype(jnp.float32))
    g = -jnp.exp(lp['gdn_A_log'].astype(jnp.float32)) * jax.nn.softplus(a_raw.astype(jnp.float32) + lp['gdn_dt_bias'].astype(jnp.float32))
    o, s_new = gated_delta_rule(q, k, v, beta, g, s0)
    o = rmsnorm(o, lp['gdn_norm']) * jax.nn.silu(z.reshape(B, T, GDN_HEADS, GDN_DV).astype(jnp.float32))
    return o.reshape(B, T, GDN_V), new_buf, s_new


def gla_branch(p, s0, lp):
    B, T, _ = p.shape
    q, k, v, r, gd = _split(p, (GLA_QK, GLA_QK, GLA_V, GLA_V, GLA_RANK))
    hs = (B, T, GLA_HEADS)
    log_a = jax.nn.log_sigmoid((gd @ lp['gla_gate_up'] + lp['gla_gate_bias']).astype(jnp.float32)) / GLA_TAU
    o, s_new = gla_chunked(q.reshape(hs + (GLA_DK,)) * (GLA_DK ** -0.5), k.reshape(hs + (GLA_DK,)),
                           v.reshape(hs + (GLA_DV,)), log_a.reshape(hs + (GLA_DK,)), s0)
    o = rmsnorm(o, lp['gla_norm']) * jax.nn.silu(r.reshape(hs + (GLA_DV,)).astype(jnp.float32))
    return o.reshape(B, T, GLA_V), s_new


def dsa_branch(p, pos, attend):
    B, T, _ = p.shape
    q, k, v, qi, ki, wi = _split(p, (ATT_Q, ATT_KV, ATT_KV, IDX_HEADS * IDX_HD, IDX_HD, IDX_HEADS))
    q = partial_rope(q.reshape(B, T, ATT_HEADS, ATT_HD), pos)
    k = partial_rope(k.reshape(B, T, ATT_KV_HEADS, ATT_HD), pos)
    v = v.reshape(B, T, ATT_KV_HEADS, ATT_HD)
    qi = partial_rope(qi.reshape(B, T, IDX_HEADS, IDX_HD), pos)
    ki = partial_rope(ki.reshape(B, T, 1, IDX_HD), pos)[:, :, 0]
    o = attend(q, k, v, qi, ki, wi)
    return o.reshape(B, T, ATT_Q), (k, v, ki)


def rwkv_branch(p, shift_buf, s0, lp):
    B, T, _ = p.shape
    prev = jnp.concatenate([shift_buf.astype(p.dtype), p[:, :-1]], axis=1)
    pm = p + (prev - p) * lp['rwkv_mu']
    r, wd, k, v, ad, gd = _split(pm, (RWKV_W, RWKV_W_RANK, RWKV_W, RWKV_W, RWKV_A_RANK, RWKV_G_RANK))
    hs = (B, T, RWKV_HEADS, RWKV_HD)
    w_log = -jax.nn.softplus(-(lp['rwkv_w0'] + jnp.tanh(wd) @ lp['rwkv_w2']).astype(jnp.float32)) - 0.5
    decay = jnp.exp(-jnp.exp(w_log))
    a = jax.nn.sigmoid((lp['rwkv_a0'] + ad @ lp['rwkv_a2']).astype(jnp.float32))
    g = (jax.nn.sigmoid(gd) @ lp['rwkv_g2']).astype(jnp.float32)
    kf = k.astype(jnp.float32)
    kk = l2norm((kf * lp['rwkv_kk']).reshape(hs))
    kf = (kf * (1.0 + (a - 1.0) * lp['rwkv_ka'])).reshape(hs)
    rf = r.astype(jnp.float32).reshape(hs)
    vf = v.astype(jnp.float32).reshape(hs)
    y, s_new = rwkv7_scan(rf, decay.reshape(hs), kf, vf, -kk, kk * a.reshape(hs), s0)
    y = head_group_norm(y, lp['rwkv_ln_w'], lp['rwkv_ln_b'])
    bonus = (jnp.sum(rf * kf * lp['rwkv_rk'], axis=-1, keepdims=True) * vf).reshape(B, T, RWKV_W)
    return (y + bonus) * g, p[:, -1:], s_new


def token_mixers(h, pos, st, lp, attend):
    gdn_s, gdn_buf, gla_s, rwkv_s, rwkv_buf = st
    proj = h @ lp['w_in']
    p_gdn, p_gla, p_att, p_rwkv, p_gate = _split(proj, IN_SPLITS)
    o_gdn, gdn_buf, gdn_s = gdn_branch(p_gdn, gdn_buf, gdn_s, lp)
    o_gla, gla_s = gla_branch(p_gla, gla_s, lp)
    o_att, rows = dsa_branch(p_att, pos, attend)
    o_rwkv, rwkv_buf, rwkv_s = rwkv_branch(p_rwkv, rwkv_buf, rwkv_s, lp)
    w_rows = _split(lp['w_branch'], BRANCH_SPLITS, axis=0)
    gate_cols = _split(p_gate, (D_MODEL,) * N_BRANCH)
    merged = None
    for o, w, gc in zip((o_gdn, o_gla, o_att, o_rwkv), w_rows, gate_cols):
        term = jax.nn.sigmoid(gc.astype(jnp.float32)) * (o.astype(h.dtype) @ w).astype(jnp.float32)
        merged = term if merged is None else merged + term
    out = merged.astype(h.dtype) @ lp['w_out']
    return out, rows + (gdn_s, gdn_buf, gla_s, rwkv_s, rwkv_buf)


def peer_ffn(h, wq, subkeys, u, v):
    B, T, D = h.shape
    n = B * T
    blk = min(PEER_BLOCK, n)
    pad = (-n) % blk
    xt = jnp.pad(h.reshape(n, D), [(0, pad), (0, 0)])

    def one(xb):
        q = (xb @ wq).reshape(blk, PEER_HEADS, 2, PEER_DKEY // 2)
        s = jnp.einsum('thpd,hpnd->thpn', q, subkeys).astype(jnp.float32)
        sv, si = lax.top_k(s, PEER_TOPK)
        cand = (sv[:, :, 0, :, None] + sv[:, :, 1, None, :]).reshape(blk, PEER_HEADS, PEER_TOPK * PEER_TOPK)
        cv, ci = lax.top_k(cand, PEER_TOPK)
        e = (jnp.take_along_axis(si[:, :, 0], ci // PEER_TOPK, axis=-1) * PEER_NKEYS
             + jnp.take_along_axis(si[:, :, 1], ci % PEER_TOPK, axis=-1))
        gate = jax.nn.softmax(cv, axis=-1)
        act = jax.nn.gelu(jnp.einsum('thkd,td->thk', u[e], xb).astype(jnp.float32), approximate=False)
        return jnp.einsum('thk,thkd->td', (gate * act).astype(xb.dtype), v[e])

    y = lax.map(one, xt.reshape(-1, blk, D))
    return y.reshape(-1, D)[:n].reshape(B, T, D)


def trunk_layer(x, c, pos, st, lp, attend):
    B = x.shape[0]
    mod = (jax.nn.silu(c) @ lp['w_ada'] + lp['b_ada']).reshape(B, N_MOD, 1, D_MODEL)
    shift1, scale1, gate1, shift2, scale2, gate2 = [mod[:, i] for i in range(N_MOD)]
    h = rmsnorm(x, lp['norm1']) * (1.0 + scale1) + shift1
    mix, new_st = token_mixers(h, pos, st, lp, attend)
    x = x + gate1 * mix
    h = rmsnorm(x, lp['norm2']) * (1.0 + scale2) + shift2
    x = x + gate2 * peer_ffn(h, lp['peer_wq'], lp['peer_subkeys'], lp['peer_u'], lp['peer_v'])
    return x, new_st


def setup_inputs(seed: int = 0) -> dict:
    key = jax.random.key(seed)
    keys = iter(jax.random.split(key, 64))

    def nrm(shape, scale):
        return jax.random.normal(next(keys), shape, jnp.float32) * scale

    def unif(shape, lo, hi):
        return jax.random.uniform(next(keys), shape, jnp.float32, lo, hi)

    L = DEPTH
    n_pages = PAST_LEN // PAGE_SIZE
    n_used = DEC_BATCH * n_pages
    n_pool = n_used + max(1, n_used // 4)
    page_table = jax.random.permutation(next(keys), n_pool)[:n_used].reshape(DEC_BATCH, n_pages).astype(jnp.int32)
    dt = jnp.exp(unif((L, GDN_HEADS), math.log(1e-3), math.log(1e-1)))
    return {
        'x_prompt': nrm((BATCH, SEQ, D_MODEL), 1.0),
        'x_sample': nrm((DEC_BATCH, DEC_SEQ, D_MODEL), 1.0),
        'cache_k': nrm((L, n_pool, PAGE_SIZE, ATT_KV_HEADS, ATT_HD), 1.0),
        'cache_v': nrm((L, n_pool, PAGE_SIZE, ATT_KV_HEADS, ATT_HD), 1.0),
        'cache_kidx': nrm((L, n_pool, PAGE_SIZE, IDX_HD), 1.0),
        'state_gdn': nrm((L, DEC_BATCH, GDN_HEADS, GDN_DK, GDN_DV), 0.05),
        'state_gdn_conv': nrm((L, DEC_BATCH, CONV_W - 1, GDN_CONV_COLS), 1.0),
        'state_gla': nrm((L, DEC_BATCH, GLA_HEADS, GLA_DK, GLA_DV), 0.05),
        'state_rwkv': nrm((L, DEC_BATCH, RWKV_HEADS, RWKV_HD, RWKV_HD), 0.05),
        'state_rwkv_shift': nrm((L, DEC_BATCH, 1, RWKV_COLS), 1.0),
        'page_table': page_table,
        'c_prompt': nrm((BATCH, D_MODEL), 1.0),
        'c_sample': nrm((DEC_BATCH, D_MODEL), 1.0),
        'w_ada': nrm((L, D_MODEL, N_MOD * D_MODEL), 0.5 * D_MODEL ** -0.5),
        'b_ada': nrm((L, N_MOD * D_MODEL), 0.01),
        'norm1': 1.0 + nrm((L, D_MODEL), 0.05),
        'norm2': 1.0 + nrm((L, D_MODEL), 0.05),
        'w_in': nrm((L, D_MODEL, IN_COLS), D_MODEL ** -0.5),
        'gdn_conv_w': nrm((L, CONV_W, GDN_CONV_COLS), CONV_W ** -0.5),
        'gdn_A_log': jnp.log(unif((L, GDN_HEADS), 1.0, 16.0)),
        'gdn_dt_bias': dt + jnp.log(-jnp.expm1(-dt)),
        'gdn_norm': 1.0 + nrm((L, GDN_DV), 0.05),
        'gla_gate_up': nrm((L, GLA_RANK, GLA_QK), GLA_RANK ** -0.5),
        'gla_gate_bias': nrm((L, GLA_QK), 0.1),
        'gla_norm': 1.0 + nrm((L, GLA_DV), 0.05),
        'rwkv_mu': unif((L, RWKV_COLS), 0.0, 1.0),
        'rwkv_w0': nrm((L, RWKV_W), 0.5),
        'rwkv_w2': nrm((L, RWKV_W_RANK, RWKV_W), 0.5 * RWKV_W_RANK ** -0.5),
        'rwkv_a0': nrm((L, RWKV_W), 0.1),
        'rwkv_a2': nrm((L, RWKV_A_RANK, RWKV_W), 0.5 * RWKV_A_RANK ** -0.5),
        'rwkv_g2': nrm((L, RWKV_G_RANK, RWKV_W), RWKV_G_RANK ** -0.5),
        'rwkv_kk': 0.85 + nrm((L, RWKV_W), 0.05),
        'rwkv_ka': 1.0 + nrm((L, RWKV_W), 0.05),
        'rwkv_rk': nrm((L, RWKV_HEADS, RWKV_HD), 0.1),
        'rwkv_ln_w': 1.0 + nrm((L, RWKV_W), 0.05),
        'rwkv_ln_b': nrm((L, RWKV_W), 0.01),
        'w_branch': nrm((L, MIX_WIDTH, D_MODEL), GDN_V ** -0.5),
        'w_out': nrm((L, D_MODEL, D_MODEL), D_MODEL ** -0.5),
        'peer_wq': nrm((L, D_MODEL, PEER_HEADS * PEER_DKEY), D_MODEL ** -0.5),
        'peer_subkeys': nrm((L, PEER_HEADS, 2, PEER_NKEYS, PEER_DKEY // 2), (PEER_DKEY // 2) ** -0.5),
        'peer_u': nrm((L, PEER_N, D_MODEL), D_MODEL ** -0.5),
        'peer_v': nrm((L, PEER_N, D_MODEL), 0.25),
        'final_norm': 1.0 + nrm((D_MODEL,), 0.05),
    }


def reference(x_prompt, x_sample, cache_k, cache_v, cache_kidx, state_gdn, state_gdn_conv, state_gla,
              state_rwkv, state_rwkv_shift, page_table, c_prompt, c_sample,
              w_ada, b_ada, norm1, norm2, w_in, gdn_conv_w, gdn_A_log, gdn_dt_bias, gdn_norm,
              gla_gate_up, gla_gate_bias, gla_norm,
              rwkv_mu, rwkv_w0, rwkv_w2, rwkv_a0, rwkv_a2, rwkv_g2, rwkv_kk, rwkv_ka, rwkv_rk,
              rwkv_ln_w, rwkv_ln_b, w_branch, w_out, peer_wq, peer_subkeys, peer_u, peer_v, final_norm):
    Bp, T, _ = x_prompt.shape
    DS = x_sample.shape[1]
    past = page_table.shape[1] * PAGE_SIZE
    pos_p = jnp.arange(T, dtype=jnp.int32)
    pos_s = past + jnp.arange(DS, dtype=jnp.int32)
    st_p0 = (jnp.zeros((Bp, GDN_HEADS, GDN_DK, GDN_DV), jnp.float32),
             jnp.zeros((Bp, CONV_W - 1, GDN_CONV_COLS), x_prompt.dtype),
             jnp.zeros((Bp, GLA_HEADS, GLA_DK, GLA_DV), jnp.float32),
             jnp.zeros((Bp, RWKV_HEADS, RWKV_HD, RWKV_HD), jnp.float32),
             jnp.zeros((Bp, 1, RWKV_COLS), x_prompt.dtype))
    weights = {'w_ada': w_ada, 'b_ada': b_ada, 'norm1': norm1, 'norm2': norm2, 'w_in': w_in,
               'gdn_conv_w': gdn_conv_w, 'gdn_A_log': gdn_A_log, 'gdn_dt_bias': gdn_dt_bias, 'gdn_norm': gdn_norm,
               'gla_gate_up': gla_gate_up, 'gla_gate_bias': gla_gate_bias, 'gla_norm': gla_norm,
               'rwkv_mu': rwkv_mu, 'rwkv_w0': rwkv_w0, 'rwkv_w2': rwkv_w2, 'rwkv_a0': rwkv_a0, 'rwkv_a2': rwkv_a2,
               'rwkv_g2': rwkv_g2, 'rwkv_kk': rwkv_kk, 'rwkv_ka': rwkv_ka, 'rwkv_rk': rwkv_rk,
               'rwkv_ln_w': rwkv_ln_w, 'rwkv_ln_b': rwkv_ln_b, 'w_branch': w_branch, 'w_out': w_out,
               'peer_wq': peer_wq, 'peer_subkeys': peer_subkeys, 'peer_u': peer_u, 'peer_v': peer_v}
    xp, xs = x_prompt, x_sample
    out_p, out_s = [], []
    for l in range(DEPTH):
        lp = {name: arr[l] for name, arr in weights.items()}
        xp, new_p = trunk_layer(xp, c_prompt, pos_p, st_p0, lp, dsa_prompt)
        attend_s = functools.partial(dsa_sample, cache_k=cache_k[l], cache_v=cache_v[l],
                                     cache_ki=cache_kidx[l], page_table=page_table)
        st_s0 = (state_gdn[l], state_gdn_conv[l], state_gla[l], state_rwkv[l], state_rwkv_shift[l])
        xs, new_s = trunk_layer(xs, c_sample, pos_s, st_s0, lp, attend_s)
        out_p.append(new_p)
        out_s.append(new_s)
    y_prompt = rmsnorm(xp, final_norm)
    y_sample = rmsnorm(xs, final_norm)
    k_p, v_p, kidx_p, gdn_p, gdn_conv_p, gla_p, rwkv_p, rwkv_shift_p = [jnp.stack(z) for z in zip(*out_p)]
    k_s, v_s, kidx_s, gdn_s, gdn_conv_s, gla_s, rwkv_s, rwkv_shift_s = [jnp.stack(z) for z in zip(*out_s)]
    return (y_prompt, y_sample, k_p, v_p, kidx_p, gdn_p, gdn_conv_p, gla_p, rwkv_p, rwkv_shift_p,
            k_s, v_s, kidx_s, gdn_s, gdn_conv_s, gla_s, rwkv_s, rwkv_shift_s)
```

```python
import math, functools
import jax, jax.numpy as jnp
from jax import lax
import numpy as np
from jax.experimental import pallas as pl
from jax.experimental.pallas import tpu as pltpu

D_MODEL = 4096
PAGE_SIZE = 128
N_MOD = 6
EPS = 1e-6
N_BRANCH = 4

GDN_HEADS = 8
GDN_DK = 128
GDN_DV = 128
CONV_W = 4
GDN_CHUNK = 64
GDN_QK = GDN_HEADS * GDN_DK
GDN_V = GDN_HEADS * GDN_DV
GDN_CONV_COLS = 2 * GDN_QK + GDN_V
GDN_COLS = GDN_CONV_COLS + GDN_V + 2 * GDN_HEADS

GLA_HEADS = 4
GLA_DK = 128
GLA_DV = 256
GLA_RANK = 16
GLA_TAU = 16.0
GLA_CHUNK = 64
GLA_QK = GLA_HEADS * GLA_DK
GLA_V = GLA_HEADS * GLA_DV
GLA_COLS = 2 * GLA_QK + 2 * GLA_V + GLA_RANK

ATT_HEADS = 8
ATT_KV_HEADS = 2
ATT_HD = 128
IDX_HEADS = 8
IDX_HD = 64
TOPK_MAX = 256
Q_BLOCK = 128
ROPE_THETA = 500000.0
ROPE_DIV = 4
ATT_Q = ATT_HEADS * ATT_HD
ATT_KV = ATT_KV_HEADS * ATT_HD
ATT_COLS = ATT_Q + 2 * ATT_KV + IDX_HEADS * IDX_HD + IDX_HD + IDX_HEADS

RWKV_HEADS = 16
RWKV_HD = 64
RWKV_W_RANK = 64
RWKV_A_RANK = 64
RWKV_G_RANK = 128
RWKV_GN_EPS = 64e-5
RWKV_W = RWKV_HEADS * RWKV_HD
RWKV_COLS = 3 * RWKV_W + RWKV_W_RANK + RWKV_A_RANK + RWKV_G_RANK

GATE_COLS = N_BRANCH * D_MODEL
IN_SPLITS = (GDN_COLS, GLA_COLS, ATT_COLS, RWKV_COLS, GATE_COLS)
IN_COLS = sum(IN_SPLITS)
BRANCH_SPLITS = (GDN_V, GLA_V, ATT_Q, RWKV_W)
MIX_WIDTH = sum(BRANCH_SPLITS)

PEER_HEADS = 8
PEER_NKEYS = 128
PEER_DKEY = 256
PEER_TOPK = 16
PEER_N = PEER_NKEYS * PEER_NKEYS
PEER_BLOCK = 64

VMEM_LIMIT_BYTES = 56 * 1024 * 1024
MM_TILE_M = 1024
MM_TILE_N = 512
BF16_SUBLANES = 16


def _mm_kernel(x_ref, w_ref, o_ref):
    o_ref[...] = jnp.dot(x_ref[...], w_ref[...], preferred_element_type=jnp.float32)


def _round_up(n, m):
    return (n + m - 1) // m * m


def _mm(x, w):
    M, K = x.shape
    N = w.shape[1]
    tm = min(MM_TILE_M, _round_up(M, BF16_SUBLANES))
    Mp = _round_up(M, tm)
    tn = MM_TILE_N
    Np = _round_up(N, tn)
    xb = x.astype(jnp.bfloat16)
    wb = w.astype(jnp.bfloat16)
    if Mp != M:
        xb = jnp.pad(xb, ((0, Mp - M), (0, 0)))
    if Np != N:
        wb = jnp.pad(wb, ((0, 0), (0, Np - N)))
    out = pl.pallas_call(
        _mm_kernel,
        grid=(Np // tn, Mp // tm),
        in_specs=[pl.BlockSpec((tm, K), lambda j, i: (i, 0)),
                  pl.BlockSpec((K, tn), lambda j, i: (0, j))],
        out_specs=pl.BlockSpec((tm, tn), lambda j, i: (i, j)),
        out_shape=jax.ShapeDtypeStruct((Mp, Np), jnp.float32),
        compiler_params=pltpu.CompilerParams(
            dimension_semantics=("parallel", "parallel"),
            vmem_limit_bytes=VMEM_LIMIT_BYTES),
        name="mm",
    )(xb, wb)
    return out[:M, :N]


def _mm3(x, w):
    B, T, K = x.shape
    return _mm(x.reshape(B * T, K), w).reshape(B, T, w.shape[1])


def _split(a, sizes, axis=-1):
    return jnp.split(a, [int(s) for s in np.cumsum(sizes)[:-1]], axis=axis)


def rmsnorm(x, gain):
    xf = x.astype(jnp.float32)
    y = xf * lax.rsqrt(jnp.mean(xf * xf, axis=-1, keepdims=True) + EPS)
    return (y * gain.astype(jnp.float32)).astype(x.dtype)


def l2norm(x):
    xf = x.astype(jnp.float32)
    return xf * lax.rsqrt(jnp.sum(xf * xf, axis=-1, keepdims=True) + EPS)


def partial_rope(x, pos):
    rd = x.shape[-1] // ROPE_DIV
    half = rd // 2
    inv_freq = ROPE_THETA ** (-jnp.arange(half, dtype=jnp.float32) * 2.0 / rd)
    ang = pos.astype(jnp.float32)[:, None] * inv_freq[None, :]
    cos = jnp.cos(ang)[:, None, :]
    sin = jnp.sin(ang)[:, None, :]
    xf = x[..., :rd].astype(jnp.float32)
    x1, x2 = xf[..., :half], xf[..., half:]
    rot = jnp.concatenate([x1 * cos - x2 * sin, x2 * cos + x1 * sin], axis=-1)
    return jnp.concatenate([rot.astype(x.dtype), x[..., rd:]], axis=-1)


def causal_conv(x, buf, w):
    T = x.shape[1]
    xp = jnp.concatenate([buf.astype(x.dtype), x], axis=1)
    y = xp[:, 0:T] * w[0]
    for i in range(1, CONV_W):
        y = y + xp[:, i:i + T] * w[i]
    return jax.nn.silu(y), xp[:, T:]


def _to_chunks(a, C):
    B, T = a.shape[:2]
    pad = (-T) % C
    a = jnp.pad(a.astype(jnp.float32), [(0, 0), (0, pad)] + [(0, 0)] * (a.ndim - 2))
    n = (T + pad) // C
    a = a.reshape((B, n, C) + a.shape[2:])
    return jnp.transpose(a, (1, 0, 3, 2) + tuple(range(4, a.ndim)))


def _from_chunks(o, T):
    n, B, H, C, X = o.shape
    return jnp.transpose(o, (1, 0, 3, 2, 4)).reshape(B, n * C, H, X)[:, :T]


def gated_delta_rule(q, k, v, beta, g, s0):
    T = q.shape[1]
    DV = v.shape[-1]
    C = min(GDN_CHUNK, T)
    qc, kc, vc = _to_chunks(q, C), _to_chunks(k, C), _to_chunks(v, C)
    bc = _to_chunks(beta, C)
    gc = jnp.cumsum(_to_chunks(g, C), axis=-1)
    tri = jnp.tril(jnp.ones((C, C), bool))
    strict = jnp.tril(jnp.ones((C, C), bool), -1)
    decay = jnp.exp(jnp.where(tri, gc[..., :, None] - gc[..., None, :], -jnp.inf))
    kb = kc * bc[..., None]
    a_mat = jnp.where(strict, jnp.einsum('nbhik,nbhjk->nbhij', kb, kc) * decay, 0.0) + jnp.eye(C, dtype=jnp.float32)
    rhs = jnp.concatenate([vc * bc[..., None], kb * jnp.exp(gc)[..., None]], axis=-1)
    sol = lax.linalg.triangular_solve(a_mat, rhs, left_side=True, lower=True, unit_diagonal=True)
    u, w = sol[..., :DV], sol[..., DV:]
    qk = jnp.einsum('nbhik,nbhjk->nbhij', qc, kc) * decay
    q_dec = qc * jnp.exp(gc)[..., None]
    k_dec = kc * jnp.exp(gc[..., -1:] - gc)[..., None]
    g_last = jnp.exp(gc[..., -1])

    def step(S, xs):
        u_n, w_n, qk_n, qd_n, kd_n, gl_n = xs
        v_new = u_n - jnp.einsum('bhck,bhkv->bhcv', w_n, S)
        o = jnp.einsum('bhck,bhkv->bhcv', qd_n, S) + jnp.einsum('bhij,bhjv->bhiv', qk_n, v_new)
        S = S * gl_n[..., None, None] + jnp.einsum('bhck,bhcv->bhkv', kd_n, v_new)
        return S, o

    S, o = lax.scan(step, s0.astype(jnp.float32), (u, w, qk, q_dec, k_dec, g_last))
    return _from_chunks(o, T), S


def gla_chunked(q, k, v, log_a, s0):
    T = q.shape[1]
    C = min(GLA_CHUNK, T)
    qc, kc, vc = _to_chunks(q, C), _to_chunks(k, C), _to_chunks(v, C)
    bc = jnp.cumsum(_to_chunks(log_a, C), axis=-2)
    tri = jnp.tril(jnp.ones((C, C), bool))[..., None]

    def step(S, xs):
        q_n, k_n, v_n, b_n = xs
        rel = jnp.exp(jnp.where(tri, b_n[..., :, None, :] - b_n[..., None, :, :], -jnp.inf))
        att = jnp.einsum('bhik,bhijk,bhjk->bhij', q_n, rel, k_n)
        o = jnp.einsum('bhik,bhkv->bhiv', q_n * jnp.exp(b_n), S) + jnp.einsum('bhij,bhjv->bhiv', att, v_n)
        b_last = b_n[..., -1:, :]
        S = S * jnp.exp(b_last)[..., 0, :, None] + jnp.einsum('bhck,bhcv->bhkv', k_n * jnp.exp(b_last - b_n), v_n)
        return S, o

    S, o = lax.scan(step, s0.astype(jnp.float32), (qc, kc, vc, bc))
    return _from_chunks(o, T), S


def rwkv7_scan(r, w, k, v, a, b, s0):
    def step(S, xs):
        r_t, w_t, k_t, v_t, a_t, b_t = xs
        sa = jnp.einsum('bhij,bhj->bhi', S, a_t)
        S = S * w_t[:, :, None, :] + sa[..., None] * b_t[:, :, None, :] + v_t[..., None] * k_t[:, :, None, :]
        return S, jnp.einsum('bhij,bhj->bhi', S, r_t)
    xs = tuple(jnp.moveaxis(t.astype(jnp.float32), 1, 0) for t in (r, w, k, v, a, b))
    S, y = lax.scan(step, s0.astype(jnp.float32), xs)
    return jnp.moveaxis(y, 0, 1), S


def head_group_norm(y, w, b):
    B, T, H, N = y.shape
    mu = jnp.mean(y, axis=-1, keepdims=True)
    var = jnp.mean(jnp.square(y - mu), axis=-1, keepdims=True)
    yn = ((y - mu) * lax.rsqrt(var + RWKV_GN_EPS)).reshape(B, T, H * N)
    return yn * w.astype(jnp.float32) + b.astype(jnp.float32)


def indexer_scores(qi, wi, ki):
    dots = jnp.einsum('bqhd,bsd->bqhs', qi, ki).astype(jnp.float32)
    return jnp.einsum('bqh,bqhs->bqs', wi.astype(jnp.float32), jax.nn.relu(dots))


def select_keys(scores, qpos, ktop):
    S = scores.shape[-1]
    allowed = jnp.arange(S)[None, None, :] <= qpos[None, :, None]
    _, idx = lax.top_k(jnp.where(allowed, scores, -jnp.inf), ktop)
    return idx, idx <= qpos[None, :, None]


def sparse_attend(q, k_sel, v_sel, valid):
    B, Q, HQ, HD = q.shape
    qg = q.reshape(B, Q, ATT_KV_HEADS, HQ // ATT_KV_HEADS, HD)
    logits = jnp.einsum('bqhgd,bqkhd->bqhgk', qg, k_sel).astype(jnp.float32) * (HD ** -0.5)
    logits = jnp.where(valid[:, :, None, None, :], logits, -jnp.inf)
    p = jax.nn.softmax(logits, axis=-1).astype(v_sel.dtype)
    return jnp.einsum('bqhgk,bqkhd->bqhgd', p, v_sel).reshape(B, Q, HQ, HD)


def dsa_prompt(q, k, v, qi, ki, wi):
    B, T = q.shape[:2]
    ktop = min(TOPK_MAX, T // 4)
    qb = min(Q_BLOCK, T)
    nb = T // qb
    gather = jax.vmap(lambda rows, idx: rows[idx])

    def block(args):
        q_b, qi_b, wi_b, start = args
        qpos = start + jnp.arange(qb)
        idx, valid = select_keys(indexer_scores(qi_b, wi_b, ki), qpos, ktop)
        return sparse_attend(q_b, gather(k, idx), gather(v, idx), valid)

    blocks = lambda a: jnp.moveaxis(a.reshape((B, nb, qb) + a.shape[2:]), 1, 0)
    o = lax.map(block, (blocks(q), blocks(qi), blocks(wi), jnp.arange(nb) * qb))
    return jnp.moveaxis(o, 0, 1).reshape(q.shape)


def dsa_sample(q, k, v, qi, ki, wi, cache_k, cache_v, cache_ki, page_table):
    DB, DS = q.shape[:2]
    past = page_table.shape[1] * PAGE_SIZE
    ktop = min(TOPK_MAX, (past + DS) // 4)
    ki_past = cache_ki[page_table].reshape(DB, past, IDX_HD).astype(ki.dtype)
    ki_all = jnp.concatenate([ki_past, ki], axis=1)
    qpos = past + jnp.arange(DS)
    idx, valid = select_keys(indexer_scores(qi, wi, ki_all), qpos, ktop)
    is_new = idx >= past
    pidx = jnp.minimum(idx, past - 1)
    phys = jax.vmap(lambda pt, i: pt[i])(page_table, pidx // PAGE_SIZE)
    off = pidx % PAGE_SIZE
    nidx = jnp.clip(idx - past, 0, DS - 1)
    gather = jax.vmap(lambda rows, i: rows[i])
    sel = lambda cache, new: jnp.where(is_new[..., None, None], gather(new, nidx), cache[phys, off].astype(new.dtype))
    return sparse_attend(q, sel(cache_k, k), sel(cache_v, v), valid)


def gdn_branch(p, conv_buf, s0, lp):
    B, T, _ = p.shape
    conv_in, z, b_raw, a_raw = _split(p, (GDN_CONV_COLS, GDN_V, GDN_HEADS, GDN_HEADS))
    conv_out, new_buf = causal_conv(conv_in, conv_buf, lp['gdn_conv_w'])
    q, k, v = _split(conv_out, (GDN_QK, GDN_QK, GDN_V))
    q = l2norm(q.reshape(B, T, GDN_HEADS, GDN_DK)) * (GDN_DK ** -0.5)
    k = l2norm(k.reshape(B, T, GDN_HEADS, GDN_DK))
    v = v.reshape(B, T, GDN_HEADS, GDN_DV)
    beta = jax.nn.sigmoid(b_raw.astype(jnp.float32))
    g = -jnp.exp(lp['gdn_A_log'].astype(jnp.float32)) * jax.nn.softplus(a_raw.astype(jnp.float32) + lp['gdn_dt_bias'].astype(jnp.float32))
    o, s_new = gated_delta_rule(q, k, v, beta, g, s0)
    o = rmsnorm(o, lp['gdn_norm']) * jax.nn.silu(z.reshape(B, T, GDN_HEADS, GDN_DV).astype(jnp.float32))
    return o.reshape(B, T, GDN_V), new_buf, s_new


def gla_branch(p, s0, lp):
    B, T, _ = p.shape
    q, k, v, r, gd = _split(p, (GLA_QK, GLA_QK, GLA_V, GLA_V, GLA_RANK))
    hs = (B, T, GLA_HEADS)
    log_a = jax.nn.log_sigmoid((gd @ lp['gla_gate_up'] + lp['gla_gate_bias']).astype(jnp.float32)) / GLA_TAU
    o, s_new = gla_chunked(q.reshape(hs + (GLA_DK,)) * (GLA_DK ** -0.5), k.reshape(hs + (GLA_DK,)),
                           v.reshape(hs + (GLA_DV,)), log_a.reshape(hs + (GLA_DK,)), s0)
    o = rmsnorm(o, lp['gla_norm']) * jax.nn.silu(r.reshape(hs + (GLA_DV,)).astype(jnp.float32))
    return o.reshape(B, T, GLA_V), s_new


def dsa_branch(p, pos, attend):
    B, T, _ = p.shape
    q, k, v, qi, ki, wi = _split(p, (ATT_Q, ATT_KV, ATT_KV, IDX_HEADS * IDX_HD, IDX_HD, IDX_HEADS))
    q = partial_rope(q.reshape(B, T, ATT_HEADS, ATT_HD), pos)
    k = partial_rope(k.reshape(B, T, ATT_KV_HEADS, ATT_HD), pos)
    v = v.reshape(B, T, ATT_KV_HEADS, ATT_HD)
    qi = partial_rope(qi.reshape(B, T, IDX_HEADS, IDX_HD), pos)
    ki = partial_rope(ki.reshape(B, T, 1, IDX_HD), pos)[:, :, 0]
    o = attend(q, k, v, qi, ki, wi)
    return o.reshape(B, T, ATT_Q), (k, v, ki)


def rwkv_branch(p, shift_buf, s0, lp):
    B, T, _ = p.shape
    prev = jnp.concatenate([shift_buf.astype(p.dtype), p[:, :-1]], axis=1)
    pm = p + (prev - p) * lp['rwkv_mu']
    r, wd, k, v, ad, gd = _split(pm, (RWKV_W, RWKV_W_RANK, RWKV_W, RWKV_W, RWKV_A_RANK, RWKV_G_RANK))
    hs = (B, T, RWKV_HEADS, RWKV_HD)
    w_log = -jax.nn.softplus(-(lp['rwkv_w0'] + jnp.tanh(wd) @ lp['rwkv_w2']).astype(jnp.float32)) - 0.5
    decay = jnp.exp(-jnp.exp(w_log))
    a = jax.nn.sigmoid((lp['rwkv_a0'] + ad @ lp['rwkv_a2']).astype(jnp.float32))
    g = (jax.nn.sigmoid(gd) @ lp['rwkv_g2']).astype(jnp.float32)
    kf = k.astype(jnp.float32)
    kk = l2norm((kf * lp['rwkv_kk']).reshape(hs))
    kf = (kf * (1.0 + (a - 1.0) * lp['rwkv_ka'])).reshape(hs)
    rf = r.astype(jnp.float32).reshape(hs)
    vf = v.astype(jnp.float32).reshape(hs)
    y, s_new = rwkv7_scan(rf, decay.reshape(hs), kf, vf, -kk, kk * a.reshape(hs), s0)
    y = head_group_norm(y, lp['rwkv_ln_w'], lp['rwkv_ln_b'])
    bonus = (jnp.sum(rf * kf * lp['rwkv_rk'], axis=-1, keepdims=True) * vf).reshape(B, T, RWKV_W)
    return (y + bonus) * g, p[:, -1:], s_new


def token_mixers(h, pos, st, lp, attend):
    gdn_s, gdn_buf, gla_s, rwkv_s, rwkv_buf = st
    proj = _mm3(h, lp['w_in'])
    p_gdn, p_gla, p_att, p_rwkv, p_gate = _split(proj, IN_SPLITS)
    o_gdn, gdn_buf, gdn_s = gdn_branch(p_gdn, gdn_buf, gdn_s, lp)
    o_gla, gla_s = gla_branch(p_gla, gla_s, lp)
    o_att, rows = dsa_branch(p_att, pos, attend)
    o_rwkv, rwkv_buf, rwkv_s = rwkv_branch(p_rwkv, rwkv_buf, rwkv_s, lp)
    w_rows = _split(lp['w_branch'], BRANCH_SPLITS, axis=0)
    gate_cols = _split(p_gate, (D_MODEL,) * N_BRANCH)
    merged = None
    for o, w, gc in zip((o_gdn, o_gla, o_att, o_rwkv), w_rows, gate_cols):
        term = jax.nn.sigmoid(gc.astype(jnp.float32)) * _mm3(o.astype(h.dtype), w)
        merged = term if merged is None else merged + term
    out = _mm3(merged.astype(h.dtype), lp['w_out'])
    return out, rows + (gdn_s, gdn_buf, gla_s, rwkv_s, rwkv_buf)


def peer_ffn(h, wq, subkeys, u, v):
    B, T, D = h.shape
    n = B * T
    blk = min(PEER_BLOCK, n)
    pad = (-n) % blk
    xt = jnp.pad(h.reshape(n, D), [(0, pad), (0, 0)])
    q_all = _mm(xt, wq)

    def one(args):
        xb, qb = args
        q = qb.reshape(blk, PEER_HEADS, 2, PEER_DKEY // 2)
        s = jnp.einsum('thpd,hpnd->thpn', q, subkeys).astype(jnp.float32)
        sv, si = lax.top_k(s, PEER_TOPK)
        cand = (sv[:, :, 0, :, None] + sv[:, :, 1, None, :]).reshape(blk, PEER_HEADS, PEER_TOPK * PEER_TOPK)
        cv, ci = lax.top_k(cand, PEER_TOPK)
        e = (jnp.take_along_axis(si[:, :, 0], ci // PEER_TOPK, axis=-1) * PEER_NKEYS
             + jnp.take_along_axis(si[:, :, 1], ci % PEER_TOPK, axis=-1))
        gate = jax.nn.softmax(cv, axis=-1)
        act = jax.nn.gelu(jnp.einsum('thkd,td->thk', u[e], xb).astype(jnp.float32), approximate=False)
        return jnp.einsum('thk,thkd->td', (gate * act).astype(xb.dtype), v[e])

    y = lax.map(one, (xt.reshape(-1, blk, D), q_all.reshape(-1, blk, PEER_HEADS * PEER_DKEY)))
    return y.reshape(-1, D)[:n].reshape(B, T, D)


def trunk_layer(x, c, pos, st, lp, attend):
    B = x.shape[0]
    mod = (jax.nn.silu(c) @ lp['w_ada'] + lp['b_ada']).reshape(B, N_MOD, 1, D_MODEL)
    shift1, scale1, gate1, shift2, scale2, gate2 = [mod[:, i] for i in range(N_MOD)]
    h = rmsnorm(x, lp['norm1']) * (1.0 + scale1) + shift1
    mix, new_st = token_mixers(h, pos, st, lp, attend)
    x = x + gate1 * mix
    h = rmsnorm(x, lp['norm2']) * (1.0 + scale2) + shift2
    x = x + gate2 * peer_ffn(h, lp['peer_wq'], lp['peer_subkeys'], lp['peer_u'], lp['peer_v'])
    return x, new_st


def kernel(x_prompt, x_sample, cache_k, cache_v, cache_kidx, state_gdn, state_gdn_conv, state_gla, state_rwkv, state_rwkv_shift, page_table, c_prompt, c_sample, w_ada, b_ada, norm1, norm2, w_in, gdn_conv_w, gdn_A_log, gdn_dt_bias, gdn_norm, gla_gate_up, gla_gate_bias, gla_norm, rwkv_mu, rwkv_w0, rwkv_w2, rwkv_a0, rwkv_a2, rwkv_g2, rwkv_kk, rwkv_ka, rwkv_rk, rwkv_ln_w, rwkv_ln_b, w_branch, w_out, peer_wq, peer_subkeys, peer_u, peer_v, final_norm):
    Bp, T, _ = x_prompt.shape
    DS = x_sample.shape[1]
    depth = w_in.shape[0]
    past = page_table.shape[1] * PAGE_SIZE
    pos_p = jnp.arange(T, dtype=jnp.int32)
    pos_s = past + jnp.arange(DS, dtype=jnp.int32)
    st_p0 = (jnp.zeros((Bp, GDN_HEADS, GDN_DK, GDN_DV), jnp.float32),
             jnp.zeros((Bp, CONV_W - 1, GDN_CONV_COLS), x_prompt.dtype),
             jnp.zeros((Bp, GLA_HEADS, GLA_DK, GLA_DV), jnp.float32),
             jnp.zeros((Bp, RWKV_HEADS, RWKV_HD, RWKV_HD), jnp.float32),
             jnp.zeros((Bp, 1, RWKV_COLS), x_prompt.dtype))
    weights = {'w_ada': w_ada, 'b_ada': b_ada, 'norm1': norm1, 'norm2': norm2, 'w_in': w_in,
               'gdn_conv_w': gdn_conv_w, 'gdn_A_log': gdn_A_log, 'gdn_dt_bias': gdn_dt_bias, 'gdn_norm': gdn_norm,
               'gla_gate_up': gla_gate_up, 'gla_gate_bias': gla_gate_bias, 'gla_norm': gla_norm,
               'rwkv_mu': rwkv_mu, 'rwkv_w0': rwkv_w0, 'rwkv_w2': rwkv_w2, 'rwkv_a0': rwkv_a0, 'rwkv_a2': rwkv_a2,
               'rwkv_g2': rwkv_g2, 'rwkv_kk': rwkv_kk, 'rwkv_ka': rwkv_ka, 'rwkv_rk': rwkv_rk,
               'rwkv_ln_w': rwkv_ln_w, 'rwkv_ln_b': rwkv_ln_b, 'w_branch': w_branch, 'w_out': w_out,
               'peer_wq': peer_wq, 'peer_subkeys': peer_subkeys, 'peer_u': peer_u, 'peer_v': peer_v}
    xp, xs = x_prompt, x_sample
    out_p, out_s = [], []
    for l in range(depth):
        lp = {name: arr[l] for name, arr in weights.items()}
        xp, new_p = trunk_layer(xp, c_prompt, pos_p, st_p0, lp, dsa_prompt)
        attend_s = functools.partial(dsa_sample, cache_k=cache_k[l], cache_v=cache_v[l],
                                     cache_ki=cache_kidx[l], page_table=page_table)
        st_s0 = (state_gdn[l], state_gdn_conv[l], state_gla[l], state_rwkv[l], state_rwkv_shift[l])
        xs, new_s = trunk_layer(xs, c_sample, pos_s, st_s0, lp, attend_s)
        out_p.append(new_p)
        out_s.append(new_s)
    y_prompt = rmsnorm(xp, final_norm)
    y_sample = rmsnorm(xs, final_norm)
    k_p, v_p, kidx_p, gdn_p, gdn_conv_p, gla_p, rwkv_p, rwkv_shift_p = [jnp.stack(z) for z in zip(*out_p)]
    k_s, v_s, kidx_s, gdn_s, gdn_conv_s, gla_s, rwkv_s, rwkv_shift_s = [jnp.stack(z) for z in zip(*out_s)]
    return (y_prompt, y_sample, k_p, v_p, kidx_p, gdn_p, gdn_conv_p, gla_p, rwkv_p, rwkv_shift_p,
            k_s, v_s, kidx_s, gdn_s, gdn_conv_s, gla_s, rwkv_s, rwkv_shift_s)
```

```python
import math, functools
import jax, jax.numpy as jnp
from jax import lax
import numpy as np
from jax.experimental import pallas as pl
from jax.experimental.pallas import tpu as pltpu

D_MODEL = 4096
PAGE_SIZE = 128
N_MOD = 6
EPS = 1e-6
N_BRANCH = 4

GDN_HEADS = 8
GDN_DK = 128
GDN_DV = 128
CONV_W = 4
GDN_CHUNK = 64
GDN_QK = GDN_HEADS * GDN_DK
GDN_V = GDN_HEADS * GDN_DV
GDN_CONV_COLS = 2 * GDN_QK + GDN_V
GDN_COLS = GDN_CONV_COLS + GDN_V + 2 * GDN_HEADS

GLA_HEADS = 4
GLA_DK = 128
GLA_DV = 256
GLA_RANK = 16
GLA_TAU = 16.0
GLA_CHUNK = 64
GLA_QK = GLA_HEADS * GLA_DK
GLA_V = GLA_HEADS * GLA_DV
GLA_COLS = 2 * GLA_QK + 2 * GLA_V + GLA_RANK

ATT_HEADS = 8
ATT_KV_HEADS = 2
ATT_HD = 128
IDX_HEADS = 8
IDX_HD = 64
TOPK_MAX = 256
Q_BLOCK = 128
ROPE_THETA = 500000.0
ROPE_DIV = 4
ATT_Q = ATT_HEADS * ATT_HD
ATT_KV = ATT_KV_HEADS * ATT_HD
ATT_COLS = ATT_Q + 2 * ATT_KV + IDX_HEADS * IDX_HD + IDX_HD + IDX_HEADS

RWKV_HEADS = 16
RWKV_HD = 64
RWKV_W_RANK = 64
RWKV_A_RANK = 64
RWKV_G_RANK = 128
RWKV_GN_EPS = 64e-5
RWKV_W = RWKV_HEADS * RWKV_HD
RWKV_COLS = 3 * RWKV_W + RWKV_W_RANK + RWKV_A_RANK + RWKV_G_RANK

GATE_COLS = N_BRANCH * D_MODEL
IN_SPLITS = (GDN_COLS, GLA_COLS, ATT_COLS, RWKV_COLS, GATE_COLS)
IN_COLS = sum(IN_SPLITS)
BRANCH_SPLITS = (GDN_V, GLA_V, ATT_Q, RWKV_W)
MIX_WIDTH = sum(BRANCH_SPLITS)

PEER_HEADS = 8
PEER_NKEYS = 128
PEER_DKEY = 256
PEER_TOPK = 16
PEER_N = PEER_NKEYS * PEER_NKEYS
PEER_BLOCK = 64

VMEM_LIMIT_BYTES = 56 * 1024 * 1024
MM_TILE_M = 1024
MM_TILE_N = 512
BF16_SUBLANES = 16
LANES = 128
PEER_SEL_TILE = 256
PEER_TOKEN_TILE = 512
PEER_EXPERT_TILE = 512
PEER_OUT_TILE_N = 2048
PEER_OUT_TILE_K = 2048
DSA_Q_TILE = 128

HIGHEST = lax.Precision.HIGHEST
NEG_INF = float("-inf")
INT_MIN = -2 ** 31


def _mm_kernel(x_ref, w_ref, o_ref):
    o_ref[...] = jnp.dot(x_ref[...], w_ref[...], preferred_element_type=jnp.float32)


def _round_up(n, m):
    return (n + m - 1) // m * m


def _mm(x, w):
    M, K = x.shape
    N = w.shape[1]
    tm = min(MM_TILE_M, _round_up(M, BF16_SUBLANES))
    Mp = _round_up(M, tm)
    tn = MM_TILE_N
    Np = _round_up(N, tn)
    xb = x.astype(jnp.bfloat16)
    wb = w.astype(jnp.bfloat16)
    if Mp != M:
        xb = jnp.pad(xb, ((0, Mp - M), (0, 0)))
    if Np != N:
        wb = jnp.pad(wb, ((0, 0), (0, Np - N)))
    out = pl.pallas_call(
        _mm_kernel,
        grid=(Np // tn, Mp // tm),
        in_specs=[pl.BlockSpec((tm, K), lambda j, i: (i, 0)),
                  pl.BlockSpec((K, tn), lambda j, i: (0, j))],
        out_specs=pl.BlockSpec((tm, tn), lambda j, i: (i, j)),
        out_shape=jax.ShapeDtypeStruct((Mp, Np), jnp.float32),
        compiler_params=pltpu.CompilerParams(
            dimension_semantics=("parallel", "parallel"),
            vmem_limit_bytes=VMEM_LIMIT_BYTES),
        name="mm",
    )(xb, wb)
    return out[:M, :N]


def _mm3(x, w):
    B, T, K = x.shape
    return _mm(x.reshape(B * T, K), w).reshape(B, T, w.shape[1])


def _extract_top(x, order, n_out):
    outs, picks = [], []
    rank = jnp.full(x.shape, float(n_out), jnp.float32)
    for r in range(n_out):
        m = jnp.max(x, axis=0, keepdims=True)
        c = jnp.min(jnp.where(x == m, order, jnp.int32(2 ** 30)), axis=0, keepdims=True)
        hit = order == c
        x = jnp.where(hit, NEG_INF, x)
        rank = jnp.where(hit, float(r), rank)
        outs.append(m)
        picks.append(c)
    return outs, picks, rank


def _peer_select_kernel(q_ref, sk_ref, ea_ref, rk_ref, sv_ref):
    Tt = q_ref.shape[0]
    NK, DK2 = sk_ref.shape[2], sk_ref.shape[3]
    key_iota = lax.broadcasted_iota(jnp.int32, (NK, Tt), 0)
    scores, tops, ranks = [], [], []
    for p in range(2):
        s_t = lax.dot_general(sk_ref[0, p], q_ref[:, p * DK2:(p + 1) * DK2], (((1,), (1,)), ((), ())),
                              precision=HIGHEST, preferred_element_type=jnp.float32)
        sv, _, rank = _extract_top(s_t, key_iota, PEER_TOPK)
        for r in range(PEER_TOPK):
            sv_ref[p, r:r + 1, :] = sv[r]
        scores.append(s_t)
        tops.append(sv)
        ranks.append(rank)
    row8 = lax.broadcasted_iota(jnp.int32, (8, Tt), 0)
    row16 = lax.broadcasted_iota(jnp.int32, (PEER_TOPK, Tt), 0)
    strips = [sv_ref[0] + tops[1][0]]
    order = [row16 * PEER_TOPK]
    for b in range(1, 8):
        n_a = PEER_TOPK // (b + 1)
        strips.append(jnp.where(row8 < n_a, sv_ref[0, 0:8, :] + tops[1][b], NEG_INF))
        order.append(row8 * PEER_TOPK + b)
    strips.append(tops[0][0] + sv_ref[1, 8:16, :])
    order.append(row8 + 8)
    cv, picks, _ = _extract_top(jnp.concatenate(strips, axis=0), jnp.concatenate(order, axis=0), PEER_TOPK)
    z = jnp.ones_like(cv[0])
    for r in range(1, PEER_TOPK):
        z = z + jnp.exp(cv[r] - cv[0])
    n_b = jnp.zeros((NK, Tt), jnp.float32)
    for c in picks:
        n_b = n_b + jnp.where(ranks[0] == jnp.right_shift(c, 4).astype(jnp.float32), 1.0, 0.0)
    rk_ref[0] = n_b
    rk_ref[1] = ranks[1]
    ea_ref[0] = jnp.exp(scores[0] - tops[0][0])
    ea_ref[1] = jnp.exp(scores[1] - tops[1][0]) / z


def _peer_select(q, subkeys, tile):
    n = q.shape[0]
    H, _, NK, DK2 = subkeys.shape
    assert PEER_TOPK == 16 and NK % 8 == 0 and n % tile == 0
    tab = jax.ShapeDtypeStruct((2 * H, NK, n), jnp.float32)
    return pl.pallas_call(
        _peer_select_kernel,
        grid=(n // tile, H),
        in_specs=[pl.BlockSpec((tile, 2 * DK2), lambda i, h: (i, h)),
                  pl.BlockSpec((1, 2, NK, DK2), lambda i, h: (h, 0, 0, 0))],
        out_specs=[pl.BlockSpec((2, NK, tile), lambda i, h: (h, 0, i)),
                   pl.BlockSpec((2, NK, tile), lambda i, h: (h, 0, i))],
        out_shape=[tab, tab],
        scratch_shapes=[pltpu.VMEM((2, PEER_TOPK, tile), jnp.float32)],
        compiler_params=pltpu.CompilerParams(dimension_semantics=("parallel", "parallel"),
                                             vmem_limit_bytes=VMEM_LIMIT_BYTES),
        name="peer_select",
    )(q, subkeys)


def _gelu_exact(x):
    return 0.5 * x * (1.0 + lax.erf(x * (2.0 ** -0.5)))


def _peer_gate_kernel(h_ref, u_ref, ea_ref, rk_ref, g_ref):
    j = pl.program_id(1)
    Tt = h_ref.shape[0]
    Et = u_ref.shape[0]
    H2, NK, _ = ea_ref.shape
    n_i0 = Et // NK
    act_t = lax.dot_general(u_ref[...], h_ref[...], (((1,), (1,)), ((), ())),
                            preferred_element_type=jnp.float32)
    for r in range(n_i0):
        i0 = j * n_i0 + r
        rows = slice(r * NK, (r + 1) * NK)
        a0 = [ea_ref[2 * h, pl.ds(i0, 1), :] for h in range(H2 // 2)]
        n_b = [rk_ref[2 * h, pl.ds(i0, 1), :] for h in range(H2 // 2)]
        for c in range(Tt // LANES):
            cols = slice(c * LANES, (c + 1) * LANES)
            acc = None
            for h in range(H2 // 2):
                keep = rk_ref[2 * h + 1, :, cols] < n_b[h][:, cols]
                term = jnp.where(keep, ea_ref[2 * h + 1, :, cols], 0.0) * a0[h][:, cols]
                acc = term if acc is None else acc + term
            g_ref[rows, cols] = (acc * _gelu_exact(act_t[rows, cols])).astype(jnp.bfloat16)


def _peer_gate(h, u, ea, rk, token_tile, expert_tile):
    n, D = h.shape
    E = u.shape[0]
    H2, NK, _ = ea.shape
    assert E == NK * NK and expert_tile % NK == 0 and E % expert_tile == 0
    assert n % token_tile == 0 and token_tile % LANES == 0
    const = pl.Buffered(1)
    return pl.pallas_call(
        _peer_gate_kernel,
        grid=(n // token_tile, E // expert_tile),
        in_specs=[pl.BlockSpec((token_tile, D), lambda i, j: (i, 0), pipeline_mode=const),
                  pl.BlockSpec((expert_tile, D), lambda i, j: (j, 0)),
                  pl.BlockSpec((H2, NK, token_tile), lambda i, j: (0, 0, i), pipeline_mode=const),
                  pl.BlockSpec((H2, NK, token_tile), lambda i, j: (0, 0, i), pipeline_mode=const)],
        out_specs=pl.BlockSpec((expert_tile, token_tile), lambda i, j: (j, i)),
        out_shape=jax.ShapeDtypeStruct((E, n), jnp.bfloat16),
        compiler_params=pltpu.CompilerParams(dimension_semantics=("parallel", "parallel"),
                                             vmem_limit_bytes=VMEM_LIMIT_BYTES),
        name="peer_gate",
    )(h, u, ea, rk)


def _mm_tn_kernel(a_ref, b_ref, o_ref, acc_ref):
    k = pl.program_id(2)
    part = lax.dot_general(a_ref[...], b_ref[...], (((0,), (0,)), ((), ())), preferred_element_type=jnp.float32)

    @pl.when(k == 0)
    def _():
        acc_ref[...] = part

    @pl.when(k > 0)
    def _():
        acc_ref[...] += part

    @pl.when(k == pl.num_programs(2) - 1)
    def _():
        o_ref[...] = acc_ref[...]


def _mm_tn(a_t, b, tm, tn, tk):
    K, M = a_t.shape
    N = b.shape[1]
    assert M % tm == 0 and N % tn == 0 and K % tk == 0
    return pl.pallas_call(
        _mm_tn_kernel,
        grid=(M // tm, N // tn, K // tk),
        in_specs=[pl.BlockSpec((tk, tm), lambda i, j, k: (k, i)),
                  pl.BlockSpec((tk, tn), lambda i, j, k: (k, j))],
        out_specs=pl.BlockSpec((tm, tn), lambda i, j, k: (i, j)),
        out_shape=jax.ShapeDtypeStruct((M, N), jnp.float32),
        scratch_shapes=[pltpu.VMEM((tm, tn), jnp.float32)],
        compiler_params=pltpu.CompilerParams(dimension_semantics=("parallel", "parallel", "arbitrary"),
                                             vmem_limit_bytes=VMEM_LIMIT_BYTES),
        name="mm_tn",
    )(a_t, b)


def peer_ffn(h, wq, subkeys, u_bf16, v_bf16):
    B, T, D = h.shape
    n = B * T
    sel_tile = min(PEER_SEL_TILE, _round_up(n, 128))
    tok_tile = min(PEER_TOKEN_TILE, _round_up(n, 128))
    n_pad = _round_up(n, max(sel_tile, tok_tile))
    xt = h.reshape(n, D).astype(jnp.bfloat16)
    if n_pad != n:
        xt = jnp.pad(xt, ((0, n_pad - n), (0, 0)))
    q = _mm(xt, wq)
    ea, rk = _peer_select(q, subkeys, sel_tile)
    g_t = _peer_gate(xt, u_bf16, ea, rk, tok_tile, PEER_EXPERT_TILE)
    y = _mm_tn(g_t, v_bf16, tok_tile, min(D, PEER_OUT_TILE_N), PEER_OUT_TILE_K)
    return y[:n].reshape(B, T, D)


def _count_rows(mask):
    return jnp.sum(jnp.where(mask, 1.0, 0.0), axis=1, keepdims=True)


def _dsa_prompt_kernel(q_ref, k_ref, v_ref, qi_ref, ki_ref, wi_ref, o_ref, *, ktop):
    Tq = q_ref.shape[1]
    T = k_ref.shape[1]
    n_bits = int(T - 1).bit_length()
    ki = ki_ref[0]
    wi = wi_ref[0]
    score = jnp.zeros((Tq, T), jnp.float32)
    for h in range(IDX_HEADS):
        d = lax.dot_general(qi_ref[0, :, h * IDX_HD:(h + 1) * IDX_HD], ki, (((1,), (1,)), ((), ())),
                            precision=HIGHEST, preferred_element_type=jnp.float32)
        score = score + wi[:, h:h + 1] * jnp.maximum(d, 0.0)
    bits = lax.bitcast_convert_type(score, jnp.int32)
    key = jnp.where(bits < 0, bits ^ jnp.int32(0x7FFFFFFF), bits)
    key = jnp.where(bits == jnp.int32(INT_MIN), 0, key)
    qpos = pl.program_id(1) * Tq + lax.broadcasted_iota(jnp.int32, (Tq, T), 0)
    col = lax.broadcasted_iota(jnp.int32, (Tq, T), 1)
    allowed = col <= qpos
    key = jnp.where(allowed, key, jnp.int32(INT_MIN))
    kf = jnp.float32(ktop)

    tau = jnp.where(_count_rows(key >= 0) >= kf, jnp.int32(0), jnp.int32(INT_MIN))

    def tau_step(it, tau):
        cand = tau | jnp.left_shift(jnp.int32(1), 30 - it)
        return jnp.where(_count_rows(key >= cand) >= kf, cand, tau)

    tau = lax.fori_loop(0, 31, tau_step, tau)
    need = kf - _count_rows(key > tau)
    tie = (key == tau) & allowed

    def tie_step(it, p):
        cand = p | jnp.left_shift(jnp.int32(1), n_bits - 1 - it)
        return jnp.where(_count_rows(tie & (col < cand)) < need, cand, p)

    p = lax.fori_loop(0, n_bits, tie_step, jnp.zeros((Tq, 1), jnp.int32))
    sel = allowed & ((key > tau) | (tie & (col <= p)))

    group = ATT_HEADS // ATT_KV_HEADS
    scale = ATT_HD ** -0.5
    for g in range(ATT_KV_HEADS):
        kg = k_ref[0, :, g * ATT_HD:(g + 1) * ATT_HD].astype(jnp.bfloat16)
        vg = v_ref[0, :, g * ATT_HD:(g + 1) * ATT_HD].astype(jnp.bfloat16)
        for hh in range(group):
            lo = (g * group + hh) * ATT_HD
            qh = q_ref[0, :, lo:lo + ATT_HD].astype(jnp.bfloat16)
            logits = lax.dot_general(qh, kg, (((1,), (1,)), ((), ())), preferred_element_type=jnp.float32) * scale
            logits = jnp.where(sel, logits, NEG_INF)
            e = jnp.exp(logits - jnp.max(logits, axis=1, keepdims=True))
            out = jnp.dot(e.astype(jnp.bfloat16), vg, preferred_element_type=jnp.float32)
            o_ref[0, :, lo:lo + ATT_HD] = out / jnp.sum(e, axis=1, keepdims=True)


def dsa_prompt(q, k, v, qi, ki, wi):
    B, T = q.shape[:2]
    ktop = min(TOPK_MAX, T // 4)
    tq = min(DSA_Q_TILE, T)
    flat = lambda a: a.reshape(B, T, -1)
    o = pl.pallas_call(
        functools.partial(_dsa_prompt_kernel, ktop=ktop),
        grid=(B, T // tq),
        in_specs=[pl.BlockSpec((1, tq, ATT_Q), lambda b, i: (b, i, 0)),
                  pl.BlockSpec((1, T, ATT_KV), lambda b, i: (b, 0, 0)),
                  pl.BlockSpec((1, T, ATT_KV), lambda b, i: (b, 0, 0)),
                  pl.BlockSpec((1, tq, IDX_HEADS * IDX_HD), lambda b, i: (b, i, 0)),
                  pl.BlockSpec((1, T, IDX_HD), lambda b, i: (b, 0, 0)),
                  pl.BlockSpec((1, tq, IDX_HEADS), lambda b, i: (b, i, 0))],
        out_specs=pl.BlockSpec((1, tq, ATT_Q), lambda b, i: (b, i, 0)),
        out_shape=jax.ShapeDtypeStruct((B, T, ATT_Q), jnp.float32),
        compiler_params=pltpu.CompilerParams(dimension_semantics=("parallel", "parallel"),
                                             vmem_limit_bytes=VMEM_LIMIT_BYTES),
        name="dsa_prompt",
    )(flat(q), flat(k), flat(v), flat(qi), ki, wi)
    return o.reshape(q.shape)


def _split(a, sizes, axis=-1):
    return jnp.split(a, [int(s) for s in np.cumsum(sizes)[:-1]], axis=axis)


def rmsnorm(x, gain):
    xf = x.astype(jnp.float32)
    y = xf * lax.rsqrt(jnp.mean(xf * xf, axis=-1, keepdims=True) + EPS)
    return (y * gain.astype(jnp.float32)).astype(x.dtype)


def l2norm(x):
    xf = x.astype(jnp.float32)
    return xf * lax.rsqrt(jnp.sum(xf * xf, axis=-1, keepdims=True) + EPS)


def partial_rope(x, pos):
    rd = x.shape[-1] // ROPE_DIV
    half = rd // 2
    inv_freq = ROPE_THETA ** (-jnp.arange(half, dtype=jnp.float32) * 2.0 / rd)
    ang = pos.astype(jnp.float32)[:, None] * inv_freq[None, :]
    cos = jnp.cos(ang)[:, None, :]
    sin = jnp.sin(ang)[:, None, :]
    xf = x[..., :rd].astype(jnp.float32)
    x1, x2 = xf[..., :half], xf[..., half:]
    rot = jnp.concatenate([x1 * cos - x2 * sin, x2 * cos + x1 * sin], axis=-1)
    return jnp.concatenate([rot.astype(x.dtype), x[..., rd:]], axis=-1)


def causal_conv(x, buf, w):
    T = x.shape[1]
    xp = jnp.concatenate([buf.astype(x.dtype), x], axis=1)
    y = xp[:, 0:T] * w[0]
    for i in range(1, CONV_W):
        y = y + xp[:, i:i + T] * w[i]
    return jax.nn.silu(y), xp[:, T:]


def _to_chunks(a, C):
    B, T = a.shape[:2]
    pad = (-T) % C
    a = jnp.pad(a.astype(jnp.float32), [(0, 0), (0, pad)] + [(0, 0)] * (a.ndim - 2))
    n = (T + pad) // C
    a = a.reshape((B, n, C) + a.shape[2:])
    return jnp.transpose(a, (1, 0, 3, 2) + tuple(range(4, a.ndim)))


def _from_chunks(o, T):
    n, B, H, C, X = o.shape
    return jnp.transpose(o, (1, 0, 3, 2, 4)).reshape(B, n * C, H, X)[:, :T]


def gated_delta_rule(q, k, v, beta, g, s0):
    T = q.shape[1]
    DV = v.shape[-1]
    C = min(GDN_CHUNK, T)
    qc, kc, vc = _to_chunks(q, C), _to_chunks(k, C), _to_chunks(v, C)
    bc = _to_chunks(beta, C)
    gc = jnp.cumsum(_to_chunks(g, C), axis=-1)
    tri = jnp.tril(jnp.ones((C, C), bool))
    strict = jnp.tril(jnp.ones((C, C), bool), -1)
    decay = jnp.exp(jnp.where(tri, gc[..., :, None] - gc[..., None, :], -jnp.inf))
    kb = kc * bc[..., None]
    a_mat = jnp.where(strict, jnp.einsum('nbhik,nbhjk->nbhij', kb, kc) * decay, 0.0) + jnp.eye(C, dtype=jnp.float32)
    rhs = jnp.concatenate([vc * bc[..., None], kb * jnp.exp(gc)[..., None]], axis=-1)
    sol = lax.linalg.triangular_solve(a_mat, rhs, left_side=True, lower=True, unit_diagonal=True)
    u, w = sol[..., :DV], sol[..., DV:]
    qk = jnp.einsum('nbhik,nbhjk->nbhij', qc, kc) * decay
    q_dec = qc * jnp.exp(gc)[..., None]
    k_dec = kc * jnp.exp(gc[..., -1:] - gc)[..., None]
    g_last = jnp.exp(gc[..., -1])

    def step(S, xs):
        u_n, w_n, qk_n, qd_n, kd_n, gl_n = xs
        v_new = u_n - jnp.einsum('bhck,bhkv->bhcv', w_n, S)
        o = jnp.einsum('bhck,bhkv->bhcv', qd_n, S) + jnp.einsum('bhij,bhjv->bhiv', qk_n, v_new)
        S = S * gl_n[..., None, None] + jnp.einsum('bhck,bhcv->bhkv', kd_n, v_new)
        return S, o

    S, o = lax.scan(step, s0.astype(jnp.float32), (u, w, qk, q_dec, k_dec, g_last))
    return _from_chunks(o, T), S


def gla_chunked(q, k, v, log_a, s0):
    T = q.shape[1]
    C = min(GLA_CHUNK, T)
    qc, kc, vc = _to_chunks(q, C), _to_chunks(k, C), _to_chunks(v, C)
    bc = jnp.cumsum(_to_chunks(log_a, C), axis=-2)
    tri = jnp.tril(jnp.ones((C, C), bool))[..., None]

    def step(S, xs):
        q_n, k_n, v_n, b_n = xs
        rel = jnp.exp(jnp.where(tri, b_n[..., :, None, :] - b_n[..., None, :, :], -jnp.inf))
        att = jnp.einsum('bhik,bhijk,bhjk->bhij', q_n, rel, k_n)
        o = jnp.einsum('bhik,bhkv->bhiv', q_n * jnp.exp(b_n), S) + jnp.einsum('bhij,bhjv->bhiv', att, v_n)
        b_last = b_n[..., -1:, :]
        S = S * jnp.exp(b_last)[..., 0, :, None] + jnp.einsum('bhck,bhcv->bhkv', k_n * jnp.exp(b_last - b_n), v_n)
        return S, o

    S, o = lax.scan(step, s0.astype(jnp.float32), (qc, kc, vc, bc))
    return _from_chunks(o, T), S


def rwkv7_scan(r, w, k, v, a, b, s0):
    def step(S, xs):
        r_t, w_t, k_t, v_t, a_t, b_t = xs
        sa = jnp.einsum('bhij,bhj->bhi', S, a_t)
        S = S * w_t[:, :, None, :] + sa[..., None] * b_t[:, :, None, :] + v_t[..., None] * k_t[:, :, None, :]
        return S, jnp.einsum('bhij,bhj->bhi', S, r_t)
    xs = tuple(jnp.moveaxis(t.astype(jnp.float32), 1, 0) for t in (r, w, k, v, a, b))
    S, y = lax.scan(step, s0.astype(jnp.float32), xs)
    return jnp.moveaxis(y, 0, 1), S


def head_group_norm(y, w, b):
    B, T, H, N = y.shape
    mu = jnp.mean(y, axis=-1, keepdims=True)
    var = jnp.mean(jnp.square(y - mu), axis=-1, keepdims=True)
    yn = ((y - mu) * lax.rsqrt(var + RWKV_GN_EPS)).reshape(B, T, H * N)
    return yn * w.astype(jnp.float32) + b.astype(jnp.float32)


def indexer_scores(qi, wi, ki):
    dots = jnp.einsum('bqhd,bsd->bqhs', qi, ki).astype(jnp.float32)
    return jnp.einsum('bqh,bqhs->bqs', wi.astype(jnp.float32), jax.nn.relu(dots))


def select_keys(scores, qpos, ktop):
    S = scores.shape[-1]
    allowed = jnp.arange(S)[None, None, :] <= qpos[None, :, None]
    _, idx = lax.top_k(jnp.where(allowed, scores, -jnp.inf), ktop)
    return idx, idx <= qpos[None, :, None]


def sparse_attend(q, k_sel, v_sel, valid):
    B, Q, HQ, HD = q.shape
    qg = q.reshape(B, Q, ATT_KV_HEADS, HQ // ATT_KV_HEADS, HD)
    logits = jnp.einsum('bqhgd,bqkhd->bqhgk', qg, k_sel).astype(jnp.float32) * (HD ** -0.5)
    logits = jnp.where(valid[:, :, None, None, :], logits, -jnp.inf)
    p = jax.nn.softmax(logits, axis=-1).astype(v_sel.dtype)
    return jnp.einsum('bqhgk,bqkhd->bqhgd', p, v_sel).reshape(B, Q, HQ, HD)


def dsa_sample(q, k, v, qi, ki, wi, cache_k, cache_v, cache_ki, page_table):
    DB, DS = q.shape[:2]
    past = page_table.shape[1] * PAGE_SIZE
    ktop = min(TOPK_MAX, (past + DS) // 4)
    ki_past = cache_ki[page_table].reshape(DB, past, IDX_HD).astype(ki.dtype)
    ki_all = jnp.concatenate([ki_past, ki], axis=1)
    qpos = past + jnp.arange(DS)
    idx, valid = select_keys(indexer_scores(qi, wi, ki_all), qpos, ktop)
    is_new = idx >= past
    pidx = jnp.minimum(idx, past - 1)
    phys = jax.vmap(lambda pt, i: pt[i])(page_table, pidx // PAGE_SIZE)
    off = pidx % PAGE_SIZE
    nidx = jnp.clip(idx - past, 0, DS - 1)
    gather = jax.vmap(lambda rows, i: rows[i])
    sel = lambda cache, new: jnp.where(is_new[..., None, None], gather(new, nidx), cache[phys, off].astype(new.dtype))
    return sparse_attend(q, sel(cache_k, k), sel(cache_v, v), valid)


def gdn_branch(p, conv_buf, s0, lp):
    B, T, _ = p.shape
    conv_in, z, b_raw, a_raw = _split(p, (GDN_CONV_COLS, GDN_V, GDN_HEADS, GDN_HEADS))
    conv_out, new_buf = causal_conv(conv_in, conv_buf, lp['gdn_conv_w'])
    q, k, v = _split(conv_out, (GDN_QK, GDN_QK, GDN_V))
    q = l2norm(q.reshape(B, T, GDN_HEADS, GDN_DK)) * (GDN_DK ** -0.5)
    k = l2norm(k.reshape(B, T, GDN_HEADS, GDN_DK))
    v = v.reshape(B, T, GDN_HEADS, GDN_DV)
    beta = jax.nn.sigmoid(b_raw.astype(jnp.float32))
    g = -jnp.exp(lp['gdn_A_log'].astype(jnp.float32)) * jax.nn.softplus(a_raw.astype(jnp.float32) + lp['gdn_dt_bias'].astype(jnp.float32))
    o, s_new = gated_delta_rule(q, k, v, beta, g, s0)
    o = rmsnorm(o, lp['gdn_norm']) * jax.nn.silu(z.reshape(B, T, GDN_HEADS, GDN_DV).astype(jnp.float32))
    return o.reshape(B, T, GDN_V), new_buf, s_new


def gla_branch(p, s0, lp):
    B, T, _ = p.shape
    q, k, v, r, gd = _split(p, (GLA_QK, GLA_QK, GLA_V, GLA_V, GLA_RANK))
    hs = (B, T, GLA_HEADS)
    log_a = jax.nn.log_sigmoid((gd @ lp['gla_gate_up'] + lp['gla_gate_bias']).astype(jnp.float32)) / GLA_TAU
    o, s_new = gla_chunked(q.reshape(hs + (GLA_DK,)) * (GLA_DK ** -0.5), k.reshape(hs + (GLA_DK,)),
                           v.reshape(hs + (GLA_DV,)), log_a.reshape(hs + (GLA_DK,)), s0)
    o = rmsnorm(o, lp['gla_norm']) * jax.nn.silu(r.reshape(hs + (GLA_DV,)).astype(jnp.float32))
    return o.reshape(B, T, GLA_V), s_new


def dsa_branch(p, pos, attend):
    B, T, _ = p.shape
    q, k, v, qi, ki, wi = _split(p, (ATT_Q, ATT_KV, ATT_KV, IDX_HEADS * IDX_HD, IDX_HD, IDX_HEADS))
    q = partial_rope(q.reshape(B, T, ATT_HEADS, ATT_HD), pos)
    k = partial_rope(k.reshape(B, T, ATT_KV_HEADS, ATT_HD), pos)
    v = v.reshape(B, T, ATT_KV_HEADS, ATT_HD)
    qi = partial_rope(qi.reshape(B, T, IDX_HEADS, IDX_HD), pos)
    ki = partial_rope(ki.reshape(B, T, 1, IDX_HD), pos)[:, :, 0]
    o = attend(q, k, v, qi, ki, wi)
    return o.reshape(B, T, ATT_Q), (k, v, ki)


def rwkv_branch(p, shift_buf, s0, lp):
    B, T, _ = p.shape
    prev = jnp.concatenate([shift_buf.astype(p.dtype), p[:, :-1]], axis=1)
    pm = p + (prev - p) * lp['rwkv_mu']
    r, wd, k, v, ad, gd = _split(pm, (RWKV_W, RWKV_W_RANK, RWKV_W, RWKV_W, RWKV_A_RANK, RWKV_G_RANK))
    hs = (B, T, RWKV_HEADS, RWKV_HD)
    w_log = -jax.nn.softplus(-(lp['rwkv_w0'] + jnp.tanh(wd) @ lp['rwkv_w2']).astype(jnp.float32)) - 0.5
    decay = jnp.exp(-jnp.exp(w_log))
    a = jax.nn.sigmoid((lp['rwkv_a0'] + ad @ lp['rwkv_a2']).astype(jnp.float32))
    g = (jax.nn.sigmoid(gd) @ lp['rwkv_g2']).astype(jnp.float32)
    kf = k.astype(jnp.float32)
    kk = l2norm((kf * lp['rwkv_kk']).reshape(hs))
    kf = (kf * (1.0 + (a - 1.0) * lp['rwkv_ka'])).reshape(hs)
    rf = r.astype(jnp.float32).reshape(hs)
    vf = v.astype(jnp.float32).reshape(hs)
    y, s_new = rwkv7_scan(rf, decay.reshape(hs), kf, vf, -kk, kk * a.reshape(hs), s0)
    y = head_group_norm(y, lp['rwkv_ln_w'], lp['rwkv_ln_b'])
    bonus = (jnp.sum(rf * kf * lp['rwkv_rk'], axis=-1, keepdims=True) * vf).reshape(B, T, RWKV_W)
    return (y + bonus) * g, p[:, -1:], s_new


def token_mixers(h, pos, st, lp, attend):
    gdn_s, gdn_buf, gla_s, rwkv_s, rwkv_buf = st
    proj = _mm3(h, lp['w_in'])
    p_gdn, p_gla, p_att, p_rwkv, p_gate = _split(proj, IN_SPLITS)
    o_gdn, gdn_buf, gdn_s = gdn_branch(p_gdn, gdn_buf, gdn_s, lp)
    o_gla, gla_s = gla_branch(p_gla, gla_s, lp)
    o_att, rows = dsa_branch(p_att, pos, attend)
    o_rwkv, rwkv_buf, rwkv_s = rwkv_branch(p_rwkv, rwkv_buf, rwkv_s, lp)
    w_rows = _split(lp['w_branch'], BRANCH_SPLITS, axis=0)
    gate_cols = _split(p_gate, (D_MODEL,) * N_BRANCH)
    merged = None
    for o, w, gc in zip((o_gdn, o_gla, o_att, o_rwkv), w_rows, gate_cols):
        term = jax.nn.sigmoid(gc.astype(jnp.float32)) * _mm3(o.astype(h.dtype), w)
        merged = term if merged is None else merged + term
    out = _mm3(merged.astype(h.dtype), lp['w_out'])
    return out, rows + (gdn_s, gdn_buf, gla_s, rwkv_s, rwkv_buf)


def trunk_layer(x, c, pos, st, lp, attend):
    B = x.shape[0]
    mod = (jax.nn.silu(c) @ lp['w_ada'] + lp['b_ada']).reshape(B, N_MOD, 1, D_MODEL)
    shift1, scale1, gate1, shift2, scale2, gate2 = [mod[:, i] for i in range(N_MOD)]
    h = rmsnorm(x, lp['norm1']) * (1.0 + scale1) + shift1
    mix, new_st = token_mixers(h, pos, st, lp, attend)
    x = x + gate1 * mix
    h = rmsnorm(x, lp['norm2']) * (1.0 + scale2) + shift2
    x = x + gate2 * peer_ffn(h, lp['peer_wq'], lp['peer_subkeys'], lp['peer_u'], lp['peer_v'])
    return x, new_st


def kernel(x_prompt, x_sample, cache_k, cache_v, cache_kidx, state_gdn, state_gdn_conv, state_gla, state_rwkv, state_rwkv_shift, page_table, c_prompt, c_sample, w_ada, b_ada, norm1, norm2, w_in, gdn_conv_w, gdn_A_log, gdn_dt_bias, gdn_norm, gla_gate_up, gla_gate_bias, gla_norm, rwkv_mu, rwkv_w0, rwkv_w2, rwkv_a0, rwkv_a2, rwkv_g2, rwkv_kk, rwkv_ka, rwkv_rk, rwkv_ln_w, rwkv_ln_b, w_branch, w_out, peer_wq, peer_subkeys, peer_u, peer_v, final_norm):
    Bp, T, _ = x_prompt.shape
    DS = x_sample.shape[1]
    depth = w_in.shape[0]
    past = page_table.shape[1] * PAGE_SIZE
    pos_p = jnp.arange(T, dtype=jnp.int32)
    pos_s = past + jnp.arange(DS, dtype=jnp.int32)
    st_p0 = (jnp.zeros((Bp, GDN_HEADS, GDN_DK, GDN_DV), jnp.float32),
             jnp.zeros((Bp, CONV_W - 1, GDN_CONV_COLS), x_prompt.dtype),
             jnp.zeros((Bp, GLA_HEADS, GLA_DK, GLA_DV), jnp.float32),
             jnp.zeros((Bp, RWKV_HEADS, RWKV_HD, RWKV_HD), jnp.float32),
             jnp.zeros((Bp, 1, RWKV_COLS), x_prompt.dtype))
    weights = {'w_ada': w_ada, 'b_ada': b_ada, 'norm1': norm1, 'norm2': norm2, 'w_in': w_in,
               'gdn_conv_w': gdn_conv_w, 'gdn_A_log': gdn_A_log, 'gdn_dt_bias': gdn_dt_bias, 'gdn_norm': gdn_norm,
               'gla_gate_up': gla_gate_up, 'gla_gate_bias': gla_gate_bias, 'gla_norm': gla_norm,
               'rwkv_mu': rwkv_mu, 'rwkv_w0': rwkv_w0, 'rwkv_w2': rwkv_w2, 'rwkv_a0': rwkv_a0, 'rwkv_a2': rwkv_a2,
               'rwkv_g2': rwkv_g2, 'rwkv_kk': rwkv_kk, 'rwkv_ka': rwkv_ka, 'rwkv_rk': rwkv_rk,
               'rwkv_ln_w': rwkv_ln_w, 'rwkv_ln_b': rwkv_ln_b, 'w_branch': w_branch, 'w_out': w_out,
               'peer_wq': peer_wq, 'peer_subkeys': peer_subkeys, 'peer_u': peer_u, 'peer_v': peer_v}
    xp, xs = x_prompt, x_sample
    out_p, out_s = [], []
    for l in range(depth):
        lp = {name: arr[l] for name, arr in weights.items()}
        lp['peer_u'] = lp['peer_u'].astype(jnp.bfloat16)
        lp['peer_v'] = lp['peer_v'].astype(jnp.bfloat16)
        xp, new_p = trunk_layer(xp, c_prompt, pos_p, st_p0, lp, dsa_prompt)
        attend_s = functools.partial(dsa_sample, cache_k=cache_k[l], cache_v=cache_v[l],
                                     cache_ki=cache_kidx[l], page_table=page_table)
        st_s0 = (state_gdn[l], state_gdn_conv[l], state_gla[l], state_rwkv[l], state_rwkv_shift[l])
        xs, new_s = trunk_layer(xs, c_sample, pos_s, st_s0, lp, attend_s)
        out_p.append(new_p)
        out_s.append(new_s)
    y_prompt = rmsnorm(xp, final_norm)
    y_sample = rmsnorm(xs, final_norm)
    k_p, v_p, kidx_p, gdn_p, gdn_conv_p, gla_p, rwkv_p, rwkv_shift_p = [jnp.stack(z) for z in zip(*out_p)]
    k_s, v_s, kidx_s, gdn_s, gdn_conv_s, gla_s, rwkv_s, rwkv_shift_s = [jnp.stack(z) for z in zip(*out_s)]
    return (y_prompt, y_sample, k_p, v_p, kidx_p, gdn_p, gdn_conv_p, gla_p, rwkv_p, rwkv_shift_p,
            k_s, v_s, kidx_s, gdn_s, gdn_conv_s, gla_s, rwkv_s, rwkv_shift_s)
```

```python
import math, functools
import jax, jax.numpy as jnp
from jax import lax
import numpy as np
from jax.experimental import pallas as pl
from jax.experimental.pallas import tpu as pltpu

D_MODEL = 4096
PAGE_SIZE = 128
N_MOD = 6
EPS = 1e-6
N_BRANCH = 4

GDN_HEADS = 8
GDN_DK = 128
GDN_DV = 128
CONV_W = 4
GDN_CHUNK = 64
GDN_QK = GDN_HEADS * GDN_DK
GDN_V = GDN_HEADS * GDN_DV
GDN_CONV_COLS = 2 * GDN_QK + GDN_V
GDN_COLS = GDN_CONV_COLS + GDN_V + 2 * GDN_HEADS

GLA_HEADS = 4
GLA_DK = 128
GLA_DV = 256
GLA_RANK = 16
GLA_TAU = 16.0
GLA_CHUNK = 64
GLA_QK = GLA_HEADS * GLA_DK
GLA_V = GLA_HEADS * GLA_DV
GLA_COLS = 2 * GLA_QK + 2 * GLA_V + GLA_RANK

ATT_HEADS = 8
ATT_KV_HEADS = 2
ATT_HD = 128
IDX_HEADS = 8
IDX_HD = 64
TOPK_MAX = 256
Q_BLOCK = 128
ROPE_THETA = 500000.0
ROPE_DIV = 4
ATT_Q = ATT_HEADS * ATT_HD
ATT_KV = ATT_KV_HEADS * ATT_HD
ATT_COLS = ATT_Q + 2 * ATT_KV + IDX_HEADS * IDX_HD + IDX_HD + IDX_HEADS

RWKV_HEADS = 16
RWKV_HD = 64
RWKV_W_RANK = 64
RWKV_A_RANK = 64
RWKV_G_RANK = 128
RWKV_GN_EPS = 64e-5
RWKV_W = RWKV_HEADS * RWKV_HD
RWKV_COLS = 3 * RWKV_W + RWKV_W_RANK + RWKV_A_RANK + RWKV_G_RANK

GATE_COLS = N_BRANCH * D_MODEL
IN_SPLITS = (GDN_COLS, GLA_COLS, ATT_COLS, RWKV_COLS, GATE_COLS)
IN_COLS = sum(IN_SPLITS)
BRANCH_SPLITS = (GDN_V, GLA_V, ATT_Q, RWKV_W)
MIX_WIDTH = sum(BRANCH_SPLITS)

PEER_HEADS = 8
PEER_NKEYS = 128
PEER_DKEY = 256
PEER_TOPK = 16
PEER_N = PEER_NKEYS * PEER_NKEYS
PEER_BLOCK = 64

VMEM_LIMIT_BYTES = 56 * 1024 * 1024
MM_TILE_M = 1024
MM_TILE_N = 512
BF16_SUBLANES = 16
LANES = 128
PEER_SEL_TILE = 256
PEER_TOKEN_TILE = 512
PEER_EXPERT_TILE = 512
PEER_OUT_TILE_N = 2048
PEER_OUT_TILE_K = 2048
DSA_Q_TILE = 128
RWKV_CHUNK = 64
RWKV_HEAD_GROUP = 8
GDN_HEAD_GROUP = 4

HIGHEST = lax.Precision.HIGHEST
NEG_INF = float("-inf")
INT_MIN = -2 ** 31


def _mm_kernel(x_ref, w_ref, o_ref):
    o_ref[...] = jnp.dot(x_ref[...], w_ref[...], preferred_element_type=jnp.float32)


def _round_up(n, m):
    return (n + m - 1) // m * m


def _mm(x, w):
    M, K = x.shape
    N = w.shape[1]
    tm = min(MM_TILE_M, _round_up(M, BF16_SUBLANES))
    Mp = _round_up(M, tm)
    tn = MM_TILE_N
    Np = _round_up(N, tn)
    xb = x.astype(jnp.bfloat16)
    wb = w.astype(jnp.bfloat16)
    if Mp != M:
        xb = jnp.pad(xb, ((0, Mp - M), (0, 0)))
    if Np != N:
        wb = jnp.pad(wb, ((0, 0), (0, Np - N)))
    out = pl.pallas_call(
        _mm_kernel,
        grid=(Np // tn, Mp // tm),
        in_specs=[pl.BlockSpec((tm, K), lambda j, i: (i, 0)),
                  pl.BlockSpec((K, tn), lambda j, i: (0, j))],
        out_specs=pl.BlockSpec((tm, tn), lambda j, i: (i, j)),
        out_shape=jax.ShapeDtypeStruct((Mp, Np), jnp.float32),
        compiler_params=pltpu.CompilerParams(
            dimension_semantics=("parallel", "parallel"),
            vmem_limit_bytes=VMEM_LIMIT_BYTES),
        name="mm",
    )(xb, wb)
    return out[:M, :N]


def _mm3(x, w):
    B, T, K = x.shape
    return _mm(x.reshape(B * T, K), w).reshape(B, T, w.shape[1])


def _extract_top(x, order, n_out):
    outs, picks = [], []
    rank = jnp.full(x.shape, float(n_out), jnp.float32)
    for r in range(n_out):
        m = jnp.max(x, axis=0, keepdims=True)
        c = jnp.min(jnp.where(x == m, order, jnp.int32(2 ** 30)), axis=0, keepdims=True)
        hit = order == c
        x = jnp.where(hit, NEG_INF, x)
        rank = jnp.where(hit, float(r), rank)
        outs.append(m)
        picks.append(c)
    return outs, picks, rank


def _peer_select_kernel(q_ref, sk_ref, ea_ref, rk_ref, sv_ref):
    Tt = q_ref.shape[0]
    NK, DK2 = sk_ref.shape[2], sk_ref.shape[3]
    key_iota = lax.broadcasted_iota(jnp.int32, (NK, Tt), 0)
    scores, tops, ranks = [], [], []
    for p in range(2):
        s_t = lax.dot_general(sk_ref[0, p], q_ref[:, p * DK2:(p + 1) * DK2], (((1,), (1,)), ((), ())),
                              precision=HIGHEST, preferred_element_type=jnp.float32)
        sv, _, rank = _extract_top(s_t, key_iota, PEER_TOPK)
        for r in range(PEER_TOPK):
            sv_ref[p, r:r + 1, :] = sv[r]
        scores.append(s_t)
        tops.append(sv)
        ranks.append(rank)
    row8 = lax.broadcasted_iota(jnp.int32, (8, Tt), 0)
    row16 = lax.broadcasted_iota(jnp.int32, (PEER_TOPK, Tt), 0)
    strips = [sv_ref[0] + tops[1][0]]
    order = [row16 * PEER_TOPK]
    for b in range(1, 8):
        n_a = PEER_TOPK // (b + 1)
        strips.append(jnp.where(row8 < n_a, sv_ref[0, 0:8, :] + tops[1][b], NEG_INF))
        order.append(row8 * PEER_TOPK + b)
    strips.append(tops[0][0] + sv_ref[1, 8:16, :])
    order.append(row8 + 8)
    cv, picks, _ = _extract_top(jnp.concatenate(strips, axis=0), jnp.concatenate(order, axis=0), PEER_TOPK)
    z = jnp.ones_like(cv[0])
    for r in range(1, PEER_TOPK):
        z = z + jnp.exp(cv[r] - cv[0])
    n_b = jnp.zeros((NK, Tt), jnp.float32)
    for c in picks:
        n_b = n_b + jnp.where(ranks[0] == jnp.right_shift(c, 4).astype(jnp.float32), 1.0, 0.0)
    rk_ref[0] = n_b
    rk_ref[1] = ranks[1]
    ea_ref[0] = jnp.exp(scores[0] - tops[0][0])
    ea_ref[1] = jnp.exp(scores[1] - tops[1][0]) / z


def _peer_select(q, subkeys, tile):
    n = q.shape[0]
    H, _, NK, DK2 = subkeys.shape
    assert PEER_TOPK == 16 and NK % 8 == 0 and n % tile == 0
    tab = jax.ShapeDtypeStruct((2 * H, NK, n), jnp.float32)
    return pl.pallas_call(
        _peer_select_kernel,
        grid=(n // tile, H),
        in_specs=[pl.BlockSpec((tile, 2 * DK2), lambda i, h: (i, h)),
                  pl.BlockSpec((1, 2, NK, DK2), lambda i, h: (h, 0, 0, 0))],
        out_specs=[pl.BlockSpec((2, NK, tile), lambda i, h: (h, 0, i)),
                   pl.BlockSpec((2, NK, tile), lambda i, h: (h, 0, i))],
        out_shape=[tab, tab],
        scratch_shapes=[pltpu.VMEM((2, PEER_TOPK, tile), jnp.float32)],
        compiler_params=pltpu.CompilerParams(dimension_semantics=("parallel", "parallel"),
                                             vmem_limit_bytes=VMEM_LIMIT_BYTES),
        name="peer_select",
    )(q, subkeys)


def _gelu_exact(x):
    return 0.5 * x * (1.0 + lax.erf(x * (2.0 ** -0.5)))


def _peer_gate_kernel(h_ref, u_ref, ea_ref, rk_ref, g_ref):
    j = pl.program_id(1)
    Tt = h_ref.shape[0]
    Et = u_ref.shape[0]
    H2, NK, _ = ea_ref.shape
    n_i0 = Et // NK
    act_t = lax.dot_general(u_ref[...], h_ref[...], (((1,), (1,)), ((), ())),
                            preferred_element_type=jnp.float32)
    for r in range(n_i0):
        i0 = j * n_i0 + r
        rows = slice(r * NK, (r + 1) * NK)
        a0 = [ea_ref[2 * h, pl.ds(i0, 1), :] for h in range(H2 // 2)]
        n_b = [rk_ref[2 * h, pl.ds(i0, 1), :] for h in range(H2 // 2)]
        for c in range(Tt // LANES):
            cols = slice(c * LANES, (c + 1) * LANES)
            acc = None
            for h in range(H2 // 2):
                keep = rk_ref[2 * h + 1, :, cols] < n_b[h][:, cols]
                term = jnp.where(keep, ea_ref[2 * h + 1, :, cols], 0.0) * a0[h][:, cols]
                acc = term if acc is None else acc + term
            g_ref[rows, cols] = (acc * _gelu_exact(act_t[rows, cols])).astype(jnp.bfloat16)


def _peer_gate(h, u, ea, rk, token_tile, expert_tile):
    n, D = h.shape
    E = u.shape[0]
    H2, NK, _ = ea.shape
    assert E == NK * NK and expert_tile % NK == 0 and E % expert_tile == 0
    assert n % token_tile == 0 and token_tile % LANES == 0
    const = pl.Buffered(1)
    return pl.pallas_call(
        _peer_gate_kernel,
        grid=(n // token_tile, E // expert_tile),
        in_specs=[pl.BlockSpec((token_tile, D), lambda i, j: (i, 0), pipeline_mode=const),
                  pl.BlockSpec((expert_tile, D), lambda i, j: (j, 0)),
                  pl.BlockSpec((H2, NK, token_tile), lambda i, j: (0, 0, i), pipeline_mode=const),
                  pl.BlockSpec((H2, NK, token_tile), lambda i, j: (0, 0, i), pipeline_mode=const)],
        out_specs=pl.BlockSpec((expert_tile, token_tile), lambda i, j: (j, i)),
        out_shape=jax.ShapeDtypeStruct((E, n), jnp.bfloat16),
        compiler_params=pltpu.CompilerParams(dimension_semantics=("parallel", "parallel"),
                                             vmem_limit_bytes=VMEM_LIMIT_BYTES),
        name="peer_gate",
    )(h, u, ea, rk)


def _mm_tn_kernel(a_ref, b_ref, o_ref, acc_ref):
    k = pl.program_id(2)
    part = lax.dot_general(a_ref[...], b_ref[...], (((0,), (0,)), ((), ())), preferred_element_type=jnp.float32)

    @pl.when(k == 0)
    def _():
        acc_ref[...] = part

    @pl.when(k > 0)
    def _():
        acc_ref[...] += part

    @pl.when(k == pl.num_programs(2) - 1)
    def _():
        o_ref[...] = acc_ref[...]


def _mm_tn(a_t, b, tm, tn, tk):
    K, M = a_t.shape
    N = b.shape[1]
    assert M % tm == 0 and N % tn == 0 and K % tk == 0
    return pl.pallas_call(
        _mm_tn_kernel,
        grid=(M // tm, N // tn, K // tk),
        in_specs=[pl.BlockSpec((tk, tm), lambda i, j, k: (k, i)),
                  pl.BlockSpec((tk, tn), lambda i, j, k: (k, j))],
        out_specs=pl.BlockSpec((tm, tn), lambda i, j, k: (i, j)),
        out_shape=jax.ShapeDtypeStruct((M, N), jnp.float32),
        scratch_shapes=[pltpu.VMEM((tm, tn), jnp.float32)],
        compiler_params=pltpu.CompilerParams(dimension_semantics=("parallel", "parallel", "arbitrary"),
                                             vmem_limit_bytes=VMEM_LIMIT_BYTES),
        name="mm_tn",
    )(a_t, b)


def peer_ffn(h, wq, subkeys, u_bf16, v_bf16):
    B, T, D = h.shape
    n = B * T
    sel_tile = min(PEER_SEL_TILE, _round_up(n, 128))
    tok_tile = min(PEER_TOKEN_TILE, _round_up(n, 128))
    n_pad = _round_up(n, max(sel_tile, tok_tile))
    xt = h.reshape(n, D).astype(jnp.bfloat16)
    if n_pad != n:
        xt = jnp.pad(xt, ((0, n_pad - n), (0, 0)))
    q = _mm(xt, wq)
    ea, rk = _peer_select(q, subkeys, sel_tile)
    g_t = _peer_gate(xt, u_bf16, ea, rk, tok_tile, PEER_EXPERT_TILE)
    y = _mm_tn(g_t, v_bf16, tok_tile, min(D, PEER_OUT_TILE_N), PEER_OUT_TILE_K)
    return y[:n].reshape(B, T, D)


def _count_rows(mask):
    return jnp.sum(jnp.where(mask, 1.0, 0.0), axis=1, keepdims=True)


def _dsa_prompt_kernel(q_ref, k_ref, v_ref, qi_ref, ki_ref, wi_ref, o_ref, *, ktop):
    Tq = q_ref.shape[1]
    T = k_ref.shape[1]
    n_bits = int(T - 1).bit_length()
    ki = ki_ref[0]
    wi = wi_ref[0]
    score = jnp.zeros((Tq, T), jnp.float32)
    for h in range(IDX_HEADS):
        d = lax.dot_general(qi_ref[0, :, h * IDX_HD:(h + 1) * IDX_HD], ki, (((1,), (1,)), ((), ())),
                            precision=HIGHEST, preferred_element_type=jnp.float32)
        score = score + wi[:, h:h + 1] * jnp.maximum(d, 0.0)
    bits = lax.bitcast_convert_type(score, jnp.int32)
    key = jnp.where(bits < 0, bits ^ jnp.int32(0x7FFFFFFF), bits)
    key = jnp.where(bits == jnp.int32(INT_MIN), 0, key)
    qpos = pl.program_id(1) * Tq + lax.broadcasted_iota(jnp.int32, (Tq, T), 0)
    col = lax.broadcasted_iota(jnp.int32, (Tq, T), 1)
    allowed = col <= qpos
    key = jnp.where(allowed, key, jnp.int32(INT_MIN))
    kf = jnp.float32(ktop)

    tau = jnp.where(_count_rows(key >= 0) >= kf, jnp.int32(0), jnp.int32(INT_MIN))

    def tau_step(it, tau):
        cand = tau | jnp.left_shift(jnp.int32(1), 30 - it)
        return jnp.where(_count_rows(key >= cand) >= kf, cand, tau)

    tau = lax.fori_loop(0, 31, tau_step, tau)
    need = kf - _count_rows(key > tau)
    tie = (key == tau) & allowed

    def tie_step(it, p):
        cand = p | jnp.left_shift(jnp.int32(1), n_bits - 1 - it)
        return jnp.where(_count_rows(tie & (col < cand)) < need, cand, p)

    p = lax.fori_loop(0, n_bits, tie_step, jnp.zeros((Tq, 1), jnp.int32))
    sel = allowed & ((key > tau) | (tie & (col <= p)))

    group = ATT_HEADS // ATT_KV_HEADS
    scale = ATT_HD ** -0.5
    for g in range(ATT_KV_HEADS):
        kg = k_ref[0, :, g * ATT_HD:(g + 1) * ATT_HD].astype(jnp.bfloat16)
        vg = v_ref[0, :, g * ATT_HD:(g + 1) * ATT_HD].astype(jnp.bfloat16)
        for hh in range(group):
            lo = (g * group + hh) * ATT_HD
            qh = q_ref[0, :, lo:lo + ATT_HD].astype(jnp.bfloat16)
            logits = lax.dot_general(qh, kg, (((1,), (1,)), ((), ())), preferred_element_type=jnp.float32) * scale
            logits = jnp.where(sel, logits, NEG_INF)
            e = jnp.exp(logits - jnp.max(logits, axis=1, keepdims=True))
            out = jnp.dot(e.astype(jnp.bfloat16), vg, preferred_element_type=jnp.float32)
            o_ref[0, :, lo:lo + ATT_HD] = out / jnp.sum(e, axis=1, keepdims=True)


def dsa_prompt(q, k, v, qi, ki, wi):
    B, T = q.shape[:2]
    ktop = min(TOPK_MAX, T // 4)
    tq = min(DSA_Q_TILE, T)
    flat = lambda a: a.reshape(B, T, -1)
    o = pl.pallas_call(
        functools.partial(_dsa_prompt_kernel, ktop=ktop),
        grid=(B, T // tq),
        in_specs=[pl.BlockSpec((1, tq, ATT_Q), lambda b, i: (b, i, 0)),
                  pl.BlockSpec((1, T, ATT_KV), lambda b, i: (b, 0, 0)),
                  pl.BlockSpec((1, T, ATT_KV), lambda b, i: (b, 0, 0)),
                  pl.BlockSpec((1, tq, IDX_HEADS * IDX_HD), lambda b, i: (b, i, 0)),
                  pl.BlockSpec((1, T, IDX_HD), lambda b, i: (b, 0, 0)),
                  pl.BlockSpec((1, tq, IDX_HEADS), lambda b, i: (b, i, 0))],
        out_specs=pl.BlockSpec((1, tq, ATT_Q), lambda b, i: (b, i, 0)),
        out_shape=jax.ShapeDtypeStruct((B, T, ATT_Q), jnp.float32),
        compiler_params=pltpu.CompilerParams(dimension_semantics=("parallel", "parallel"),
                                             vmem_limit_bytes=VMEM_LIMIT_BYTES),
        name="dsa_prompt",
    )(flat(q), flat(k), flat(v), flat(qi), ki, wi)
    return o.reshape(q.shape)


_NN = (((1,), (0,)), ((), ()))
_NT = (((1,), (1,)), ((), ()))
_TN = (((0,), (0,)), ((), ()))


def _dot_f32(x, y, dims=_NN):
    return lax.dot_general(x, y, dims, precision=HIGHEST, preferred_element_type=jnp.float32)


def _split_bf16(x):
    hi = x.astype(jnp.bfloat16)
    return hi, (x - hi.astype(jnp.float32)).astype(jnp.bfloat16)


def _dot3(x, y, dims=_NN):
    xh, xl = x if isinstance(x, tuple) else _split_bf16(x)
    yh, yl = y if isinstance(y, tuple) else _split_bf16(y)
    d = lambda p, q: lax.dot_general(p, q, dims, preferred_element_type=jnp.float32)
    return d(xh, yh) + (d(xh, yl) + d(xl, yh))


def _dot1(x, y, dims=_NN):
    return lax.dot_general(x.astype(jnp.bfloat16), y.astype(jnp.bfloat16), dims, preferred_element_type=jnp.float32)


def _unit_lower_solve(l_mats, rhs, size):
    xs = list(rhs)
    for step in range(max(1, int(size - 1).bit_length())):
        if step:
            l_mats = [_dot3(sp, sp) for sp in splits]
        splits = [_split_bf16(l) for l in l_mats]
        xs = [x + _dot3(sp, x) for sp, x in zip(splits, xs)]
    return xs


def _rwkv_kernel(r_ref, lw_ref, k_ref, v_ref, a_ref, b_ref, s0_ref, y_ref, s_ref):
    C = r_ref.shape[2]
    row = lax.broadcasted_iota(jnp.int32, (C, C), 0)
    col = lax.broadcasted_iota(jnp.int32, (C, C), 1)
    incl = row >= col
    strict = row > col
    tri = jnp.where(incl, 1.0, 0.0)
    incl2 = (lax.broadcasted_iota(jnp.int32, (C, 2 * C), 0)
             >= (lax.broadcasted_iota(jnp.int32, (C, 2 * C), 1) & (C - 1)))

    @pl.when(pl.program_id(2) == 0)
    def _():
        s_ref[...] = s0_ref[...]

    heads = range(r_ref.shape[1])
    lw = [lw_ref[0, h] for h in heads]
    v = [v_ref[0, h] for h in heads]
    s0 = [s_ref[0, h] for h in heads]
    log_p = [_dot_f32(tri, x) for x in lw]
    ar = [_split_bf16(jnp.concatenate([a_ref[0, h] * jnp.exp(log_p[h] - lw[h]), r_ref[0, h] * jnp.exp(log_p[h])],
                                      axis=0)) for h in heads]
    bk = [jnp.concatenate([b_ref[0, h], k_ref[0, h]], axis=0) for h in heads]
    g = [_dot3(ar[h], bk[h] * jnp.exp(-jnp.concatenate([log_p[h], log_p[h]], axis=0)), _NT) for h in heads]
    ars = [_dot3(ar[h], s0[h], _NT) for h in heads]
    rhs = [ars[h][:C] + _dot3(jnp.where(strict, g[h][:C, C:], 0.0), v[h]) for h in heads]
    u = _unit_lower_solve([jnp.where(strict, g[h][:C, :C], 0.0) for h in heads], rhs, C)
    uv = [jnp.concatenate([u[h], v[h]], axis=0) for h in heads]
    for h in heads:
        y_ref[0, h] = ars[h][C:] + _dot1(jnp.where(incl2, g[h][C:], 0.0), uv[h])
    for h in heads:
        to_end = jnp.exp(log_p[h][C - 1:C, :] - jnp.concatenate([log_p[h], log_p[h]], axis=0))
        s_ref[0, h] = s0[h] * jnp.exp(log_p[h][C - 1:C, :]) + _dot3(uv[h], bk[h] * to_end, _TN)


def rwkv7_scan(r, w_log, k, v, a, b, s0):
    B, T, H, N = r.shape
    C = min(RWKV_CHUNK, _round_up(T, 8))
    Tp = _round_up(T, C)

    def prep(x):
        x = jnp.moveaxis(x.astype(jnp.float32), 2, 1)
        return jnp.pad(x, ((0, 0), (0, 0), (0, Tp - T), (0, 0))) if Tp != T else x

    hg = RWKV_HEAD_GROUP
    seq = pl.BlockSpec((1, hg, C, N), lambda bi, gi, ci: (bi, gi, ci, 0))
    st = pl.BlockSpec((1, hg, N, N), lambda bi, gi, ci: (bi, gi, 0, 0))
    y, s_new = pl.pallas_call(
        _rwkv_kernel,
        grid=(B, H // hg, Tp // C),
        in_specs=[seq] * 6 + [st],
        out_specs=[seq, st],
        out_shape=[jax.ShapeDtypeStruct((B, H, Tp, N), jnp.float32),
                   jax.ShapeDtypeStruct((B, H, N, N), jnp.float32)],
        compiler_params=pltpu.CompilerParams(dimension_semantics=("parallel", "parallel", "arbitrary"),
                                             vmem_limit_bytes=VMEM_LIMIT_BYTES),
        name="rwkv7_scan",
    )(prep(r), prep(w_log), prep(k), prep(v), prep(a), prep(b), s0.astype(jnp.float32))
    return jnp.moveaxis(y[:, :, :T], 1, 2), s_new


def _gdn_kernel(q_ref, k_ref, v_ref, bl_ref, gl_ref, gr_ref, s0_ref, o_ref, s_ref):
    C = q_ref.shape[2]
    DV = v_ref.shape[3]
    row = lax.broadcasted_iota(jnp.int32, (C, C), 0)
    col = lax.broadcasted_iota(jnp.int32, (C, C), 1)
    incl = row >= col
    strict = row > col
    tri = jnp.where(incl, 1.0, 0.0)
    tri_t = jnp.where(row <= col, 1.0, 0.0)

    @pl.when(pl.program_id(2) == 0)
    def _():
        s_ref[...] = s0_ref[...]

    heads = range(q_ref.shape[1])
    k = [k_ref[0, h] for h in heads]
    s0 = [s_ref[0, h] for h in heads]
    gc = [_dot_f32(tri, gl_ref[0, h]) for h in heads]
    gc_row = [_dot_f32(gr_ref[0, h, 0], tri_t)[0:1, :] for h in heads]
    decay = [jnp.exp(jnp.where(incl, gc[h][:, :C] - gc_row[h], NEG_INF)) for h in heads]
    kb = [k[h] * bl_ref[0, h] for h in heads]
    a_low = [jnp.where(strict, _dot3(kb[h], k[h], _NT) * decay[h], 0.0) for h in heads]
    rhs = [jnp.concatenate([v_ref[0, h] * bl_ref[0, h], kb[h] * jnp.exp(gc[h])], axis=1) for h in heads]
    sol = _unit_lower_solve([-a for a in a_low], rhs, C)
    v_new = [sol[h][:, :DV] - _dot3(sol[h][:, DV:], s0[h]) for h in heads]
    for h in heads:
        qk = _dot1(q_ref[0, h], k[h], _NT) * decay[h]
        o_ref[0, h] = _dot1(q_ref[0, h] * jnp.exp(gc[h]), s0[h]) + _dot1(qk, v_new[h])
    for h in heads:
        g_last = gc[h][C - 1:C, :]
        s_ref[0, h] = s0[h] * jnp.exp(g_last) + _dot3(k[h] * jnp.exp(g_last - gc[h]), v_new[h], _TN)


def gated_delta_rule(q, k, v, beta, g, s0):
    B, T, H, DK = q.shape
    DV = v.shape[-1]
    assert DK == LANES and DV == LANES
    C = min(GDN_CHUNK, _round_up(T, 8))
    Tp = _round_up(T, C)
    nc = Tp // C

    def heads_first(x):
        x = jnp.moveaxis(x.astype(jnp.float32), 2, 1)
        return jnp.pad(x, ((0, 0), (0, 0), (0, Tp - T)) + ((0, 0),) * (x.ndim - 3)) if Tp != T else x

    lanes = lambda x: jnp.broadcast_to(heads_first(x)[..., None], (B, H, Tp, LANES))
    g_rows = jnp.broadcast_to(heads_first(g).reshape(B, H, nc, 1, C), (B, H, nc, 8, C))
    hg = GDN_HEAD_GROUP
    seq = lambda d: pl.BlockSpec((1, hg, C, d), lambda bi, gi, ci: (bi, gi, ci, 0))
    st = pl.BlockSpec((1, hg, DK, DV), lambda bi, gi, ci: (bi, gi, 0, 0))
    o, s_new = pl.pallas_call(
        _gdn_kernel,
        grid=(B, H // hg, nc),
        in_specs=[seq(DK), seq(DK), seq(DV), seq(LANES), seq(LANES),
                  pl.BlockSpec((1, hg, 1, 8, C), lambda bi, gi, ci: (bi, gi, ci, 0, 0)), st],
        out_specs=[seq(DV), st],
        out_shape=[jax.ShapeDtypeStruct((B, H, Tp, DV), jnp.float32),
                   jax.ShapeDtypeStruct((B, H, DK, DV), jnp.float32)],
        compiler_params=pltpu.CompilerParams(dimension_semantics=("parallel", "parallel", "arbitrary"),
                                             vmem_limit_bytes=VMEM_LIMIT_BYTES),
        name="gated_delta_rule",
    )(heads_first(q), heads_first(k), heads_first(v), lanes(beta), lanes(g), g_rows, s0.astype(jnp.float32))
    return jnp.moveaxis(o[:, :, :T], 1, 2), s_new


def _split(a, sizes, axis=-1):
    return jnp.split(a, [int(s) for s in np.cumsum(sizes)[:-1]], axis=axis)


def rmsnorm(x, gain):
    xf = x.astype(jnp.float32)
    y = xf * lax.rsqrt(jnp.mean(xf * xf, axis=-1, keepdims=True) + EPS)
    return (y * gain.astype(jnp.float32)).astype(x.dtype)


def l2norm(x):
    xf = x.astype(jnp.float32)
    return xf * lax.rsqrt(jnp.sum(xf * xf, axis=-1, keepdims=True) + EPS)


def partial_rope(x, pos):
    rd = x.shape[-1] // ROPE_DIV
    half = rd // 2
    inv_freq = ROPE_THETA ** (-jnp.arange(half, dtype=jnp.float32) * 2.0 / rd)
    ang = pos.astype(jnp.float32)[:, None] * inv_freq[None, :]
    cos = jnp.cos(ang)[:, None, :]
    sin = jnp.sin(ang)[:, None, :]
    xf = x[..., :rd].astype(jnp.float32)
    x1, x2 = xf[..., :half], xf[..., half:]
    rot = jnp.concatenate([x1 * cos - x2 * sin, x2 * cos + x1 * sin], axis=-1)
    return jnp.concatenate([rot.astype(x.dtype), x[..., rd:]], axis=-1)


def causal_conv(x, buf, w):
    T = x.shape[1]
    xp = jnp.concatenate([buf.astype(x.dtype), x], axis=1)
    y = xp[:, 0:T] * w[0]
    for i in range(1, CONV_W):
        y = y + xp[:, i:i + T] * w[i]
    return jax.nn.silu(y), xp[:, T:]


def _to_chunks(a, C):
    B, T = a.shape[:2]
    pad = (-T) % C
    a = jnp.pad(a.astype(jnp.float32), [(0, 0), (0, pad)] + [(0, 0)] * (a.ndim - 2))
    n = (T + pad) // C
    a = a.reshape((B, n, C) + a.shape[2:])
    return jnp.transpose(a, (1, 0, 3, 2) + tuple(range(4, a.ndim)))


def _from_chunks(o, T):
    n, B, H, C, X = o.shape
    return jnp.transpose(o, (1, 0, 3, 2, 4)).reshape(B, n * C, H, X)[:, :T]


def gla_chunked(q, k, v, log_a, s0):
    T = q.shape[1]
    C = min(GLA_CHUNK, T)
    qc, kc, vc = _to_chunks(q, C), _to_chunks(k, C), _to_chunks(v, C)
    bc = jnp.cumsum(_to_chunks(log_a, C), axis=-2)
    tri = jnp.tril(jnp.ones((C, C), bool))[..., None]

    def step(S, xs):
        q_n, k_n, v_n, b_n = xs
        rel = jnp.exp(jnp.where(tri, b_n[..., :, None, :] - b_n[..., None, :, :], -jnp.inf))
        att = jnp.einsum('bhik,bhijk,bhjk->bhij', q_n, rel, k_n)
        o = jnp.einsum('bhik,bhkv->bhiv', q_n * jnp.exp(b_n), S) + jnp.einsum('bhij,bhjv->bhiv', att, v_n)
        b_last = b_n[..., -1:, :]
        S = S * jnp.exp(b_last)[..., 0, :, None] + jnp.einsum('bhck,bhcv->bhkv', k_n * jnp.exp(b_last - b_n), v_n)
        return S, o

    S, o = lax.scan(step, s0.astype(jnp.float32), (qc, kc, vc, bc))
    return _from_chunks(o, T), S


def head_group_norm(y, w, b):
    B, T, H, N = y.shape
    mu = jnp.mean(y, axis=-1, keepdims=True)
    var = jnp.mean(jnp.square(y - mu), axis=-1, keepdims=True)
    yn = ((y - mu) * lax.rsqrt(var + RWKV_GN_EPS)).reshape(B, T, H * N)
    return yn * w.astype(jnp.float32) + b.astype(jnp.float32)


def indexer_scores(qi, wi, ki):
    dots = jnp.einsum('bqhd,bsd->bqhs', qi, ki).astype(jnp.float32)
    return jnp.einsum('bqh,bqhs->bqs', wi.astype(jnp.float32), jax.nn.relu(dots))


def select_keys(scores, qpos, ktop):
    S = scores.shape[-1]
    allowed = jnp.arange(S)[None, None, :] <= qpos[None, :, None]
    _, idx = lax.top_k(jnp.where(allowed, scores, -jnp.inf), ktop)
    return idx, idx <= qpos[None, :, None]


def sparse_attend(q, k_sel, v_sel, valid):
    B, Q, HQ, HD = q.shape
    qg = q.reshape(B, Q, ATT_KV_HEADS, HQ // ATT_KV_HEADS, HD)
    logits = jnp.einsum('bqhgd,bqkhd->bqhgk', qg, k_sel).astype(jnp.float32) * (HD ** -0.5)
    logits = jnp.where(valid[:, :, None, None, :], logits, -jnp.inf)
    p = jax.nn.softmax(logits, axis=-1).astype(v_sel.dtype)
    return jnp.einsum('bqhgk,bqkhd->bqhgd', p, v_sel).reshape(B, Q, HQ, HD)


def dsa_sample(q, k, v, qi, ki, wi, cache_k, cache_v, cache_ki, page_table):
    DB, DS = q.shape[:2]
    past = page_table.shape[1] * PAGE_SIZE
    ktop = min(TOPK_MAX, (past + DS) // 4)
    ki_past = cache_ki[page_table].reshape(DB, past, IDX_HD).astype(ki.dtype)
    ki_all = jnp.concatenate([ki_past, ki], axis=1)
    qpos = past + jnp.arange(DS)
    idx, valid = select_keys(indexer_scores(qi, wi, ki_all), qpos, ktop)
    is_new = idx >= past
    pidx = jnp.minimum(idx, past - 1)
    phys = jax.vmap(lambda pt, i: pt[i])(page_table, pidx // PAGE_SIZE)
    off = pidx % PAGE_SIZE
    nidx = jnp.clip(idx - past, 0, DS - 1)
    gather = jax.vmap(lambda rows, i: rows[i])
    sel = lambda cache, new: jnp.where(is_new[..., None, None], gather(new, nidx), cache[phys, off].astype(new.dtype))
    return sparse_attend(q, sel(cache_k, k), sel(cache_v, v), valid)


def gdn_branch(p, conv_buf, s0, lp):
    B, T, _ = p.shape
    conv_in, z, b_raw, a_raw = _split(p, (GDN_CONV_COLS, GDN_V, GDN_HEADS, GDN_HEADS))
    conv_out, new_buf = causal_conv(conv_in, conv_buf, lp['gdn_conv_w'])
    q, k, v = _split(conv_out, (GDN_QK, GDN_QK, GDN_V))
    q = l2norm(q.reshape(B, T, GDN_HEADS, GDN_DK)) * (GDN_DK ** -0.5)
    k = l2norm(k.reshape(B, T, GDN_HEADS, GDN_DK))
    v = v.reshape(B, T, GDN_HEADS, GDN_DV)
    beta = jax.nn.sigmoid(b_raw.astype(jnp.float32))
    g = -jnp.exp(lp['gdn_A_log'].astype(jnp.float32)) * jax.nn.softplus(a_raw.astype(jnp.float32) + lp['gdn_dt_bias'].astype(jnp.float32))
    o, s_new = gated_delta_rule(q, k, v, beta, g, s0)
    o = rmsnorm(o, lp['gdn_norm']) * jax.nn.silu(z.reshape(B, T, GDN_HEADS, GDN_DV).astype(jnp.float32))
    return o.reshape(B, T, GDN_V), new_buf, s_new


def gla_branch(p, s0, lp):
    B, T, _ = p.shape
    q, k, v, r, gd = _split(p, (GLA_QK, GLA_QK, GLA_V, GLA_V, GLA_RANK))
    hs = (B, T, GLA_HEADS)
    log_a = jax.nn.log_sigmoid((gd @ lp['gla_gate_up'] + lp['gla_gate_bias']).astype(jnp.float32)) / GLA_TAU
    o, s_new = gla_chunked(q.reshape(hs + (GLA_DK,)) * (GLA_DK ** -0.5), k.reshape(hs + (GLA_DK,)),
                           v.reshape(hs + (GLA_DV,)), log_a.reshape(hs + (GLA_DK,)), s0)
    o = rmsnorm(o, lp['gla_norm']) * jax.nn.silu(r.reshape(hs + (GLA_DV,)).astype(jnp.float32))
    return o.reshape(B, T, GLA_V), s_new


def dsa_branch(p, pos, attend):
    B, T, _ = p.shape
    q, k, v, qi, ki, wi = _split(p, (ATT_Q, ATT_KV, ATT_KV, IDX_HEADS * IDX_HD, IDX_HD, IDX_HEADS))
    q = partial_rope(q.reshape(B, T, ATT_HEADS, ATT_HD), pos)
    k = partial_rope(k.reshape(B, T, ATT_KV_HEADS, ATT_HD), pos)
    v = v.reshape(B, T, ATT_KV_HEADS, ATT_HD)
    qi = partial_rope(qi.reshape(B, T, IDX_HEADS, IDX_HD), pos)
    ki = partial_rope(ki.reshape(B, T, 1, IDX_HD), pos)[:, :, 0]
    o = attend(q, k, v, qi, ki, wi)
    return o.reshape(B, T, ATT_Q), (k, v, ki)


def rwkv_branch(p, shift_buf, s0, lp):
    B, T, _ = p.shape
    prev = jnp.concatenate([shift_buf.astype(p.dtype), p[:, :-1]], axis=1)
    pm = p + (prev - p) * lp['rwkv_mu']
    r, wd, k, v, ad, gd = _split(pm, (RWKV_W, RWKV_W_RANK, RWKV_W, RWKV_W, RWKV_A_RANK, RWKV_G_RANK))
    hs = (B, T, RWKV_HEADS, RWKV_HD)
    w_log = -jax.nn.softplus(-(lp['rwkv_w0'] + jnp.tanh(wd) @ lp['rwkv_w2']).astype(jnp.float32)) - 0.5
    log_decay = -jnp.exp(w_log)
    a = jax.nn.sigmoid((lp['rwkv_a0'] + ad @ lp['rwkv_a2']).astype(jnp.float32))
    g = (jax.nn.sigmoid(gd) @ lp['rwkv_g2']).astype(jnp.float32)
    kf = k.astype(jnp.float32)
    kk = l2norm((kf * lp['rwkv_kk']).reshape(hs))
    kf = (kf * (1.0 + (a - 1.0) * lp['rwkv_ka'])).reshape(hs)
    rf = r.astype(jnp.float32).reshape(hs)
    vf = v.astype(jnp.float32).reshape(hs)
    y, s_new = rwkv7_scan(rf, log_decay.reshape(hs), kf, vf, -kk, kk * a.reshape(hs), s0)
    y = head_group_norm(y, lp['rwkv_ln_w'], lp['rwkv_ln_b'])
    bonus = (jnp.sum(rf * kf * lp['rwkv_rk'], axis=-1, keepdims=True) * vf).reshape(B, T, RWKV_W)
    return (y + bonus) * g, p[:, -1:], s_new


def token_mixers(h, pos, st, lp, attend):
    gdn_s, gdn_buf, gla_s, rwkv_s, rwkv_buf = st
    proj = _mm3(h, lp['w_in'])
    p_gdn, p_gla, p_att, p_rwkv, p_gate = _split(proj, IN_SPLITS)
    o_gdn, gdn_buf, gdn_s = gdn_branch(p_gdn, gdn_buf, gdn_s, lp)
    o_gla, gla_s = gla_branch(p_gla, gla_s, lp)
    o_att, rows = dsa_branch(p_att, pos, attend)
    o_rwkv, rwkv_buf, rwkv_s = rwkv_branch(p_rwkv, rwkv_buf, rwkv_s, lp)
    w_rows = _split(lp['w_branch'], BRANCH_SPLITS, axis=0)
    gate_cols = _split(p_gate, (D_MODEL,) * N_BRANCH)
    merged = None
    for o, w, gc in zip((o_gdn, o_gla, o_att, o_rwkv), w_rows, gate_cols):
        term = jax.nn.sigmoid(gc.astype(jnp.float32)) * _mm3(o.astype(h.dtype), w)
        merged = term if merged is None else merged + term
    out = _mm3(merged.astype(h.dtype), lp['w_out'])
    return out, rows + (gdn_s, gdn_buf, gla_s, rwkv_s, rwkv_buf)


def trunk_layer(x, c, pos, st, lp, attend):
    B = x.shape[0]
    mod = (jax.nn.silu(c) @ lp['w_ada'] + lp['b_ada']).reshape(B, N_MOD, 1, D_MODEL)
    shift1, scale1, gate1, shift2, scale2, gate2 = [mod[:, i] for i in range(N_MOD)]
    h = rmsnorm(x, lp['norm1']) * (1.0 + scale1) + shift1
    mix, new_st = token_mixers(h, pos, st, lp, attend)
    x = x + gate1 * mix
    h = rmsnorm(x, lp['norm2']) * (1.0 + scale2) + shift2
    x = x + gate2 * peer_ffn(h, lp['peer_wq'], lp['peer_subkeys'], lp['peer_u'], lp['peer_v'])
    return x, new_st


def kernel(x_prompt, x_sample, cache_k, cache_v, cache_kidx, state_gdn, state_gdn_conv, state_gla, state_rwkv, state_rwkv_shift, page_table, c_prompt, c_sample, w_ada, b_ada, norm1, norm2, w_in, gdn_conv_w, gdn_A_log, gdn_dt_bias, gdn_norm, gla_gate_up, gla_gate_bias, gla_norm, rwkv_mu, rwkv_w0, rwkv_w2, rwkv_a0, rwkv_a2, rwkv_g2, rwkv_kk, rwkv_ka, rwkv_rk, rwkv_ln_w, rwkv_ln_b, w_branch, w_out, peer_wq, peer_subkeys, peer_u, peer_v, final_norm):
    Bp, T, _ = x_prompt.shape
    DS = x_sample.shape[1]
    depth = w_in.shape[0]
    past = page_table.shape[1] * PAGE_SIZE
    pos_p = jnp.arange(T, dtype=jnp.int32)
    pos_s = past + jnp.arange(DS, dtype=jnp.int32)
    st_p0 = (jnp.zeros((Bp, GDN_HEADS, GDN_DK, GDN_DV), jnp.float32),
             jnp.zeros((Bp, CONV_W - 1, GDN_CONV_COLS), x_prompt.dtype),
             jnp.zeros((Bp, GLA_HEADS, GLA_DK, GLA_DV), jnp.float32),
             jnp.zeros((Bp, RWKV_HEADS, RWKV_HD, RWKV_HD), jnp.float32),
             jnp.zeros((Bp, 1, RWKV_COLS), x_prompt.dtype))
    weights = {'w_ada': w_ada, 'b_ada': b_ada, 'norm1': norm1, 'norm2': norm2, 'w_in': w_in,
               'gdn_conv_w': gdn_conv_w, 'gdn_A_log': gdn_A_log, 'gdn_dt_bias': gdn_dt_bias, 'gdn_norm': gdn_norm,
               'gla_gate_up': gla_gate_up, 'gla_gate_bias': gla_gate_bias, 'gla_norm': gla_norm,
               'rwkv_mu': rwkv_mu, 'rwkv_w0': rwkv_w0, 'rwkv_w2': rwkv_w2, 'rwkv_a0': rwkv_a0, 'rwkv_a2': rwkv_a2,
               'rwkv_g2': rwkv_g2, 'rwkv_kk': rwkv_kk, 'rwkv_ka': rwkv_ka, 'rwkv_rk': rwkv_rk,
               'rwkv_ln_w': rwkv_ln_w, 'rwkv_ln_b': rwkv_ln_b, 'w_branch': w_branch, 'w_out': w_out,
               'peer_wq': peer_wq, 'peer_subkeys': peer_subkeys, 'peer_u': peer_u, 'peer_v': peer_v}
    xp, xs = x_prompt, x_sample
    out_p, out_s = [], []
    for l in range(depth):
        lp = {name: arr[l] for name, arr in weights.items()}
        lp['peer_u'] = lp['peer_u'].astype(jnp.bfloat16)
        lp['peer_v'] = lp['peer_v'].astype(jnp.bfloat16)
        xp, new_p = trunk_layer(xp, c_prompt, pos_p, st_p0, lp, dsa_prompt)
        attend_s = functools.partial(dsa_sample, cache_k=cache_k[l], cache_v=cache_v[l],
                                     cache_ki=cache_kidx[l], page_table=page_table)
        st_s0 = (state_gdn[l], state_gdn_conv[l], state_gla[l], state_rwkv[l], state_rwkv_shift[l])
        xs, new_s = trunk_layer(xs, c_sample, pos_s, st_s0, lp, attend_s)
        out_p.append(new_p)
        out_s.append(new_s)
    y_prompt = rmsnorm(xp, final_norm)
    y_sample = rmsnorm(xs, final_norm)
    k_p, v_p, kidx_p, gdn_p, gdn_conv_p, gla_p, rwkv_p, rwkv_shift_p = [jnp.stack(z) for z in zip(*out_p)]
    k_s, v_s, kidx_s, gdn_s, gdn_conv_s, gla_s, rwkv_s, rwkv_shift_s = [jnp.stack(z) for z in zip(*out_s)]
    return (y_prompt, y_sample, k_p, v_p, kidx_p, gdn_p, gdn_conv_p, gla_p, rwkv_p, rwkv_shift_p,
            k_s, v_s, kidx_s, gdn_s, gdn_conv_s, gla_s, rwkv_s, rwkv_shift_s)
```

```python
import math, functools
import jax, jax.numpy as jnp
from jax import lax
import numpy as np
from jax.experimental import pallas as pl
from jax.experimental.pallas import tpu as pltpu

D_MODEL = 4096
PAGE_SIZE = 128
N_MOD = 6
EPS = 1e-6
N_BRANCH = 4

GDN_HEADS = 8
GDN_DK = 128
GDN_DV = 128
CONV_W = 4
GDN_CHUNK = 64
GDN_QK = GDN_HEADS * GDN_DK
GDN_V = GDN_HEADS * GDN_DV
GDN_CONV_COLS = 2 * GDN_QK + GDN_V
GDN_COLS = GDN_CONV_COLS + GDN_V + 2 * GDN_HEADS

GLA_HEADS = 4
GLA_DK = 128
GLA_DV = 256
GLA_RANK = 16
GLA_TAU = 16.0
GLA_CHUNK = 64
GLA_QK = GLA_HEADS * GLA_DK
GLA_V = GLA_HEADS * GLA_DV
GLA_COLS = 2 * GLA_QK + 2 * GLA_V + GLA_RANK

ATT_HEADS = 8
ATT_KV_HEADS = 2
ATT_HD = 128
IDX_HEADS = 8
IDX_HD = 64
TOPK_MAX = 256
Q_BLOCK = 128
ROPE_THETA = 500000.0
ROPE_DIV = 4
ATT_Q = ATT_HEADS * ATT_HD
ATT_KV = ATT_KV_HEADS * ATT_HD
ATT_COLS = ATT_Q + 2 * ATT_KV + IDX_HEADS * IDX_HD + IDX_HD + IDX_HEADS

RWKV_HEADS = 16
RWKV_HD = 64
RWKV_W_RANK = 64
RWKV_A_RANK = 64
RWKV_G_RANK = 128
RWKV_GN_EPS = 64e-5
RWKV_W = RWKV_HEADS * RWKV_HD
RWKV_COLS = 3 * RWKV_W + RWKV_W_RANK + RWKV_A_RANK + RWKV_G_RANK

GATE_COLS = N_BRANCH * D_MODEL
IN_SPLITS = (GDN_COLS, GLA_COLS, ATT_COLS, RWKV_COLS, GATE_COLS)
IN_COLS = sum(IN_SPLITS)
BRANCH_SPLITS = (GDN_V, GLA_V, ATT_Q, RWKV_W)
MIX_WIDTH = sum(BRANCH_SPLITS)

PEER_HEADS = 8
PEER_NKEYS = 128
PEER_DKEY = 256
PEER_TOPK = 16
PEER_N = PEER_NKEYS * PEER_NKEYS
PEER_BLOCK = 64

VMEM_LIMIT_BYTES = 56 * 1024 * 1024
MM_TILE_M = 1024
MM_TILE_N = 512
MM_TILE_N_FEW_ROWS = 2048
MM_FEW_ROWS = 64
MERGE_TILE_M = 512
MERGE_TILE_N = 512
BF16_SUBLANES = 16
LANES = 128
PEER_SEL_TILE = 256
PEER_TOKEN_TILE = 512
PEER_EXPERT_TILE = 512
PEER_OUT_TILE_N = 2048
PEER_OUT_TILE_K = 2048
DSA_Q_TILE = 128
DSA_KEY_GROUPS = 4
RWKV_CHUNK = 64
RWKV_HEAD_GROUP = 8
GDN_HEAD_GROUP = 4

HIGHEST = lax.Precision.HIGHEST
NEG_INF = float("-inf")
INT_MIN = -2 ** 31


def _mm_kernel(x_ref, w_ref, o_ref):
    o_ref[...] = jnp.dot(x_ref[...], w_ref[...], preferred_element_type=jnp.float32)


def _round_up(n, m):
    return (n + m - 1) // m * m


def _pad_cols_bf16(w, multiple=None):
    multiple = multiple or MM_TILE_N
    n_pad = _round_up(w.shape[1], multiple)
    wb = w.astype(jnp.bfloat16)
    return wb if n_pad == w.shape[1] else jnp.pad(wb, ((0, 0), (0, n_pad - w.shape[1])))


def _mm(x, w):
    M, K = x.shape
    N = w.shape[1]
    assert N % MM_TILE_N == 0
    tm = min(MM_TILE_M, _round_up(M, BF16_SUBLANES))
    Mp = _round_up(M, tm)
    tn = next(t for t in (MM_TILE_N_FEW_ROWS, MM_TILE_N) if N % t == 0) if Mp <= MM_FEW_ROWS else MM_TILE_N
    xb = x.astype(jnp.bfloat16)
    if Mp != M:
        xb = jnp.pad(xb, ((0, Mp - M), (0, 0)))
    out = pl.pallas_call(
        _mm_kernel,
        grid=(N // tn, Mp // tm),
        in_specs=[pl.BlockSpec((tm, K), lambda j, i: (i, 0)),
                  pl.BlockSpec((K, tn), lambda j, i: (0, j))],
        out_specs=pl.BlockSpec((tm, tn), lambda j, i: (i, j)),
        out_shape=jax.ShapeDtypeStruct((Mp, N), jnp.float32),
        compiler_params=pltpu.CompilerParams(
            dimension_semantics=("parallel", "parallel"),
            vmem_limit_bytes=VMEM_LIMIT_BYTES),
        name="mm",
    )(xb, w.astype(jnp.bfloat16))
    return out if Mp == M else out[:M]


def _mm3(x, w):
    B, T, K = x.shape
    return _mm(x.reshape(B * T, K), w).reshape(B, T, w.shape[1])


def _merge_kernel(*refs):
    n = (len(refs) - 1) // 3
    out_ref = refs[-1]
    acc = None
    for b in range(n):
        o_ref, w_ref, g_ref = refs[b], refs[n + b], refs[2 * n + b]
        term = jax.nn.sigmoid(g_ref[...]) * jnp.dot(o_ref[...], w_ref[...], preferred_element_type=jnp.float32)
        acc = term if acc is None else acc + term
    out_ref[...] = acc.astype(out_ref.dtype)


def _merge_branches(outs, w_branch, p_gate):
    n = len(outs)
    M = outs[0].shape[0]
    D = w_branch.shape[1]
    kb = outs[0].shape[1]
    assert all(o.shape == (M, kb) for o in outs) and w_branch.shape[0] == n * kb and p_gate.shape == (M, n * D)
    tm = min(MERGE_TILE_M, _round_up(M, BF16_SUBLANES))
    Mp = _round_up(M, tm)
    tn = MERGE_TILE_N
    obs = [o.astype(jnp.bfloat16) for o in outs]
    if Mp != M:
        obs = [jnp.pad(o, ((0, Mp - M), (0, 0))) for o in obs]
        p_gate = jnp.pad(p_gate, ((0, Mp - M), (0, 0)))
    o_specs = [pl.BlockSpec((tm, kb), lambda i, j: (i, 0)) for _ in range(n)]
    w_specs = [pl.BlockSpec((kb, tn), functools.partial(lambda i, j, b: (b, j), b=b)) for b in range(n)]
    g_specs = [pl.BlockSpec((tm, tn), functools.partial(lambda i, j, b: (i, b * (D // tn) + j), b=b))
               for b in range(n)]
    out = pl.pallas_call(
        _merge_kernel,
        grid=(Mp // tm, D // tn),
        in_specs=o_specs + w_specs + g_specs,
        out_specs=pl.BlockSpec((tm, tn), lambda i, j: (i, j)),
        out_shape=jax.ShapeDtypeStruct((Mp, D), jnp.bfloat16),
        compiler_params=pltpu.CompilerParams(dimension_semantics=("parallel", "parallel"),
                                             vmem_limit_bytes=VMEM_LIMIT_BYTES),
        name="merge_branches",
    )(*obs, *([w_branch] * n), *([p_gate] * n))
    return out if Mp == M else out[:M]


def _extract_top(x, order, n_out):
    outs, picks = [], []
    rank = jnp.full(x.shape, float(n_out), jnp.float32)
    for r in range(n_out):
        m = jnp.max(x, axis=0, keepdims=True)
        c = jnp.min(jnp.where(x == m, order, jnp.int32(2 ** 30)), axis=0, keepdims=True)
        hit = order == c
        x = jnp.where(hit, NEG_INF, x)
        rank = jnp.where(hit, float(r), rank)
        outs.append(m)
        picks.append(c)
    return outs, picks, rank


def _peer_select_kernel(q_ref, sk_ref, ea_ref, rk_ref, sv_ref):
    Tt = q_ref.shape[0]
    NK, DK2 = sk_ref.shape[2], sk_ref.shape[3]
    key_iota = lax.broadcasted_iota(jnp.int32, (NK, Tt), 0)
    scores, tops, ranks = [], [], []
    for p in range(2):
        s_t = lax.dot_general(sk_ref[0, p], q_ref[:, p * DK2:(p + 1) * DK2], (((1,), (1,)), ((), ())),
                              precision=HIGHEST, preferred_element_type=jnp.float32)
        sv, _, rank = _extract_top(s_t, key_iota, PEER_TOPK)
        for r in range(PEER_TOPK):
            sv_ref[p, r:r + 1, :] = sv[r]
        scores.append(s_t)
        tops.append(sv)
        ranks.append(rank)
    row8 = lax.broadcasted_iota(jnp.int32, (8, Tt), 0)
    row16 = lax.broadcasted_iota(jnp.int32, (PEER_TOPK, Tt), 0)
    strips = [sv_ref[0] + tops[1][0]]
    order = [row16 * PEER_TOPK]
    for b in range(1, 8):
        n_a = PEER_TOPK // (b + 1)
        strips.append(jnp.where(row8 < n_a, sv_ref[0, 0:8, :] + tops[1][b], NEG_INF))
        order.append(row8 * PEER_TOPK + b)
    strips.append(tops[0][0] + sv_ref[1, 8:16, :])
    order.append(row8 + 8)
    cv, picks, _ = _extract_top(jnp.concatenate(strips, axis=0), jnp.concatenate(order, axis=0), PEER_TOPK)
    z = jnp.ones_like(cv[0])
    for r in range(1, PEER_TOPK):
        z = z + jnp.exp(cv[r] - cv[0])
    n_b = jnp.zeros((NK, Tt), jnp.float32)
    for c in picks:
        n_b = n_b + jnp.where(ranks[0] == jnp.right_shift(c, 4).astype(jnp.float32), 1.0, 0.0)
    rk_ref[0] = n_b
    rk_ref[1] = ranks[1]
    ea_ref[0] = jnp.exp(scores[0] - tops[0][0])
    ea_ref[1] = jnp.exp(scores[1] - tops[1][0]) / z


def _peer_select(q, subkeys, tile):
    n = q.shape[0]
    H, _, NK, DK2 = subkeys.shape
    assert PEER_TOPK == 16 and NK % 8 == 0 and n % tile == 0
    tab = jax.ShapeDtypeStruct((2 * H, NK, n), jnp.float32)
    return pl.pallas_call(
        _peer_select_kernel,
        grid=(n // tile, H),
        in_specs=[pl.BlockSpec((tile, 2 * DK2), lambda i, h: (i, h)),
                  pl.BlockSpec((1, 2, NK, DK2), lambda i, h: (h, 0, 0, 0))],
        out_specs=[pl.BlockSpec((2, NK, tile), lambda i, h: (h, 0, i)),
                   pl.BlockSpec((2, NK, tile), lambda i, h: (h, 0, i))],
        out_shape=[tab, tab],
        scratch_shapes=[pltpu.VMEM((2, PEER_TOPK, tile), jnp.float32)],
        compiler_params=pltpu.CompilerParams(dimension_semantics=("parallel", "parallel"),
                                             vmem_limit_bytes=VMEM_LIMIT_BYTES),
        name="peer_select",
    )(q, subkeys)


def _gelu_exact(x):
    return 0.5 * x * (1.0 + lax.erf(x * (2.0 ** -0.5)))


def _peer_gate_kernel(h_ref, u_ref, ea_ref, rk_ref, g_ref):
    j = pl.program_id(1)
    Tt = h_ref.shape[0]
    Et = u_ref.shape[0]
    H2, NK, _ = ea_ref.shape
    n_i0 = Et // NK
    act_t = lax.dot_general(u_ref[...], h_ref[...], (((1,), (1,)), ((), ())),
                            preferred_element_type=jnp.float32)
    for r in range(n_i0):
        i0 = j * n_i0 + r
        rows = slice(r * NK, (r + 1) * NK)
        a0 = [ea_ref[2 * h, pl.ds(i0, 1), :] for h in range(H2 // 2)]
        n_b = [rk_ref[2 * h, pl.ds(i0, 1), :] for h in range(H2 // 2)]
        for c in range(Tt // LANES):
            cols = slice(c * LANES, (c + 1) * LANES)
            acc = None
            for h in range(H2 // 2):
                keep = rk_ref[2 * h + 1, :, cols] < n_b[h][:, cols]
                term = jnp.where(keep, ea_ref[2 * h + 1, :, cols], 0.0) * a0[h][:, cols]
                acc = term if acc is None else acc + term
            g_ref[rows, cols] = (acc * _gelu_exact(act_t[rows, cols])).astype(jnp.bfloat16)


def _peer_gate(h, u, ea, rk, token_tile, expert_tile):
    n, D = h.shape
    E = u.shape[0]
    H2, NK, _ = ea.shape
    assert E == NK * NK and expert_tile % NK == 0 and E % expert_tile == 0
    assert n % token_tile == 0 and token_tile % LANES == 0
    const = pl.Buffered(1)
    return pl.pallas_call(
        _peer_gate_kernel,
        grid=(n // token_tile, E // expert_tile),
        in_specs=[pl.BlockSpec((token_tile, D), lambda i, j: (i, 0), pipeline_mode=const),
                  pl.BlockSpec((expert_tile, D), lambda i, j: (j, 0)),
                  pl.BlockSpec((H2, NK, token_tile), lambda i, j: (0, 0, i), pipeline_mode=const),
                  pl.BlockSpec((H2, NK, token_tile), lambda i, j: (0, 0, i), pipeline_mode=const)],
        out_specs=pl.BlockSpec((expert_tile, token_tile), lambda i, j: (j, i)),
        out_shape=jax.ShapeDtypeStruct((E, n), jnp.bfloat16),
        compiler_params=pltpu.CompilerParams(dimension_semantics=("parallel", "parallel"),
                                             vmem_limit_bytes=VMEM_LIMIT_BYTES),
        name="peer_gate",
    )(h, u, ea, rk)


def _mm_tn_kernel(a_ref, b_ref, o_ref, acc_ref):
    k = pl.program_id(2)
    part = lax.dot_general(a_ref[...], b_ref[...], (((0,), (0,)), ((), ())), preferred_element_type=jnp.float32)

    @pl.when(k == 0)
    def _():
        acc_ref[...] = part

    @pl.when(k > 0)
    def _():
        acc_ref[...] += part

    @pl.when(k == pl.num_programs(2) - 1)
    def _():
        o_ref[...] = acc_ref[...]


def _mm_tn(a_t, b, tm, tn, tk):
    K, M = a_t.shape
    N = b.shape[1]
    assert M % tm == 0 and N % tn == 0 and K % tk == 0
    return pl.pallas_call(
        _mm_tn_kernel,
        grid=(M // tm, N // tn, K // tk),
        in_specs=[pl.BlockSpec((tk, tm), lambda i, j, k: (k, i)),
                  pl.BlockSpec((tk, tn), lambda i, j, k: (k, j))],
        out_specs=pl.BlockSpec((tm, tn), lambda i, j, k: (i, j)),
        out_shape=jax.ShapeDtypeStruct((M, N), jnp.float32),
        scratch_shapes=[pltpu.VMEM((tm, tn), jnp.float32)],
        compiler_params=pltpu.CompilerParams(dimension_semantics=("parallel", "parallel", "arbitrary"),
                                             vmem_limit_bytes=VMEM_LIMIT_BYTES),
        name="mm_tn",
    )(a_t, b)


def peer_ffn(h, wq, subkeys, u_bf16, v_bf16):
    B, T, D = h.shape
    n = B * T
    sel_tile = min(PEER_SEL_TILE, _round_up(n, 128))
    tok_tile = min(PEER_TOKEN_TILE, _round_up(n, 128))
    n_pad = _round_up(n, max(sel_tile, tok_tile))
    xt = h.reshape(n, D).astype(jnp.bfloat16)
    if n_pad != n:
        xt = jnp.pad(xt, ((0, n_pad - n), (0, 0)))
    q = _mm(xt, wq)
    ea, rk = _peer_select(q, subkeys, sel_tile)
    g_t = _peer_gate(xt, u_bf16, ea, rk, tok_tile, PEER_EXPERT_TILE)
    y = _mm_tn(g_t, v_bf16, tok_tile, min(D, PEER_OUT_TILE_N), PEER_OUT_TILE_K)
    return y[:n].reshape(B, T, D)


def _count_rows(mask):
    return jnp.sum(jnp.where(mask, 1.0, 0.0), axis=1, keepdims=True)


def _dsa_prompt_kernel(q_ref, k_ref, v_ref, qi_ref, ki_ref, wi_ref, o_ref, *, ktop, first_q_tile):
    Tq = q_ref.shape[1]
    T = k_ref.shape[1]
    n_bits = int(T - 1).bit_length()
    ki = ki_ref[0]
    wi = wi_ref[0]
    score = jnp.zeros((Tq, T), jnp.float32)
    for h in range(IDX_HEADS):
        d = lax.dot_general(qi_ref[0, :, h * IDX_HD:(h + 1) * IDX_HD], ki, (((1,), (1,)), ((), ())),
                            precision=HIGHEST, preferred_element_type=jnp.float32)
        score = score + wi[:, h:h + 1] * jnp.maximum(d, 0.0)
    bits = lax.bitcast_convert_type(score, jnp.int32)
    key = jnp.where(bits < 0, bits ^ jnp.int32(0x7FFFFFFF), bits)
    key = jnp.where(bits == jnp.int32(INT_MIN), 0, key)
    qpos = (first_q_tile + pl.program_id(1)) * Tq + lax.broadcasted_iota(jnp.int32, (Tq, T), 0)
    col = lax.broadcasted_iota(jnp.int32, (Tq, T), 1)
    allowed = col <= qpos
    key = jnp.where(allowed, key, jnp.int32(INT_MIN))
    kf = jnp.float32(ktop)

    tau = jnp.where(_count_rows(key >= 0) >= kf, jnp.int32(0), jnp.int32(INT_MIN))

    def tau_step(it, tau):
        cand = tau | jnp.left_shift(jnp.int32(1), 30 - it)
        return jnp.where(_count_rows(key >= cand) >= kf, cand, tau)

    tau = lax.fori_loop(0, 31, tau_step, tau)
    need = kf - _count_rows(key > tau)
    tie = (key == tau) & allowed

    def tie_step(it, p):
        cand = p | jnp.left_shift(jnp.int32(1), n_bits - 1 - it)
        return jnp.where(_count_rows(tie & (col < cand)) < need, cand, p)

    p = lax.fori_loop(0, n_bits, tie_step, jnp.zeros((Tq, 1), jnp.int32))
    sel = allowed & ((key > tau) | (tie & (col <= p)))

    group = ATT_HEADS // ATT_KV_HEADS
    scale = ATT_HD ** -0.5
    for g in range(ATT_KV_HEADS):
        kg = k_ref[0, :, g * ATT_HD:(g + 1) * ATT_HD].astype(jnp.bfloat16)
        vg = v_ref[0, :, g * ATT_HD:(g + 1) * ATT_HD].astype(jnp.bfloat16)
        for hh in range(group):
            lo = (g * group + hh) * ATT_HD
            qh = q_ref[0, :, lo:lo + ATT_HD].astype(jnp.bfloat16)
            logits = lax.dot_general(qh, kg, (((1,), (1,)), ((), ())), preferred_element_type=jnp.float32) * scale
            logits = jnp.where(sel, logits, NEG_INF)
            e = jnp.exp(logits - jnp.max(logits, axis=1, keepdims=True))
            out = jnp.dot(e.astype(jnp.bfloat16), vg, preferred_element_type=jnp.float32)
            o_ref[0, :, lo:lo + ATT_HD] = out / jnp.sum(e, axis=1, keepdims=True)


def dsa_prompt(q, k, v, qi, ki, wi):
    B, T = q.shape[:2]
    ktop = min(TOPK_MAX, T // 4)
    tq = min(DSA_Q_TILE, T)
    flat = lambda a: a.reshape(B, T, -1)
    args = (flat(q), flat(k), flat(v), flat(qi), ki, wi)
    n_groups = DSA_KEY_GROUPS if (T // tq) % DSA_KEY_GROUPS == 0 else 1
    tiles = T // tq // n_groups
    outs = []
    for grp in range(n_groups):
        first = grp * tiles
        t_keys = (first + tiles) * tq
        q_map = functools.partial(lambda b, i, first: (b, first + i, 0), first=first)
        kv_map = lambda b, i: (b, 0, 0)
        outs.append(pl.pallas_call(
            functools.partial(_dsa_prompt_kernel, ktop=ktop, first_q_tile=first),
            grid=(B, tiles),
            in_specs=[pl.BlockSpec((1, tq, ATT_Q), q_map),
                      pl.BlockSpec((1, t_keys, ATT_KV), kv_map),
                      pl.BlockSpec((1, t_keys, ATT_KV), kv_map),
                      pl.BlockSpec((1, tq, IDX_HEADS * IDX_HD), q_map),
                      pl.BlockSpec((1, t_keys, IDX_HD), kv_map),
                      pl.BlockSpec((1, tq, IDX_HEADS), q_map)],
            out_specs=pl.BlockSpec((1, tq, ATT_Q), lambda b, i: (b, i, 0)),
            out_shape=jax.ShapeDtypeStruct((B, tiles * tq, ATT_Q), jnp.float32),
            compiler_params=pltpu.CompilerParams(dimension_semantics=("parallel", "parallel"),
                                                 vmem_limit_bytes=VMEM_LIMIT_BYTES),
            name="dsa_prompt",
        )(*args))
    return jnp.concatenate(outs, axis=1).reshape(q.shape)


_NN = (((1,), (0,)), ((), ()))
_NT = (((1,), (1,)), ((), ()))
_TN = (((0,), (0,)), ((), ()))


def _dot_f32(x, y, dims=_NN):
    return lax.dot_general(x, y, dims, precision=HIGHEST, preferred_element_type=jnp.float32)


def _split_bf16(x):
    hi = x.astype(jnp.bfloat16)
    return hi, (x - hi.astype(jnp.float32)).astype(jnp.bfloat16)


def _dot3(x, y, dims=_NN):
    xh, xl = x if isinstance(x, tuple) else _split_bf16(x)
    yh, yl = y if isinstance(y, tuple) else _split_bf16(y)
    d = lambda p, q: lax.dot_general(p, q, dims, preferred_element_type=jnp.float32)
    return d(xh, yh) + (d(xh, yl) + d(xl, yh))


def _dot1(x, y, dims=_NN):
    return lax.dot_general(x.astype(jnp.bfloat16), y.astype(jnp.bfloat16), dims, preferred_element_type=jnp.float32)


def _unit_lower_solve(l_mats, rhs, size):
    xs = list(rhs)
    for step in range(max(1, int(size - 1).bit_length())):
        if step:
            l_mats = [_dot3(sp, sp) for sp in splits]
        splits = [_split_bf16(l) for l in l_mats]
        xs = [x + _dot3(sp, x) for sp, x in zip(splits, xs)]
    return xs


def _rwkv_kernel(r_ref, lw_ref, k_ref, v_ref, a_ref, b_ref, s0_ref, y_ref, s_ref):
    C = r_ref.shape[2]
    row = lax.broadcasted_iota(jnp.int32, (C, C), 0)
    col = lax.broadcasted_iota(jnp.int32, (C, C), 1)
    incl = row >= col
    strict = row > col
    tri = jnp.where(incl, 1.0, 0.0)
    incl2 = (lax.broadcasted_iota(jnp.int32, (C, 2 * C), 0)
             >= (lax.broadcasted_iota(jnp.int32, (C, 2 * C), 1) & (C - 1)))

    @pl.when(pl.program_id(2) == 0)
    def _():
        s_ref[...] = s0_ref[...]

    heads = range(r_ref.shape[1])
    lw = [lw_ref[0, h] for h in heads]
    v = [v_ref[0, h] for h in heads]
    s0 = [s_ref[0, h] for h in heads]
    log_p = [_dot_f32(tri, x) for x in lw]
    ar = [_split_bf16(jnp.concatenate([a_ref[0, h] * jnp.exp(log_p[h] - lw[h]), r_ref[0, h] * jnp.exp(log_p[h])],
                                      axis=0)) for h in heads]
    bk = [jnp.concatenate([b_ref[0, h], k_ref[0, h]], axis=0) for h in heads]
    g = [_dot3(ar[h], bk[h] * jnp.exp(-jnp.concatenate([log_p[h], log_p[h]], axis=0)), _NT) for h in heads]
    ars = [_dot3(ar[h], s0[h], _NT) for h in heads]
    rhs = [ars[h][:C] + _dot3(jnp.where(strict, g[h][:C, C:], 0.0), v[h]) for h in heads]
    u = _unit_lower_solve([jnp.where(strict, g[h][:C, :C], 0.0) for h in heads], rhs, C)
    uv = [jnp.concatenate([u[h], v[h]], axis=0) for h in heads]
    for h in heads:
        y_ref[0, h] = ars[h][C:] + _dot1(jnp.where(incl2, g[h][C:], 0.0), uv[h])
    for h in heads:
        to_end = jnp.exp(log_p[h][C - 1:C, :] - jnp.concatenate([log_p[h], log_p[h]], axis=0))
        s_ref[0, h] = s0[h] * jnp.exp(log_p[h][C - 1:C, :]) + _dot3(uv[h], bk[h] * to_end, _TN)


def rwkv7_scan(r, w_log, k, v, a, b, s0):
    B, T, H, N = r.shape
    C = min(RWKV_CHUNK, _round_up(T, 8))
    Tp = _round_up(T, C)

    def prep(x):
        x = jnp.moveaxis(x.astype(jnp.float32), 2, 1)
        return jnp.pad(x, ((0, 0), (0, 0), (0, Tp - T), (0, 0))) if Tp != T else x

    hg = RWKV_HEAD_GROUP
    seq = pl.BlockSpec((1, hg, C, N), lambda bi, gi, ci: (bi, gi, ci, 0))
    st = pl.BlockSpec((1, hg, N, N), lambda bi, gi, ci: (bi, gi, 0, 0))
    y, s_new = pl.pallas_call(
        _rwkv_kernel,
        grid=(B, H // hg, Tp // C),
        in_specs=[seq] * 6 + [st],
        out_specs=[seq, st],
        out_shape=[jax.ShapeDtypeStruct((B, H, Tp, N), jnp.float32),
                   jax.ShapeDtypeStruct((B, H, N, N), jnp.float32)],
        compiler_params=pltpu.CompilerParams(dimension_semantics=("parallel", "parallel", "arbitrary"),
                                             vmem_limit_bytes=VMEM_LIMIT_BYTES),
        name="rwkv7_scan",
    )(prep(r), prep(w_log), prep(k), prep(v), prep(a), prep(b), s0.astype(jnp.float32))
    return jnp.moveaxis(y[:, :, :T], 1, 2), s_new


def _gdn_kernel(q_ref, k_ref, v_ref, bl_ref, gl_ref, gr_ref, s0_ref, o_ref, s_ref):
    C = q_ref.shape[2]
    DV = v_ref.shape[3]
    row = lax.broadcasted_iota(jnp.int32, (C, C), 0)
    col = lax.broadcasted_iota(jnp.int32, (C, C), 1)
    incl = row >= col
    strict = row > col
    tri = jnp.where(incl, 1.0, 0.0)
    tri_t = jnp.where(row <= col, 1.0, 0.0)

    @pl.when(pl.program_id(2) == 0)
    def _():
        s_ref[...] = s0_ref[...]

    heads = range(q_ref.shape[1])
    k = [k_ref[0, h] for h in heads]
    s0 = [s_ref[0, h] for h in heads]
    gc = [_dot_f32(tri, gl_ref[0, h]) for h in heads]
    gc_row = [_dot_f32(gr_ref[0, h, 0], tri_t)[0:1, :] for h in heads]
    decay = [jnp.exp(jnp.where(incl, gc[h][:, :C] - gc_row[h], NEG_INF)) for h in heads]
    kb = [k[h] * bl_ref[0, h] for h in heads]
    a_low = [jnp.where(strict, _dot3(kb[h], k[h], _NT) * decay[h], 0.0) for h in heads]
    rhs = [jnp.concatenate([v_ref[0, h] * bl_ref[0, h], kb[h] * jnp.exp(gc[h])], axis=1) for h in heads]
    sol = _unit_lower_solve([-a for a in a_low], rhs, C)
    v_new = [sol[h][:, :DV] - _dot3(sol[h][:, DV:], s0[h]) for h in heads]
    for h in heads:
        qk = _dot1(q_ref[0, h], k[h], _NT) * decay[h]
        o_ref[0, h] = _dot1(q_ref[0, h] * jnp.exp(gc[h]), s0[h]) + _dot1(qk, v_new[h])
    for h in heads:
        g_last = gc[h][C - 1:C, :]
        s_ref[0, h] = s0[h] * jnp.exp(g_last) + _dot3(k[h] * jnp.exp(g_last - gc[h]), v_new[h], _TN)


def gated_delta_rule(q, k, v, beta, g, s0):
    B, T, H, DK = q.shape
    DV = v.shape[-1]
    assert DK == LANES and DV == LANES
    C = min(GDN_CHUNK, _round_up(T, 8))
    Tp = _round_up(T, C)
    nc = Tp // C

    def heads_first(x):
        x = jnp.moveaxis(x.astype(jnp.float32), 2, 1)
        return jnp.pad(x, ((0, 0), (0, 0), (0, Tp - T)) + ((0, 0),) * (x.ndim - 3)) if Tp != T else x

    lanes = lambda x: jnp.broadcast_to(heads_first(x)[..., None], (B, H, Tp, LANES))
    g_rows = jnp.broadcast_to(heads_first(g).reshape(B, H, nc, 1, C), (B, H, nc, 8, C))
    hg = GDN_HEAD_GROUP
    seq = lambda d: pl.BlockSpec((1, hg, C, d), lambda bi, gi, ci: (bi, gi, ci, 0))
    st = pl.BlockSpec((1, hg, DK, DV), lambda bi, gi, ci: (bi, gi, 0, 0))
    o, s_new = pl.pallas_call(
        _gdn_kernel,
        grid=(B, H // hg, nc),
        in_specs=[seq(DK), seq(DK), seq(DV), seq(LANES), seq(LANES),
                  pl.BlockSpec((1, hg, 1, 8, C), lambda bi, gi, ci: (bi, gi, ci, 0, 0)), st],
        out_specs=[seq(DV), st],
        out_shape=[jax.ShapeDtypeStruct((B, H, Tp, DV), jnp.float32),
                   jax.ShapeDtypeStruct((B, H, DK, DV), jnp.float32)],
        compiler_params=pltpu.CompilerParams(dimension_semantics=("parallel", "parallel", "arbitrary"),
                                             vmem_limit_bytes=VMEM_LIMIT_BYTES),
        name="gated_delta_rule",
    )(heads_first(q), heads_first(k), heads_first(v), lanes(beta), lanes(g), g_rows, s0.astype(jnp.float32))
    return jnp.moveaxis(o[:, :, :T], 1, 2), s_new


def _split(a, sizes, axis=-1):
    return jnp.split(a, [int(s) for s in np.cumsum(sizes)[:-1]], axis=axis)


def rmsnorm(x, gain):
    xf = x.astype(jnp.float32)
    y = xf * lax.rsqrt(jnp.mean(xf * xf, axis=-1, keepdims=True) + EPS)
    return (y * gain.astype(jnp.float32)).astype(x.dtype)


def l2norm(x):
    xf = x.astype(jnp.float32)
    return xf * lax.rsqrt(jnp.sum(xf * xf, axis=-1, keepdims=True) + EPS)


def partial_rope(x, pos):
    rd = x.shape[-1] // ROPE_DIV
    half = rd // 2
    inv_freq = ROPE_THETA ** (-jnp.arange(half, dtype=jnp.float32) * 2.0 / rd)
    ang = pos.astype(jnp.float32)[:, None] * inv_freq[None, :]
    cos = jnp.cos(ang)[:, None, :]
    sin = jnp.sin(ang)[:, None, :]
    xf = x[..., :rd].astype(jnp.float32)
    x1, x2 = xf[..., :half], xf[..., half:]
    rot = jnp.concatenate([x1 * cos - x2 * sin, x2 * cos + x1 * sin], axis=-1)
    return jnp.concatenate([rot.astype(x.dtype), x[..., rd:]], axis=-1)


def causal_conv(x, buf, w):
    T = x.shape[1]
    xp = jnp.concatenate([buf.astype(x.dtype), x], axis=1)
    y = xp[:, 0:T] * w[0]
    for i in range(1, CONV_W):
        y = y + xp[:, i:i + T] * w[i]
    return jax.nn.silu(y), xp[:, T:]


def _to_chunks(a, C):
    B, T = a.shape[:2]
    pad = (-T) % C
    a = jnp.pad(a.astype(jnp.float32), [(0, 0), (0, pad)] + [(0, 0)] * (a.ndim - 2))
    n = (T + pad) // C
    a = a.reshape((B, n, C) + a.shape[2:])
    return jnp.transpose(a, (1, 0, 3, 2) + tuple(range(4, a.ndim)))


def _from_chunks(o, T):
    n, B, H, C, X = o.shape
    return jnp.transpose(o, (1, 0, 3, 2, 4)).reshape(B, n * C, H, X)[:, :T]


def gla_chunked(q, k, v, log_a, s0):
    T = q.shape[1]
    C = min(GLA_CHUNK, T)
    qc, kc, vc = _to_chunks(q, C), _to_chunks(k, C), _to_chunks(v, C)
    bc = jnp.cumsum(_to_chunks(log_a, C), axis=-2)
    tri = jnp.tril(jnp.ones((C, C), bool))[..., None]

    def step(S, xs):
        q_n, k_n, v_n, b_n = xs
        rel = jnp.exp(jnp.where(tri, b_n[..., :, None, :] - b_n[..., None, :, :], -jnp.inf))
        att = jnp.einsum('bhik,bhijk,bhjk->bhij', q_n, rel, k_n)
        o = jnp.einsum('bhik,bhkv->bhiv', q_n * jnp.exp(b_n), S) + jnp.einsum('bhij,bhjv->bhiv', att, v_n)
        b_last = b_n[..., -1:, :]
        S = S * jnp.exp(b_last)[..., 0, :, None] + jnp.einsum('bhck,bhcv->bhkv', k_n * jnp.exp(b_last - b_n), v_n)
        return S, o

    S, o = lax.scan(step, s0.astype(jnp.float32), (qc, kc, vc, bc))
    return _from_chunks(o, T), S


def head_group_norm(y, w, b):
    B, T, H, N = y.shape
    mu = jnp.mean(y, axis=-1, keepdims=True)
    var = jnp.mean(jnp.square(y - mu), axis=-1, keepdims=True)
    yn = ((y - mu) * lax.rsqrt(var + RWKV_GN_EPS)).reshape(B, T, H * N)
    return yn * w.astype(jnp.float32) + b.astype(jnp.float32)


def indexer_scores(qi, wi, ki):
    dots = jnp.einsum('bqhd,bsd->bqhs', qi, ki).astype(jnp.float32)
    return jnp.einsum('bqh,bqhs->bqs', wi.astype(jnp.float32), jax.nn.relu(dots))


def select_keys(scores, qpos, ktop):
    S = scores.shape[-1]
    allowed = jnp.arange(S)[None, None, :] <= qpos[None, :, None]
    _, idx = lax.top_k(jnp.where(allowed, scores, -jnp.inf), ktop)
    return idx, idx <= qpos[None, :, None]


def sparse_attend(q, k_sel, v_sel, valid):
    B, Q, HQ, HD = q.shape
    qg = q.reshape(B, Q, ATT_KV_HEADS, HQ // ATT_KV_HEADS, HD)
    logits = jnp.einsum('bqhgd,bqkhd->bqhgk', qg, k_sel).astype(jnp.float32) * (HD ** -0.5)
    logits = jnp.where(valid[:, :, None, None, :], logits, -jnp.inf)
    p = jax.nn.softmax(logits, axis=-1).astype(v_sel.dtype)
    return jnp.einsum('bqhgk,bqkhd->bqhgd', p, v_sel).reshape(B, Q, HQ, HD)


def dsa_sample(q, k, v, qi, ki, wi, cache_k, cache_v, cache_ki, page_table):
    DB, DS = q.shape[:2]
    past = page_table.shape[1] * PAGE_SIZE
    ktop = min(TOPK_MAX, (past + DS) // 4)
    ki_past = cache_ki[page_table].reshape(DB, past, IDX_HD).astype(ki.dtype)
    ki_all = jnp.concatenate([ki_past, ki], axis=1)
    qpos = past + jnp.arange(DS)
    idx, valid = select_keys(indexer_scores(qi, wi, ki_all), qpos, ktop)
    is_new = idx >= past
    pidx = jnp.minimum(idx, past - 1)
    phys = jax.vmap(lambda pt, i: pt[i])(page_table, pidx // PAGE_SIZE)
    off = pidx % PAGE_SIZE
    nidx = jnp.clip(idx - past, 0, DS - 1)
    gather = jax.vmap(lambda rows, i: rows[i])
    sel = lambda cache, new: jnp.where(is_new[..., None, None], gather(new, nidx), cache[phys, off].astype(new.dtype))
    return sparse_attend(q, sel(cache_k, k), sel(cache_v, v), valid)


def gdn_branch(p, conv_buf, s0, lp):
    B, T, _ = p.shape
    conv_in, z, b_raw, a_raw = _split(p, (GDN_CONV_COLS, GDN_V, GDN_HEADS, GDN_HEADS))
    conv_out, new_buf = causal_conv(conv_in, conv_buf, lp['gdn_conv_w'])
    q, k, v = _split(conv_out, (GDN_QK, GDN_QK, GDN_V))
    q = l2norm(q.reshape(B, T, GDN_HEADS, GDN_DK)) * (GDN_DK ** -0.5)
    k = l2norm(k.reshape(B, T, GDN_HEADS, GDN_DK))
    v = v.reshape(B, T, GDN_HEADS, GDN_DV)
    beta = jax.nn.sigmoid(b_raw.astype(jnp.float32))
    g = -jnp.exp(lp['gdn_A_log'].astype(jnp.float32)) * jax.nn.softplus(a_raw.astype(jnp.float32) + lp['gdn_dt_bias'].astype(jnp.float32))
    o, s_new = gated_delta_rule(q, k, v, beta, g, s0)
    o = rmsnorm(o, lp['gdn_norm']) * jax.nn.silu(z.reshape(B, T, GDN_HEADS, GDN_DV).astype(jnp.float32))
    return o.reshape(B, T, GDN_V), new_buf, s_new


def gla_branch(p, s0, lp):
    B, T, _ = p.shape
    q, k, v, r, gd = _split(p, (GLA_QK, GLA_QK, GLA_V, GLA_V, GLA_RANK))
    hs = (B, T, GLA_HEADS)
    log_a = jax.nn.log_sigmoid((gd @ lp['gla_gate_up'] + lp['gla_gate_bias']).astype(jnp.float32)) / GLA_TAU
    o, s_new = gla_chunked(q.reshape(hs + (GLA_DK,)) * (GLA_DK ** -0.5), k.reshape(hs + (GLA_DK,)),
                           v.reshape(hs + (GLA_DV,)), log_a.reshape(hs + (GLA_DK,)), s0)
    o = rmsnorm(o, lp['gla_norm']) * jax.nn.silu(r.reshape(hs + (GLA_DV,)).astype(jnp.float32))
    return o.reshape(B, T, GLA_V), s_new


def dsa_branch(p, pos, attend):
    B, T, _ = p.shape
    q, k, v, qi, ki, wi = _split(p, (ATT_Q, ATT_KV, ATT_KV, IDX_HEADS * IDX_HD, IDX_HD, IDX_HEADS))
    q = partial_rope(q.reshape(B, T, ATT_HEADS, ATT_HD), pos)
    k = partial_rope(k.reshape(B, T, ATT_KV_HEADS, ATT_HD), pos)
    v = v.reshape(B, T, ATT_KV_HEADS, ATT_HD)
    qi = partial_rope(qi.reshape(B, T, IDX_HEADS, IDX_HD), pos)
    ki = partial_rope(ki.reshape(B, T, 1, IDX_HD), pos)[:, :, 0]
    o = attend(q, k, v, qi, ki, wi)
    return o.reshape(B, T, ATT_Q), (k, v, ki)


def rwkv_branch(p, shift_buf, s0, lp):
    B, T, _ = p.shape
    prev = jnp.concatenate([shift_buf.astype(p.dtype), p[:, :-1]], axis=1)
    pm = p + (prev - p) * lp['rwkv_mu']
    r, wd, k, v, ad, gd = _split(pm, (RWKV_W, RWKV_W_RANK, RWKV_W, RWKV_W, RWKV_A_RANK, RWKV_G_RANK))
    hs = (B, T, RWKV_HEADS, RWKV_HD)
    w_log = -jax.nn.softplus(-(lp['rwkv_w0'] + jnp.tanh(wd) @ lp['rwkv_w2']).astype(jnp.float32)) - 0.5
    log_decay = -jnp.exp(w_log)
    a = jax.nn.sigmoid((lp['rwkv_a0'] + ad @ lp['rwkv_a2']).astype(jnp.float32))
    g = (jax.nn.sigmoid(gd) @ lp['rwkv_g2']).astype(jnp.float32)
    kf = k.astype(jnp.float32)
    kk = l2norm((kf * lp['rwkv_kk']).reshape(hs))
    kf = (kf * (1.0 + (a - 1.0) * lp['rwkv_ka'])).reshape(hs)
    rf = r.astype(jnp.float32).reshape(hs)
    vf = v.astype(jnp.float32).reshape(hs)
    y, s_new = rwkv7_scan(rf, log_decay.reshape(hs), kf, vf, -kk, kk * a.reshape(hs), s0)
    y = head_group_norm(y, lp['rwkv_ln_w'], lp['rwkv_ln_b'])
    bonus = (jnp.sum(rf * kf * lp['rwkv_rk'], axis=-1, keepdims=True) * vf).reshape(B, T, RWKV_W)
    return (y + bonus) * g, p[:, -1:], s_new


def token_mixers(h, pos, st, lp, attend):
    gdn_s, gdn_buf, gla_s, rwkv_s, rwkv_buf = st
    B, T, _ = h.shape
    hb = h.astype(jnp.bfloat16)
    p_gdn, p_gla, p_att, p_rwkv, p_gate = [_mm3(hb, w)[..., :size] for w, size in zip(lp['w_in_groups'], IN_SPLITS)]
    o_gdn, gdn_buf, gdn_s = gdn_branch(p_gdn, gdn_buf, gdn_s, lp)
    o_gla, gla_s = gla_branch(p_gla, gla_s, lp)
    o_att, rows = dsa_branch(p_att, pos, attend)
    o_rwkv, rwkv_buf, rwkv_s = rwkv_branch(p_rwkv, rwkv_buf, rwkv_s, lp)
    assert len(set(BRANCH_SPLITS)) == 1
    merged = _merge_branches([o.reshape(B * T, -1) for o in (o_gdn, o_gla, o_att, o_rwkv)],
                             lp['w_branch'], p_gate.reshape(B * T, GATE_COLS))
    out = _mm(merged, lp['w_out']).reshape(B, T, D_MODEL)
    return out, rows + (gdn_s, gdn_buf, gla_s, rwkv_s, rwkv_buf)


def trunk_layer(x, c, pos, st, lp, attend):
    B = x.shape[0]
    mod = (jax.nn.silu(c) @ lp['w_ada'] + lp['b_ada']).reshape(B, N_MOD, 1, D_MODEL)
    shift1, scale1, gate1, shift2, scale2, gate2 = [mod[:, i] for i in range(N_MOD)]
    h = rmsnorm(x, lp['norm1']) * (1.0 + scale1) + shift1
    mix, new_st = token_mixers(h, pos, st, lp, attend)
    x = x + gate1 * mix
    h = rmsnorm(x, lp['norm2']) * (1.0 + scale2) + shift2
    x = x + gate2 * peer_ffn(h, lp['peer_wq'], lp['peer_subkeys'], lp['peer_u'], lp['peer_v'])
    return x, new_st


def kernel(x_prompt, x_sample, cache_k, cache_v, cache_kidx, state_gdn, state_gdn_conv, state_gla, state_rwkv, state_rwkv_shift, page_table, c_prompt, c_sample, w_ada, b_ada, norm1, norm2, w_in, gdn_conv_w, gdn_A_log, gdn_dt_bias, gdn_norm, gla_gate_up, gla_gate_bias, gla_norm, rwkv_mu, rwkv_w0, rwkv_w2, rwkv_a0, rwkv_a2, rwkv_g2, rwkv_kk, rwkv_ka, rwkv_rk, rwkv_ln_w, rwkv_ln_b, w_branch, w_out, peer_wq, peer_subkeys, peer_u, peer_v, final_norm):
    Bp, T, _ = x_prompt.shape
    DS = x_sample.shape[1]
    depth = w_in.shape[0]
    past = page_table.shape[1] * PAGE_SIZE
    pos_p = jnp.arange(T, dtype=jnp.int32)
    pos_s = past + jnp.arange(DS, dtype=jnp.int32)
    st_p0 = (jnp.zeros((Bp, GDN_HEADS, GDN_DK, GDN_DV), jnp.float32),
             jnp.zeros((Bp, CONV_W - 1, GDN_CONV_COLS), x_prompt.dtype),
             jnp.zeros((Bp, GLA_HEADS, GLA_DK, GLA_DV), jnp.float32),
             jnp.zeros((Bp, RWKV_HEADS, RWKV_HD, RWKV_HD), jnp.float32),
             jnp.zeros((Bp, 1, RWKV_COLS), x_prompt.dtype))
    weights = {'w_ada': w_ada, 'b_ada': b_ada, 'norm1': norm1, 'norm2': norm2, 'w_in': w_in,
               'gdn_conv_w': gdn_conv_w, 'gdn_A_log': gdn_A_log, 'gdn_dt_bias': gdn_dt_bias, 'gdn_norm': gdn_norm,
               'gla_gate_up': gla_gate_up, 'gla_gate_bias': gla_gate_bias, 'gla_norm': gla_norm,
               'rwkv_mu': rwkv_mu, 'rwkv_w0': rwkv_w0, 'rwkv_w2': rwkv_w2, 'rwkv_a0': rwkv_a0, 'rwkv_a2': rwkv_a2,
               'rwkv_g2': rwkv_g2, 'rwkv_kk': rwkv_kk, 'rwkv_ka': rwkv_ka, 'rwkv_rk': rwkv_rk,
               'rwkv_ln_w': rwkv_ln_w, 'rwkv_ln_b': rwkv_ln_b, 'w_branch': w_branch, 'w_out': w_out,
               'peer_wq': peer_wq, 'peer_subkeys': peer_subkeys, 'peer_u': peer_u, 'peer_v': peer_v}
    xp, xs = x_prompt, x_sample
    out_p, out_s = [], []
    for l in range(depth):
        lp = {name: arr[l] for name, arr in weights.items()}
        for name in ('peer_u', 'peer_v', 'w_branch', 'w_out', 'peer_wq'):
            lp[name] = lp[name].astype(jnp.bfloat16)
        lp['w_in_groups'] = [_pad_cols_bf16(w) for w in _split(lp['w_in'], IN_SPLITS)]
        xp, new_p = trunk_layer(xp, c_prompt, pos_p, st_p0, lp, dsa_prompt)
        attend_s = functools.partial(dsa_sample, cache_k=cache_k[l], cache_v=cache_v[l],
                                     cache_ki=cache_kidx[l], page_table=page_table)
        st_s0 = (state_gdn[l], state_gdn_conv[l], state_gla[l], state_rwkv[l], state_rwkv_shift[l])
        xs, new_s = trunk_layer(xs, c_sample, pos_s, st_s0, lp, attend_s)
        out_p.append(new_p)
        out_s.append(new_s)
    y_prompt = rmsnorm(xp, final_norm)
    y_sample = rmsnorm(xs, final_norm)
    k_p, v_p, kidx_p, gdn_p, gdn_conv_p, gla_p, rwkv_p, rwkv_shift_p = [jnp.stack(z) for z in zip(*out_p)]
    k_s, v_s, kidx_s, gdn_s, gdn_conv_s, gla_s, rwkv_s, rwkv_shift_s = [jnp.stack(z) for z in zip(*out_s)]
    return (y_prompt, y_sample, k_p, v_p, kidx_p, gdn_p, gdn_conv_p, gla_p, rwkv_p, rwkv_shift_p,
            k_s, v_s, kidx_s, gdn_s, gdn_conv_s, gla_s, rwkv_s, rwkv_shift_s)
```

```python
import math, functools
import jax, jax.numpy as jnp
from jax import lax
import numpy as np
from jax.experimental import pallas as pl
from jax.experimental.pallas import tpu as pltpu

D_MODEL = 4096
PAGE_SIZE = 128
N_MOD = 6
EPS = 1e-6
N_BRANCH = 4

GDN_HEADS = 8
GDN_DK = 128
GDN_DV = 128
CONV_W = 4
GDN_CHUNK = 64
GDN_QK = GDN_HEADS * GDN_DK
GDN_V = GDN_HEADS * GDN_DV
GDN_CONV_COLS = 2 * GDN_QK + GDN_V
GDN_COLS = GDN_CONV_COLS + GDN_V + 2 * GDN_HEADS

GLA_HEADS = 4
GLA_DK = 128
GLA_DV = 256
GLA_RANK = 16
GLA_TAU = 16.0
GLA_CHUNK = 64
GLA_QK = GLA_HEADS * GLA_DK
GLA_V = GLA_HEADS * GLA_DV
GLA_COLS = 2 * GLA_QK + 2 * GLA_V + GLA_RANK

ATT_HEADS = 8
ATT_KV_HEADS = 2
ATT_HD = 128
IDX_HEADS = 8
IDX_HD = 64
TOPK_MAX = 256
Q_BLOCK = 128
ROPE_THETA = 500000.0
ROPE_DIV = 4
ATT_Q = ATT_HEADS * ATT_HD
ATT_KV = ATT_KV_HEADS * ATT_HD
ATT_COLS = ATT_Q + 2 * ATT_KV + IDX_HEADS * IDX_HD + IDX_HD + IDX_HEADS

RWKV_HEADS = 16
RWKV_HD = 64
RWKV_W_RANK = 64
RWKV_A_RANK = 64
RWKV_G_RANK = 128
RWKV_GN_EPS = 64e-5
RWKV_W = RWKV_HEADS * RWKV_HD
RWKV_COLS = 3 * RWKV_W + RWKV_W_RANK + RWKV_A_RANK + RWKV_G_RANK

GATE_COLS = N_BRANCH * D_MODEL
IN_SPLITS = (GDN_COLS, GLA_COLS, ATT_COLS, RWKV_COLS, GATE_COLS)
IN_COLS = sum(IN_SPLITS)
BRANCH_SPLITS = (GDN_V, GLA_V, ATT_Q, RWKV_W)
MIX_WIDTH = sum(BRANCH_SPLITS)

PEER_HEADS = 8
PEER_NKEYS = 128
PEER_DKEY = 256
PEER_TOPK = 16
PEER_N = PEER_NKEYS * PEER_NKEYS
PEER_BLOCK = 64

VMEM_LIMIT_BYTES = 56 * 1024 * 1024
MM_TILE_M = 1024
MM_TILE_N = 512
MM_TILE_N_FEW_ROWS = 2048
MM_FEW_ROWS = 64
MERGE_TILE_M = 512
MERGE_TILE_N = 512
BF16_SUBLANES = 16
LANES = 128
PEER_SEL_TILE = 256
PEER_TOKEN_TILE = 512
PEER_EXPERT_TILE = 512
PEER_OUT_TILE_N = 2048
PEER_OUT_TILE_K = 2048
DSA_Q_TILE = 128
DSA_KEY_GROUPS = 4
RWKV_CHUNK = 64
RWKV_HEAD_GROUP = 8
GDN_HEAD_GROUP = 4

HIGHEST = lax.Precision.HIGHEST
NEG_INF = float("-inf")
INT_MIN = -2 ** 31


def _mm_kernel(x_ref, w_ref, o_ref):
    o_ref[...] = jnp.dot(x_ref[...], w_ref[...], preferred_element_type=jnp.float32)


def _round_up(n, m):
    return (n + m - 1) // m * m


def _pad_cols_bf16(w, multiple=None):
    multiple = multiple or MM_TILE_N
    n_pad = _round_up(w.shape[1], multiple)
    wb = w.astype(jnp.bfloat16)
    return wb if n_pad == w.shape[1] else jnp.pad(wb, ((0, 0), (0, n_pad - w.shape[1])))


def _mm(x, w):
    M, K = x.shape
    N = w.shape[1]
    assert N % MM_TILE_N == 0
    tm = min(MM_TILE_M, _round_up(M, BF16_SUBLANES))
    Mp = _round_up(M, tm)
    tn = next(t for t in (MM_TILE_N_FEW_ROWS, MM_TILE_N) if N % t == 0) if Mp <= MM_FEW_ROWS else MM_TILE_N
    xb = x.astype(jnp.bfloat16)
    if Mp != M:
        xb = jnp.pad(xb, ((0, Mp - M), (0, 0)))
    out = pl.pallas_call(
        _mm_kernel,
        grid=(N // tn, Mp // tm),
        in_specs=[pl.BlockSpec((tm, K), lambda j, i: (i, 0)),
                  pl.BlockSpec((K, tn), lambda j, i: (0, j))],
        out_specs=pl.BlockSpec((tm, tn), lambda j, i: (i, j)),
        out_shape=jax.ShapeDtypeStruct((Mp, N), jnp.float32),
        compiler_params=pltpu.CompilerParams(
            dimension_semantics=("parallel", "parallel"),
            vmem_limit_bytes=VMEM_LIMIT_BYTES),
        name="mm",
    )(xb, w.astype(jnp.bfloat16))
    return out if Mp == M else out[:M]


def _mm3(x, w):
    B, T, K = x.shape
    return _mm(x.reshape(B * T, K), w).reshape(B, T, w.shape[1])


def _merge_kernel(*refs):
    n = (len(refs) - 1) // 3
    out_ref = refs[-1]
    acc = None
    for b in range(n):
        o_ref, w_ref, g_ref = refs[b], refs[n + b], refs[2 * n + b]
        term = jax.nn.sigmoid(g_ref[...]) * jnp.dot(o_ref[...], w_ref[...], preferred_element_type=jnp.float32)
        acc = term if acc is None else acc + term
    out_ref[...] = acc.astype(out_ref.dtype)


def _merge_branches(outs, w_branch, p_gate):
    n = len(outs)
    M = outs[0].shape[0]
    D = w_branch.shape[1]
    kb = outs[0].shape[1]
    assert all(o.shape == (M, kb) for o in outs) and w_branch.shape[0] == n * kb and p_gate.shape == (M, n * D)
    tm = min(MERGE_TILE_M, _round_up(M, BF16_SUBLANES))
    Mp = _round_up(M, tm)
    tn = MERGE_TILE_N
    obs = [o.astype(jnp.bfloat16) for o in outs]
    if Mp != M:
        obs = [jnp.pad(o, ((0, Mp - M), (0, 0))) for o in obs]
        p_gate = jnp.pad(p_gate, ((0, Mp - M), (0, 0)))
    o_specs = [pl.BlockSpec((tm, kb), lambda i, j: (i, 0)) for _ in range(n)]
    w_specs = [pl.BlockSpec((kb, tn), functools.partial(lambda i, j, b: (b, j), b=b)) for b in range(n)]
    g_specs = [pl.BlockSpec((tm, tn), functools.partial(lambda i, j, b: (i, b * (D // tn) + j), b=b))
               for b in range(n)]
    out = pl.pallas_call(
        _merge_kernel,
        grid=(Mp // tm, D // tn),
        in_specs=o_specs + w_specs + g_specs,
        out_specs=pl.BlockSpec((tm, tn), lambda i, j: (i, j)),
        out_shape=jax.ShapeDtypeStruct((Mp, D), jnp.bfloat16),
        compiler_params=pltpu.CompilerParams(dimension_semantics=("parallel", "parallel"),
                                             vmem_limit_bytes=VMEM_LIMIT_BYTES),
        name="merge_branches",
    )(*obs, *([w_branch] * n), *([p_gate] * n))
    return out if Mp == M else out[:M]


def _extract_top(x, order, n_out):
    outs, picks = [], []
    rank = jnp.full(x.shape, float(n_out), jnp.float32)
    for r in range(n_out):
        m = jnp.max(x, axis=0, keepdims=True)
        c = jnp.min(jnp.where(x == m, order, jnp.int32(2 ** 30)), axis=0, keepdims=True)
        hit = order == c
        x = jnp.where(hit, NEG_INF, x)
        rank = jnp.where(hit, float(r), rank)
        outs.append(m)
        picks.append(c)
    return outs, picks, rank


def _peer_select_kernel(q_ref, sk_ref, ea_ref, rk_ref, sv_ref):
    Tt = q_ref.shape[0]
    NK, DK2 = sk_ref.shape[2], sk_ref.shape[3]
    key_iota = lax.broadcasted_iota(jnp.int32, (NK, Tt), 0)
    scores, tops, ranks = [], [], []
    for p in range(2):
        s_t = lax.dot_general(sk_ref[0, p], q_ref[:, p * DK2:(p + 1) * DK2], (((1,), (1,)), ((), ())),
                              precision=HIGHEST, preferred_element_type=jnp.float32)
        sv, _, rank = _extract_top(s_t, key_iota, PEER_TOPK)
        for r in range(PEER_TOPK):
            sv_ref[p, r:r + 1, :] = sv[r]
        scores.append(s_t)
        tops.append(sv)
        ranks.append(rank)
    row8 = lax.broadcasted_iota(jnp.int32, (8, Tt), 0)
    row16 = lax.broadcasted_iota(jnp.int32, (PEER_TOPK, Tt), 0)
    strips = [sv_ref[0] + tops[1][0]]
    order = [row16 * PEER_TOPK]
    for b in range(1, 8):
        n_a = PEER_TOPK // (b + 1)
        strips.append(jnp.where(row8 < n_a, sv_ref[0, 0:8, :] + tops[1][b], NEG_INF))
        order.append(row8 * PEER_TOPK + b)
    strips.append(tops[0][0] + sv_ref[1, 8:16, :])
    order.append(row8 + 8)
    cv, picks, _ = _extract_top(jnp.concatenate(strips, axis=0), jnp.concatenate(order, axis=0), PEER_TOPK)
    z = jnp.ones_like(cv[0])
    for r in range(1, PEER_TOPK):
        z = z + jnp.exp(cv[r] - cv[0])
    n_b = jnp.zeros((NK, Tt), jnp.float32)
    for c in picks:
        n_b = n_b + jnp.where(ranks[0] == jnp.right_shift(c, 4).astype(jnp.float32), 1.0, 0.0)
    rk_ref[0] = n_b
    rk_ref[1] = ranks[1]
    ea_ref[0] = jnp.exp(scores[0] - tops[0][0])
    ea_ref[1] = jnp.exp(scores[1] - tops[1][0]) / z


def _peer_select(q, subkeys, tile):
    n = q.shape[0]
    H, _, NK, DK2 = subkeys.shape
    assert PEER_TOPK == 16 and NK % 8 == 0 and n % tile == 0
    tab = jax.ShapeDtypeStruct((2 * H, NK, n), jnp.float32)
    return pl.pallas_call(
        _peer_select_kernel,
        grid=(n // tile, H),
        in_specs=[pl.BlockSpec((tile, 2 * DK2), lambda i, h: (i, h)),
                  pl.BlockSpec((1, 2, NK, DK2), lambda i, h: (h, 0, 0, 0))],
        out_specs=[pl.BlockSpec((2, NK, tile), lambda i, h: (h, 0, i)),
                   pl.BlockSpec((2, NK, tile), lambda i, h: (h, 0, i))],
        out_shape=[tab, tab],
        scratch_shapes=[pltpu.VMEM((2, PEER_TOPK, tile), jnp.float32)],
        compiler_params=pltpu.CompilerParams(dimension_semantics=("parallel", "parallel"),
                                             vmem_limit_bytes=VMEM_LIMIT_BYTES),
        name="peer_select",
    )(q, subkeys)


def _gelu_exact(x):
    return 0.5 * x * (1.0 + lax.erf(x * (2.0 ** -0.5)))


def _peer_gate_kernel(h_ref, u_ref, ea_ref, rk_ref, g_ref):
    j = pl.program_id(1)
    Tt = h_ref.shape[0]
    Et = u_ref.shape[0]
    H2, NK, _ = ea_ref.shape
    n_i0 = Et // NK
    act_t = lax.dot_general(u_ref[...], h_ref[...], (((1,), (1,)), ((), ())),
                            preferred_element_type=jnp.float32)
    for r in range(n_i0):
        i0 = j * n_i0 + r
        rows = slice(r * NK, (r + 1) * NK)
        a0 = [ea_ref[2 * h, pl.ds(i0, 1), :] for h in range(H2 // 2)]
        n_b = [rk_ref[2 * h, pl.ds(i0, 1), :] for h in range(H2 // 2)]
        for c in range(Tt // LANES):
            cols = slice(c * LANES, (c + 1) * LANES)
            acc = None
            for h in range(H2 // 2):
                keep = rk_ref[2 * h + 1, :, cols] < n_b[h][:, cols]
                term = jnp.where(keep, ea_ref[2 * h + 1, :, cols], 0.0) * a0[h][:, cols]
                acc = term if acc is None else acc + term
            g_ref[rows, cols] = (acc * _gelu_exact(act_t[rows, cols])).astype(jnp.bfloat16)


def _peer_gate(h, u, ea, rk, token_tile, expert_tile):
    n, D = h.shape
    E = u.shape[0]
    H2, NK, _ = ea.shape
    assert E == NK * NK and expert_tile % NK == 0 and E % expert_tile == 0
    assert n % token_tile == 0 and token_tile % LANES == 0
    const = pl.Buffered(1)
    return pl.pallas_call(
        _peer_gate_kernel,
        grid=(n // token_tile, E // expert_tile),
        in_specs=[pl.BlockSpec((token_tile, D), lambda i, j: (i, 0), pipeline_mode=const),
                  pl.BlockSpec((expert_tile, D), lambda i, j: (j, 0)),
                  pl.BlockSpec((H2, NK, token_tile), lambda i, j: (0, 0, i), pipeline_mode=const),
                  pl.BlockSpec((H2, NK, token_tile), lambda i, j: (0, 0, i), pipeline_mode=const)],
        out_specs=pl.BlockSpec((expert_tile, token_tile), lambda i, j: (j, i)),
        out_shape=jax.ShapeDtypeStruct((E, n), jnp.bfloat16),
        compiler_params=pltpu.CompilerParams(dimension_semantics=("parallel", "parallel"),
                                             vmem_limit_bytes=VMEM_LIMIT_BYTES),
        name="peer_gate",
    )(h, u, ea, rk)


def _mm_tn_kernel(a_ref, b_ref, o_ref, acc_ref):
    k = pl.program_id(2)
    part = lax.dot_general(a_ref[...], b_ref[...], (((0,), (0,)), ((), ())), preferred_element_type=jnp.float32)

    @pl.when(k == 0)
    def _():
        acc_ref[...] = part

    @pl.when(k > 0)
    def _():
        acc_ref[...] += part

    @pl.when(k == pl.num_programs(2) - 1)
    def _():
        o_ref[...] = acc_ref[...]


def _mm_tn(a_t, b, tm, tn, tk):
    K, M = a_t.shape
    N = b.shape[1]
    assert M % tm == 0 and N % tn == 0 and K % tk == 0
    return pl.pallas_call(
        _mm_tn_kernel,
        grid=(M // tm, N // tn, K // tk),
        in_specs=[pl.BlockSpec((tk, tm), lambda i, j, k: (k, i)),
                  pl.BlockSpec((tk, tn), lambda i, j, k: (k, j))],
        out_specs=pl.BlockSpec((tm, tn), lambda i, j, k: (i, j)),
        out_shape=jax.ShapeDtypeStruct((M, N), jnp.float32),
        scratch_shapes=[pltpu.VMEM((tm, tn), jnp.float32)],
        compiler_params=pltpu.CompilerParams(dimension_semantics=("parallel", "parallel", "arbitrary"),
                                             vmem_limit_bytes=VMEM_LIMIT_BYTES),
        name="mm_tn",
    )(a_t, b)


def peer_ffn(h, wq, subkeys, u_bf16, v_bf16):
    B, T, D = h.shape
    n = B * T
    sel_tile = min(PEER_SEL_TILE, _round_up(n, 128))
    tok_tile = min(PEER_TOKEN_TILE, _round_up(n, 128))
    n_pad = _round_up(n, max(sel_tile, tok_tile))
    xt = h.reshape(n, D).astype(jnp.bfloat16)
    if n_pad != n:
        xt = jnp.pad(xt, ((0, n_pad - n), (0, 0)))
    q = _mm(xt, wq)
    ea, rk = _peer_select(q, subkeys, sel_tile)
    g_t = _peer_gate(xt, u_bf16, ea, rk, tok_tile, PEER_EXPERT_TILE)
    y = _mm_tn(g_t, v_bf16, tok_tile, min(D, PEER_OUT_TILE_N), PEER_OUT_TILE_K)
    return y[:n].reshape(B, T, D)


def _count_rows(mask):
    return jnp.sum(jnp.where(mask, 1.0, 0.0), axis=1, keepdims=True)


def _dsa_prompt_kernel(q_ref, k_ref, v_ref, qi_ref, ki_ref, wi_ref, o_ref, *, ktop, first_q_tile):
    Tq = q_ref.shape[1]
    T = k_ref.shape[1]
    n_bits = int(T - 1).bit_length()
    ki = ki_ref[0]
    wi = wi_ref[0]
    score = jnp.zeros((Tq, T), jnp.float32)
    for h in range(IDX_HEADS):
        d = lax.dot_general(qi_ref[0, :, h * IDX_HD:(h + 1) * IDX_HD], ki, (((1,), (1,)), ((), ())),
                            precision=HIGHEST, preferred_element_type=jnp.float32)
        score = score + wi[:, h:h + 1] * jnp.maximum(d, 0.0)
    bits = lax.bitcast_convert_type(score, jnp.int32)
    key = jnp.where(bits < 0, bits ^ jnp.int32(0x7FFFFFFF), bits)
    key = jnp.where(bits == jnp.int32(INT_MIN), 0, key)
    qpos = (first_q_tile + pl.program_id(1)) * Tq + lax.broadcasted_iota(jnp.int32, (Tq, T), 0)
    col = lax.broadcasted_iota(jnp.int32, (Tq, T), 1)
    allowed = col <= qpos
    key = jnp.where(allowed, key, jnp.int32(INT_MIN))
    kf = jnp.float32(ktop)

    tau = jnp.where(_count_rows(key >= 0) >= kf, jnp.int32(0), jnp.int32(INT_MIN))

    def tau_step(it, tau):
        cand = tau | jnp.left_shift(jnp.int32(1), 30 - it)
        return jnp.where(_count_rows(key >= cand) >= kf, cand, tau)

    tau = lax.fori_loop(0, 31, tau_step, tau)
    need = kf - _count_rows(key > tau)
    tie = (key == tau) & allowed

    def tie_step(it, p):
        cand = p | jnp.left_shift(jnp.int32(1), n_bits - 1 - it)
        return jnp.where(_count_rows(tie & (col < cand)) < need, cand, p)

    p = lax.fori_loop(0, n_bits, tie_step, jnp.zeros((Tq, 1), jnp.int32))
    sel = allowed & ((key > tau) | (tie & (col <= p)))

    group = ATT_HEADS // ATT_KV_HEADS
    scale = ATT_HD ** -0.5
    for g in range(ATT_KV_HEADS):
        kg = k_ref[0, :, g * ATT_HD:(g + 1) * ATT_HD].astype(jnp.bfloat16)
        vg = v_ref[0, :, g * ATT_HD:(g + 1) * ATT_HD].astype(jnp.bfloat16)
        for hh in range(group):
            lo = (g * group + hh) * ATT_HD
            qh = q_ref[0, :, lo:lo + ATT_HD].astype(jnp.bfloat16)
            logits = lax.dot_general(qh, kg, (((1,), (1,)), ((), ())), preferred_element_type=jnp.float32) * scale
            logits = jnp.where(sel, logits, NEG_INF)
            e = jnp.exp(logits - jnp.max(logits, axis=1, keepdims=True))
            out = jnp.dot(e.astype(jnp.bfloat16), vg, preferred_element_type=jnp.float32)
            o_ref[0, :, lo:lo + ATT_HD] = out / jnp.sum(e, axis=1, keepdims=True)


def dsa_prompt(q, k, v, qi, ki, wi):
    B, T = q.shape[:2]
    ktop = min(TOPK_MAX, T // 4)
    tq = min(DSA_Q_TILE, T)
    flat = lambda a: a.reshape(B, T, -1)
    args = (flat(q), flat(k), flat(v), flat(qi), ki, wi)
    n_groups = DSA_KEY_GROUPS if (T // tq) % DSA_KEY_GROUPS == 0 else 1
    tiles = T // tq // n_groups
    outs = []
    for grp in range(n_groups):
        first = grp * tiles
        t_keys = (first + tiles) * tq
        q_map = functools.partial(lambda b, i, first: (b, first + i, 0), first=first)
        kv_map = lambda b, i: (b, 0, 0)
        outs.append(pl.pallas_call(
            functools.partial(_dsa_prompt_kernel, ktop=ktop, first_q_tile=first),
            grid=(B, tiles),
            in_specs=[pl.BlockSpec((1, tq, ATT_Q), q_map),
                      pl.BlockSpec((1, t_keys, ATT_KV), kv_map),
                      pl.BlockSpec((1, t_keys, ATT_KV), kv_map),
                      pl.BlockSpec((1, tq, IDX_HEADS * IDX_HD), q_map),
                      pl.BlockSpec((1, t_keys, IDX_HD), kv_map),
                      pl.BlockSpec((1, tq, IDX_HEADS), q_map)],
            out_specs=pl.BlockSpec((1, tq, ATT_Q), lambda b, i: (b, i, 0)),
            out_shape=jax.ShapeDtypeStruct((B, tiles * tq, ATT_Q), jnp.float32),
            compiler_params=pltpu.CompilerParams(dimension_semantics=("parallel", "parallel"),
                                                 vmem_limit_bytes=VMEM_LIMIT_BYTES),
            name="dsa_prompt",
        )(*args))
    return jnp.concatenate(outs, axis=1).reshape(q.shape)


_NN = (((1,), (0,)), ((), ()))
_NT = (((1,), (1,)), ((), ()))
_TN = (((0,), (0,)), ((), ()))


def _dot_f32(x, y, dims=_NN):
    return lax.dot_general(x, y, dims, precision=HIGHEST, preferred_element_type=jnp.float32)


def _split_bf16(x):
    hi = x.astype(jnp.bfloat16)
    return hi, (x - hi.astype(jnp.float32)).astype(jnp.bfloat16)


def _dot3(x, y, dims=_NN):
    xh, xl = x if isinstance(x, tuple) else _split_bf16(x)
    yh, yl = y if isinstance(y, tuple) else _split_bf16(y)
    d = lambda p, q: lax.dot_general(p, q, dims, preferred_element_type=jnp.float32)
    return d(xh, yh) + (d(xh, yl) + d(xl, yh))


def _dot1(x, y, dims=_NN):
    return lax.dot_general(x.astype(jnp.bfloat16), y.astype(jnp.bfloat16), dims, preferred_element_type=jnp.float32)


def _unit_lower_solve(l_mats, rhs, size):
    xs = list(rhs)
    for step in range(max(1, int(size - 1).bit_length())):
        if step:
            l_mats = [_dot3(sp, sp) for sp in splits]
        splits = [_split_bf16(l) for l in l_mats]
        xs = [x + _dot3(sp, x) for sp, x in zip(splits, xs)]
    return xs


def _rwkv_kernel(r_ref, lw_ref, k_ref, v_ref, a_ref, b_ref, s0_ref, y_ref, s_ref):
    C = r_ref.shape[1]
    HG, N = s_ref.shape[1], s_ref.shape[2]
    row = lax.broadcasted_iota(jnp.int32, (C, C), 0)
    col = lax.broadcasted_iota(jnp.int32, (C, C), 1)
    incl = row >= col
    strict = row > col
    tri = jnp.where(incl, 1.0, 0.0)
    incl2 = (lax.broadcasted_iota(jnp.int32, (C, 2 * C), 0)
             >= (lax.broadcasted_iota(jnp.int32, (C, 2 * C), 1) & (C - 1)))

    @pl.when(pl.program_id(2) == 0)
    def _():
        s_ref[...] = s0_ref[...]

    heads = range(HG)
    head = lambda x, h: x[:, h * N:(h + 1) * N]
    lw = lw_ref[0]
    log_p = _dot_f32(tri, lw)
    inv_p = jnp.exp(-log_p)
    to_end = jnp.exp(log_p[C - 1:C, :] - log_p)
    p_end = jnp.exp(log_p[C - 1:C, :])
    a_t, r_t = a_ref[0] * jnp.exp(log_p - lw), r_ref[0] * jnp.exp(log_p)
    b_t, k_t = b_ref[0] * inv_p, k_ref[0] * inv_p
    b_end, k_end = b_ref[0] * to_end, k_ref[0] * to_end
    v_all = v_ref[0]
    v = [head(v_all, h) for h in heads]
    s0 = [s_ref[0, h] for h in heads]
    ar = [_split_bf16(jnp.concatenate([head(a_t, h), head(r_t, h)], axis=0)) for h in heads]
    g = [_dot3(ar[h], jnp.concatenate([head(b_t, h), head(k_t, h)], axis=0), _NT) for h in heads]
    ars = [_dot3(ar[h], s0[h], _NT) for h in heads]
    rhs = [ars[h][:C] + _dot3(jnp.where(strict, g[h][:C, C:], 0.0), v[h]) for h in heads]
    u = _unit_lower_solve([jnp.where(strict, g[h][:C, :C], 0.0) for h in heads], rhs, C)
    uv = [jnp.concatenate([u[h], v[h]], axis=0) for h in heads]
    for h in heads:
        y_ref[0, :, h * N:(h + 1) * N] = ars[h][C:] + _dot1(jnp.where(incl2, g[h][C:], 0.0), uv[h])
    for h in heads:
        bk_end = jnp.concatenate([head(b_end, h), head(k_end, h)], axis=0)
        s_ref[0, h] = s0[h] * head(p_end, h) + _dot3(uv[h], bk_end, _TN)


def rwkv7_scan(r, w_log, k, v, a, b, s0):
    B, T, H, N = r.shape
    C = min(RWKV_CHUNK, _round_up(T, 8))
    assert C & (C - 1) == 0
    Tp = _round_up(T, C)

    def prep(x):
        x = x.astype(jnp.float32).reshape(B, T, H * N)
        return jnp.pad(x, ((0, 0), (0, Tp - T), (0, 0))) if Tp != T else x

    hg = RWKV_HEAD_GROUP
    seq = pl.BlockSpec((1, C, hg * N), lambda bi, gi, ci: (bi, ci, gi))
    st = pl.BlockSpec((1, hg, N, N), lambda bi, gi, ci: (bi, gi, 0, 0))
    y, s_new = pl.pallas_call(
        _rwkv_kernel,
        grid=(B, H // hg, Tp // C),
        in_specs=[seq] * 6 + [st],
        out_specs=[seq, st],
        out_shape=[jax.ShapeDtypeStruct((B, Tp, H * N), jnp.float32),
                   jax.ShapeDtypeStruct((B, H, N, N), jnp.float32)],
        compiler_params=pltpu.CompilerParams(dimension_semantics=("parallel", "parallel", "arbitrary"),
                                             vmem_limit_bytes=VMEM_LIMIT_BYTES),
        name="rwkv7_scan",
    )(prep(r), prep(w_log), prep(k), prep(v), prep(a), prep(b), s0.astype(jnp.float32))
    return y[:, :T].reshape(B, T, H, N), s_new


def _gdn_kernel(q_ref, k_ref, v_ref, b_ref, g_ref, gr_ref, s0_ref, o_ref, s_ref):
    C = q_ref.shape[1]
    HG = s_ref.shape[1]
    DK, DV = s_ref.shape[2], s_ref.shape[3]
    row = lax.broadcasted_iota(jnp.int32, (C, C), 0)
    col = lax.broadcasted_iota(jnp.int32, (C, C), 1)
    incl = row >= col
    strict = row > col
    tri = jnp.where(incl, 1.0, 0.0)
    tri_t = jnp.where(row <= col, 1.0, 0.0)

    @pl.when(pl.program_id(2) == 0)
    def _():
        s_ref[...] = s0_ref[...]

    def unit(x):
        return x * lax.rsqrt(jnp.sum(x * x, axis=-1, keepdims=True) + EPS)

    heads = range(HG)
    lanes = lambda ref, h: jnp.broadcast_to(ref[0, 0, :, h:h + 1], (C, LANES))
    beta = [lanes(b_ref, h) for h in heads]
    k = [unit(k_ref[0, :, h * DK:(h + 1) * DK]) for h in heads]
    s0 = [s_ref[0, h] for h in heads]
    gc = [_dot_f32(tri, lanes(g_ref, h)) for h in heads]
    gc_row = [_dot_f32(gr_ref[0, h, 0], tri_t)[0:1, :] for h in heads]
    decay = [jnp.exp(jnp.where(incl, gc[h][:, :C] - gc_row[h], NEG_INF)) for h in heads]
    kb = [k[h] * beta[h] for h in heads]
    a_low = [jnp.where(strict, _dot3(kb[h], k[h], _NT) * decay[h], 0.0) for h in heads]
    rhs = [jnp.concatenate([v_ref[0, :, h * DV:(h + 1) * DV] * beta[h], kb[h] * jnp.exp(gc[h])], axis=1)
           for h in heads]
    sol = _unit_lower_solve([-a for a in a_low], rhs, C)
    v_new = [sol[h][:, :DV] - _dot3(sol[h][:, DV:], s0[h]) for h in heads]
    for h in heads:
        q = unit(q_ref[0, :, h * DK:(h + 1) * DK]) * (DK ** -0.5)
        qk = _dot1(q, k[h], _NT) * decay[h]
        o_ref[0, :, h * DV:(h + 1) * DV] = _dot1(q * jnp.exp(gc[h]), s0[h]) + _dot1(qk, v_new[h])
    for h in heads:
        g_last = gc[h][C - 1:C, :]
        s_ref[0, h] = s0[h] * jnp.exp(g_last) + _dot3(k[h] * jnp.exp(g_last - gc[h]), v_new[h], _TN)


def gated_delta_rule(qkv, beta, g, s0):
    B, T, _ = qkv.shape
    _, H, DK, DV = s0.shape
    assert DK == LANES and DV == LANES and qkv.shape[2] == (2 * DK + DV) * H
    C = min(GDN_CHUNK, _round_up(T, 8))
    Tp = _round_up(T, C)
    nc = Tp // C
    hg = GDN_HEAD_GROUP
    ng = H // hg
    pad_t = lambda x: jnp.pad(x, ((0, 0), (0, Tp - T), (0, 0))) if Tp != T else x
    qkv = pad_t(qkv.astype(jnp.float32))
    per_group = lambda x: jnp.transpose(pad_t(x.astype(jnp.float32)).reshape(B, Tp, ng, hg), (0, 2, 1, 3))
    g_rows = jnp.broadcast_to(jnp.moveaxis(pad_t(g.astype(jnp.float32)), 2, 1).reshape(B, H, nc, 1, C),
                              (B, H, nc, 8, C))
    cols = lambda first, d: pl.BlockSpec((1, C, hg * d), functools.partial(
        lambda bi, gi, ci, first: (bi, ci, first + gi), first=first))
    tok = pl.BlockSpec((1, 1, C, hg), lambda bi, gi, ci: (bi, gi, ci, 0))
    st = pl.BlockSpec((1, hg, DK, DV), lambda bi, gi, ci: (bi, gi, 0, 0))
    o, s_new = pl.pallas_call(
        _gdn_kernel,
        grid=(B, ng, nc),
        in_specs=[cols(0, DK), cols(ng, DK), cols(2 * ng, DV), tok, tok,
                  pl.BlockSpec((1, hg, 1, 8, C), lambda bi, gi, ci: (bi, gi, ci, 0, 0)), st],
        out_specs=[cols(0, DV), st],
        out_shape=[jax.ShapeDtypeStruct((B, Tp, H * DV), jnp.float32),
                   jax.ShapeDtypeStruct((B, H, DK, DV), jnp.float32)],
        compiler_params=pltpu.CompilerParams(dimension_semantics=("parallel", "parallel", "arbitrary"),
                                             vmem_limit_bytes=VMEM_LIMIT_BYTES),
        name="gated_delta_rule",
    )(qkv, qkv, qkv, per_group(beta), per_group(g), g_rows, s0.astype(jnp.float32))
    return o[:, :T], s_new


def _split(a, sizes, axis=-1):
    ends = [int(e) for e in np.cumsum(sizes)]
    return [lax.slice_in_dim(a, e - int(s), e, axis=axis % a.ndim) for s, e in zip(sizes, ends)]


def rmsnorm(x, gain):
    xf = x.astype(jnp.float32)
    y = xf * lax.rsqrt(jnp.mean(xf * xf, axis=-1, keepdims=True) + EPS)
    return (y * gain.astype(jnp.float32)).astype(x.dtype)


def l2norm(x):
    xf = x.astype(jnp.float32)
    return xf * lax.rsqrt(jnp.sum(xf * xf, axis=-1, keepdims=True) + EPS)


def partial_rope(x, pos):
    rd = x.shape[-1] // ROPE_DIV
    half = rd // 2
    inv_freq = ROPE_THETA ** (-jnp.arange(half, dtype=jnp.float32) * 2.0 / rd)
    ang = pos.astype(jnp.float32)[:, None] * inv_freq[None, :]
    cos = jnp.cos(ang)[:, None, :]
    sin = jnp.sin(ang)[:, None, :]
    xf = x[..., :rd].astype(jnp.float32)
    x1, x2 = xf[..., :half], xf[..., half:]
    rot = jnp.concatenate([x1 * cos - x2 * sin, x2 * cos + x1 * sin], axis=-1)
    return jnp.concatenate([rot.astype(x.dtype), x[..., rd:]], axis=-1)


def causal_conv(x, buf, w):
    T = x.shape[1]
    xp = jnp.concatenate([buf.astype(x.dtype), x], axis=1)
    y = xp[:, 0:T] * w[0]
    for i in range(1, CONV_W):
        y = y + xp[:, i:i + T] * w[i]
    return jax.nn.silu(y), xp[:, T:]


def _to_chunks(a, C):
    B, T = a.shape[:2]
    pad = (-T) % C
    a = jnp.pad(a.astype(jnp.float32), [(0, 0), (0, pad)] + [(0, 0)] * (a.ndim - 2))
    n = (T + pad) // C
    a = a.reshape((B, n, C) + a.shape[2:])
    return jnp.transpose(a, (1, 0, 3, 2) + tuple(range(4, a.ndim)))


def _from_chunks(o, T):
    n, B, H, C, X = o.shape
    return jnp.transpose(o, (1, 0, 3, 2, 4)).reshape(B, n * C, H, X)[:, :T]


def gla_chunked(q, k, v, log_a, s0):
    T = q.shape[1]
    C = min(GLA_CHUNK, T)
    qc, kc, vc = _to_chunks(q, C), _to_chunks(k, C), _to_chunks(v, C)
    bc = jnp.cumsum(_to_chunks(log_a, C), axis=-2)
    tri = jnp.tril(jnp.ones((C, C), bool))[..., None]

    def step(S, xs):
        q_n, k_n, v_n, b_n = xs
        rel = jnp.exp(jnp.where(tri, b_n[..., :, None, :] - b_n[..., None, :, :], -jnp.inf))
        att = jnp.einsum('bhik,bhijk,bhjk->bhij', q_n, rel, k_n)
        o = jnp.einsum('bhik,bhkv->bhiv', q_n * jnp.exp(b_n), S) + jnp.einsum('bhij,bhjv->bhiv', att, v_n)
        b_last = b_n[..., -1:, :]
        S = S * jnp.exp(b_last)[..., 0, :, None] + jnp.einsum('bhck,bhcv->bhkv', k_n * jnp.exp(b_last - b_n), v_n)
        return S, o

    S, o = lax.scan(step, s0.astype(jnp.float32), (qc, kc, vc, bc))
    return _from_chunks(o, T), S


def head_group_norm(y, w, b):
    B, T, H, N = y.shape
    mu = jnp.mean(y, axis=-1, keepdims=True)
    var = jnp.mean(jnp.square(y - mu), axis=-1, keepdims=True)
    yn = ((y - mu) * lax.rsqrt(var + RWKV_GN_EPS)).reshape(B, T, H * N)
    return yn * w.astype(jnp.float32) + b.astype(jnp.float32)


def indexer_scores(qi, wi, ki):
    dots = jnp.einsum('bqhd,bsd->bqhs', qi, ki).astype(jnp.float32)
    return jnp.einsum('bqh,bqhs->bqs', wi.astype(jnp.float32), jax.nn.relu(dots))


def select_keys(scores, qpos, ktop):
    S = scores.shape[-1]
    allowed = jnp.arange(S)[None, None, :] <= qpos[None, :, None]
    _, idx = lax.top_k(jnp.where(allowed, scores, -jnp.inf), ktop)
    return idx, idx <= qpos[None, :, None]


def sparse_attend(q, k_sel, v_sel, valid):
    B, Q, HQ, HD = q.shape
    qg = q.reshape(B, Q, ATT_KV_HEADS, HQ // ATT_KV_HEADS, HD)
    logits = jnp.einsum('bqhgd,bqkhd->bqhgk', qg, k_sel).astype(jnp.float32) * (HD ** -0.5)
    logits = jnp.where(valid[:, :, None, None, :], logits, -jnp.inf)
    p = jax.nn.softmax(logits, axis=-1).astype(v_sel.dtype)
    return jnp.einsum('bqhgk,bqkhd->bqhgd', p, v_sel).reshape(B, Q, HQ, HD)


def dsa_sample(q, k, v, qi, ki, wi, cache_k, cache_v, cache_ki, page_table):
    DB, DS = q.shape[:2]
    past = page_table.shape[1] * PAGE_SIZE
    ktop = min(TOPK_MAX, (past + DS) // 4)
    ki_past = cache_ki[page_table].reshape(DB, past, IDX_HD).astype(ki.dtype)
    ki_all = jnp.concatenate([ki_past, ki], axis=1)
    qpos = past + jnp.arange(DS)
    idx, valid = select_keys(indexer_scores(qi, wi, ki_all), qpos, ktop)
    is_new = idx >= past
    pidx = jnp.minimum(idx, past - 1)
    phys = jax.vmap(lambda pt, i: pt[i])(page_table, pidx // PAGE_SIZE)
    off = pidx % PAGE_SIZE
    nidx = jnp.clip(idx - past, 0, DS - 1)
    gather = jax.vmap(lambda rows, i: rows[i])
    sel = lambda cache, new: jnp.where(is_new[..., None, None], gather(new, nidx), cache[phys, off].astype(new.dtype))
    return sparse_attend(q, sel(cache_k, k), sel(cache_v, v), valid)


def gdn_branch(p, conv_buf, s0, lp):
    B, T, _ = p.shape
    conv_in, z, b_raw, a_raw = _split(p, (GDN_CONV_COLS, GDN_V, GDN_HEADS, GDN_HEADS))
    conv_out, new_buf = causal_conv(conv_in, conv_buf, lp['gdn_conv_w'])
    beta = jax.nn.sigmoid(b_raw.astype(jnp.float32))
    g = -jnp.exp(lp['gdn_A_log'].astype(jnp.float32)) * jax.nn.softplus(a_raw.astype(jnp.float32) + lp['gdn_dt_bias'].astype(jnp.float32))
    o, s_new = gated_delta_rule(conv_out, beta, g, s0)
    o = o.reshape(B, T, GDN_HEADS, GDN_DV)
    o = rmsnorm(o, lp['gdn_norm']) * jax.nn.silu(z.reshape(B, T, GDN_HEADS, GDN_DV).astype(jnp.float32))
    return o.reshape(B, T, GDN_V), new_buf, s_new


def gla_branch(p, s0, lp):
    B, T, _ = p.shape
    q, k, v, r, gd = _split(p, (GLA_QK, GLA_QK, GLA_V, GLA_V, GLA_RANK))
    hs = (B, T, GLA_HEADS)
    log_a = jax.nn.log_sigmoid((gd @ lp['gla_gate_up'] + lp['gla_gate_bias']).astype(jnp.float32)) / GLA_TAU
    o, s_new = gla_chunked(q.reshape(hs + (GLA_DK,)) * (GLA_DK ** -0.5), k.reshape(hs + (GLA_DK,)),
                           v.reshape(hs + (GLA_DV,)), log_a.reshape(hs + (GLA_DK,)), s0)
    o = rmsnorm(o, lp['gla_norm']) * jax.nn.silu(r.reshape(hs + (GLA_DV,)).astype(jnp.float32))
    return o.reshape(B, T, GLA_V), s_new


def dsa_branch(p, pos, attend):
    B, T, _ = p.shape
    q, k, v, qi, ki, wi = _split(p, (ATT_Q, ATT_KV, ATT_KV, IDX_HEADS * IDX_HD, IDX_HD, IDX_HEADS))
    q = partial_rope(q.reshape(B, T, ATT_HEADS, ATT_HD), pos)
    k = partial_rope(k.reshape(B, T, ATT_KV_HEADS, ATT_HD), pos)
    v = v.reshape(B, T, ATT_KV_HEADS, ATT_HD)
    qi = partial_rope(qi.reshape(B, T, IDX_HEADS, IDX_HD), pos)
    ki = partial_rope(ki.reshape(B, T, 1, IDX_HD), pos)[:, :, 0]
    o = attend(q, k, v, qi, ki, wi)
    return o.reshape(B, T, ATT_Q), (k, v, ki)


def rwkv_branch(p, shift_buf, s0, lp):
    B, T, _ = p.shape
    prev = jnp.concatenate([shift_buf.astype(p.dtype), p[:, :-1]], axis=1)
    pm = p + (prev - p) * lp['rwkv_mu']
    r, wd, k, v, ad, gd = _split(pm, (RWKV_W, RWKV_W_RANK, RWKV_W, RWKV_W, RWKV_A_RANK, RWKV_G_RANK))
    hs = (B, T, RWKV_HEADS, RWKV_HD)
    w_log = -jax.nn.softplus(-(lp['rwkv_w0'] + jnp.tanh(wd) @ lp['rwkv_w2']).astype(jnp.float32)) - 0.5
    log_decay = -jnp.exp(w_log)
    a = jax.nn.sigmoid((lp['rwkv_a0'] + ad @ lp['rwkv_a2']).astype(jnp.float32))
    g = (jax.nn.sigmoid(gd) @ lp['rwkv_g2']).astype(jnp.float32)
    kf = k.astype(jnp.float32)
    kk = l2norm((kf * lp['rwkv_kk']).reshape(hs))
    kf = (kf * (1.0 + (a - 1.0) * lp['rwkv_ka'])).reshape(hs)
    rf = r.astype(jnp.float32).reshape(hs)
    vf = v.astype(jnp.float32).reshape(hs)
    y, s_new = rwkv7_scan(rf, log_decay.reshape(hs), kf, vf, -kk, kk * a.reshape(hs), s0)
    y = head_group_norm(y, lp['rwkv_ln_w'], lp['rwkv_ln_b'])
    bonus = (jnp.sum(rf * kf * lp['rwkv_rk'], axis=-1, keepdims=True) * vf).reshape(B, T, RWKV_W)
    return (y + bonus) * g, p[:, -1:], s_new


def token_mixers(h, pos, st, lp, attend):
    gdn_s, gdn_buf, gla_s, rwkv_s, rwkv_buf = st
    B, T, _ = h.shape
    hb = h.astype(jnp.bfloat16)
    p_gdn, p_gla, p_att, p_rwkv, p_gate = [_mm3(hb, w)[..., :size] for w, size in zip(lp['w_in_groups'], IN_SPLITS)]
    o_gdn, gdn_buf, gdn_s = gdn_branch(p_gdn, gdn_buf, gdn_s, lp)
    o_gla, gla_s = gla_branch(p_gla, gla_s, lp)
    o_att, rows = dsa_branch(p_att, pos, attend)
    o_rwkv, rwkv_buf, rwkv_s = rwkv_branch(p_rwkv, rwkv_buf, rwkv_s, lp)
    assert len(set(BRANCH_SPLITS)) == 1
    merged = _merge_branches([o.reshape(B * T, -1) for o in (o_gdn, o_gla, o_att, o_rwkv)],
                             lp['w_branch'], p_gate.reshape(B * T, GATE_COLS))
    out = _mm(merged, lp['w_out']).reshape(B, T, D_MODEL)
    return out, rows + (gdn_s, gdn_buf, gla_s, rwkv_s, rwkv_buf)


def trunk_layer(x, mod, pos, st, lp, attend):
    B = x.shape[0]
    mod = mod.reshape(B, N_MOD, 1, D_MODEL)
    shift1, scale1, gate1, shift2, scale2, gate2 = [mod[:, i] for i in range(N_MOD)]
    h = rmsnorm(x, lp['norm1']) * (1.0 + scale1) + shift1
    mix, new_st = token_mixers(h, pos, st, lp, attend)
    x = x + gate1 * mix
    h = rmsnorm(x, lp['norm2']) * (1.0 + scale2) + shift2
    x = x + gate2 * peer_ffn(h, lp['peer_wq'], lp['peer_subkeys'], lp['peer_u'], lp['peer_v'])
    return x, new_st


def kernel(x_prompt, x_sample, cache_k, cache_v, cache_kidx, state_gdn, state_gdn_conv, state_gla, state_rwkv, state_rwkv_shift, page_table, c_prompt, c_sample, w_ada, b_ada, norm1, norm2, w_in, gdn_conv_w, gdn_A_log, gdn_dt_bias, gdn_norm, gla_gate_up, gla_gate_bias, gla_norm, rwkv_mu, rwkv_w0, rwkv_w2, rwkv_a0, rwkv_a2, rwkv_g2, rwkv_kk, rwkv_ka, rwkv_rk, rwkv_ln_w, rwkv_ln_b, w_branch, w_out, peer_wq, peer_subkeys, peer_u, peer_v, final_norm):
    Bp, T, _ = x_prompt.shape
    DS = x_sample.shape[1]
    depth = w_in.shape[0]
    past = page_table.shape[1] * PAGE_SIZE
    pos_p = jnp.arange(T, dtype=jnp.int32)
    pos_s = past + jnp.arange(DS, dtype=jnp.int32)
    st_p0 = (jnp.zeros((Bp, GDN_HEADS, GDN_DK, GDN_DV), jnp.float32),
             jnp.zeros((Bp, CONV_W - 1, GDN_CONV_COLS), x_prompt.dtype),
             jnp.zeros((Bp, GLA_HEADS, GLA_DK, GLA_DV), jnp.float32),
             jnp.zeros((Bp, RWKV_HEADS, RWKV_HD, RWKV_HD), jnp.float32),
             jnp.zeros((Bp, 1, RWKV_COLS), x_prompt.dtype))
    weights = {'w_ada': w_ada, 'b_ada': b_ada, 'norm1': norm1, 'norm2': norm2, 'w_in': w_in,
               'gdn_conv_w': gdn_conv_w, 'gdn_A_log': gdn_A_log, 'gdn_dt_bias': gdn_dt_bias, 'gdn_norm': gdn_norm,
               'gla_gate_up': gla_gate_up, 'gla_gate_bias': gla_gate_bias, 'gla_norm': gla_norm,
               'rwkv_mu': rwkv_mu, 'rwkv_w0': rwkv_w0, 'rwkv_w2': rwkv_w2, 'rwkv_a0': rwkv_a0, 'rwkv_a2': rwkv_a2,
               'rwkv_g2': rwkv_g2, 'rwkv_kk': rwkv_kk, 'rwkv_ka': rwkv_ka, 'rwkv_rk': rwkv_rk,
               'rwkv_ln_w': rwkv_ln_w, 'rwkv_ln_b': rwkv_ln_b, 'w_branch': w_branch, 'w_out': w_out,
               'peer_wq': peer_wq, 'peer_subkeys': peer_subkeys, 'peer_u': peer_u, 'peer_v': peer_v}
    xp, xs = x_prompt, x_sample
    out_p, out_s = [], []
    for l in range(depth):
        lp = {name: arr[l] for name, arr in weights.items()}
        for name in ('peer_u', 'peer_v', 'w_branch', 'w_out', 'peer_wq'):
            lp[name] = lp[name].astype(jnp.bfloat16)
        lp['w_in_groups'] = [_pad_cols_bf16(w) for w in _split(lp['w_in'], IN_SPLITS)]
        mod = jax.nn.silu(jnp.concatenate([c_prompt, c_sample], axis=0)) @ lp['w_ada'] + lp['b_ada']
        xp, new_p = trunk_layer(xp, mod[:Bp], pos_p, st_p0, lp, dsa_prompt)
        attend_s = functools.partial(dsa_sample, cache_k=cache_k[l], cache_v=cache_v[l],
                                     cache_ki=cache_kidx[l], page_table=page_table)
        st_s0 = (state_gdn[l], state_gdn_conv[l], state_gla[l], state_rwkv[l], state_rwkv_shift[l])
        xs, new_s = trunk_layer(xs, mod[Bp:], pos_s, st_s0, lp, attend_s)
        out_p.append(new_p)
        out_s.append(new_s)
    y_prompt = rmsnorm(xp, final_norm)
    y_sample = rmsnorm(xs, final_norm)
    k_p, v_p, kidx_p, gdn_p, gdn_conv_p, gla_p, rwkv_p, rwkv_shift_p = [jnp.stack(z) for z in zip(*out_p)]
    k_s, v_s, kidx_s, gdn_s, gdn_conv_s, gla_s, rwkv_s, rwkv_shift_s = [jnp.stack(z) for z in zip(*out_s)]
    return (y_prompt, y_sample, k_p, v_p, kidx_p, gdn_p, gdn_conv_p, gla_p, rwkv_p, rwkv_shift_p,
            k_s, v_s, kidx_s, gdn_s, gdn_conv_s, gla_s, rwkv_s, rwkv_shift_s)
```

```python
import math, functools
import jax, jax.numpy as jnp
from jax import lax
import numpy as np
from jax.experimental import pallas as pl
from jax.experimental.pallas import tpu as pltpu

D_MODEL = 4096
PAGE_SIZE = 128
N_MOD = 6
EPS = 1e-6
N_BRANCH = 4

GDN_HEADS = 8
GDN_DK = 128
GDN_DV = 128
CONV_W = 4
GDN_CHUNK = 64
GDN_QK = GDN_HEADS * GDN_DK
GDN_V = GDN_HEADS * GDN_DV
GDN_CONV_COLS = 2 * GDN_QK + GDN_V
GDN_COLS = GDN_CONV_COLS + GDN_V + 2 * GDN_HEADS

GLA_HEADS = 4
GLA_DK = 128
GLA_DV = 256
GLA_RANK = 16
GLA_TAU = 16.0
GLA_CHUNK = 64
GLA_QK = GLA_HEADS * GLA_DK
GLA_V = GLA_HEADS * GLA_DV
GLA_COLS = 2 * GLA_QK + 2 * GLA_V + GLA_RANK

ATT_HEADS = 8
ATT_KV_HEADS = 2
ATT_HD = 128
IDX_HEADS = 8
IDX_HD = 64
TOPK_MAX = 256
Q_BLOCK = 128
ROPE_THETA = 500000.0
ROPE_DIV = 4
ATT_Q = ATT_HEADS * ATT_HD
ATT_KV = ATT_KV_HEADS * ATT_HD
ATT_COLS = ATT_Q + 2 * ATT_KV + IDX_HEADS * IDX_HD + IDX_HD + IDX_HEADS

RWKV_HEADS = 16
RWKV_HD = 64
RWKV_W_RANK = 64
RWKV_A_RANK = 64
RWKV_G_RANK = 128
RWKV_GN_EPS = 64e-5
RWKV_W = RWKV_HEADS * RWKV_HD
RWKV_COLS = 3 * RWKV_W + RWKV_W_RANK + RWKV_A_RANK + RWKV_G_RANK

GATE_COLS = N_BRANCH * D_MODEL
IN_SPLITS = (GDN_COLS, GLA_COLS, ATT_COLS, RWKV_COLS, GATE_COLS)
IN_COLS = sum(IN_SPLITS)
BRANCH_SPLITS = (GDN_V, GLA_V, ATT_Q, RWKV_W)
MIX_WIDTH = sum(BRANCH_SPLITS)

PEER_HEADS = 8
PEER_NKEYS = 128
PEER_DKEY = 256
PEER_TOPK = 16
PEER_N = PEER_NKEYS * PEER_NKEYS
PEER_BLOCK = 64

VMEM_LIMIT_BYTES = 56 * 1024 * 1024
MM_TILE_M = 1024
MM_TILE_N = 512
MM_TILE_N_FEW_ROWS = 2048
MM_TILE_N_F32_WEIGHT = 1024
MM_FEW_ROWS = 64
MERGE_TILE_M = 512
MERGE_TILE_N = 512
BF16_SUBLANES = 16
LANES = 128
PEER_SEL_TILE = 256
PEER_TOKEN_TILE = 512
PEER_EXPERT_TILE = 512
PEER_OUT_TILE_M = 1024
PEER_OUT_TILE_N = 1024
PEER_OUT_TILE_K = 2048
DSA_Q_TILE = 128
DSA_KEY_GROUPS = 4
RWKV_CHUNK = 64
RWKV_HEAD_GROUP = 8
GDN_HEAD_GROUP = 4

HIGHEST = lax.Precision.HIGHEST
NEG_INF = float("-inf")
INT_MIN = -2 ** 31


def _mm_kernel(x_ref, w_ref, o_ref):
    o_ref[...] = jnp.dot(x_ref[...], w_ref[...].astype(jnp.bfloat16), preferred_element_type=jnp.float32)


def _mm_few_rows_f32w(x, w, layer):
    M, K = x.shape
    N = w.shape[2]
    tn = MM_TILE_N_F32_WEIGHT
    assert N % tn == 0 and M <= MM_FEW_ROWS
    Mp = _round_up(M, BF16_SUBLANES)
    xb = jnp.pad(x.astype(jnp.bfloat16), ((0, Mp - M), (0, 0)))
    out = pl.pallas_call(
        _mm_kernel,
        grid=(N // tn,),
        in_specs=[pl.BlockSpec((Mp, K), lambda j: (0, 0)),
                  pl.BlockSpec((None, K, tn), lambda j: (layer, 0, j))],
        out_specs=pl.BlockSpec((Mp, tn), lambda j: (0, j)),
        out_shape=jax.ShapeDtypeStruct((Mp, N), jnp.float32),
        compiler_params=pltpu.CompilerParams(dimension_semantics=("parallel",),
                                             vmem_limit_bytes=VMEM_LIMIT_BYTES),
        name="mm_few_rows",
    )(xb, w)
    return out[:M]


def _round_up(n, m):
    return (n + m - 1) // m * m


def _pad_cols_bf16(w, multiple=None):
    multiple = multiple or MM_TILE_N
    n_pad = _round_up(w.shape[1], multiple)
    wb = w.astype(jnp.bfloat16)
    return wb if n_pad == w.shape[1] else jnp.pad(wb, ((0, 0), (0, n_pad - w.shape[1])))


def _mm(x, w):
    M, K = x.shape
    N = w.shape[1]
    assert N % MM_TILE_N == 0
    tm = min(MM_TILE_M, _round_up(M, BF16_SUBLANES))
    Mp = _round_up(M, tm)
    tn = next(t for t in (MM_TILE_N_FEW_ROWS, MM_TILE_N) if N % t == 0) if Mp <= MM_FEW_ROWS else MM_TILE_N
    xb = x.astype(jnp.bfloat16)
    if Mp != M:
        xb = jnp.pad(xb, ((0, Mp - M), (0, 0)))
    out = pl.pallas_call(
        _mm_kernel,
        grid=(N // tn, Mp // tm),
        in_specs=[pl.BlockSpec((tm, K), lambda j, i: (i, 0)),
                  pl.BlockSpec((K, tn), lambda j, i: (0, j))],
        out_specs=pl.BlockSpec((tm, tn), lambda j, i: (i, j)),
        out_shape=jax.ShapeDtypeStruct((Mp, N), jnp.float32),
        compiler_params=pltpu.CompilerParams(
            dimension_semantics=("parallel", "parallel"),
            vmem_limit_bytes=VMEM_LIMIT_BYTES),
        name="mm",
    )(xb, w.astype(jnp.bfloat16))
    return out if Mp == M else out[:M]


def _mm3(x, w):
    B, T, K = x.shape
    return _mm(x.reshape(B * T, K), w).reshape(B, T, w.shape[1])


def _merge_kernel(*refs):
    n = (len(refs) - 1) // 3
    out_ref = refs[-1]
    acc = None
    for b in range(n):
        o_ref, w_ref, g_ref = refs[b], refs[n + b], refs[2 * n + b]
        term = jax.nn.sigmoid(g_ref[...]) * jnp.dot(o_ref[...], w_ref[...], preferred_element_type=jnp.float32)
        acc = term if acc is None else acc + term
    out_ref[...] = acc.astype(out_ref.dtype)


def _merge_branches(outs, w_branch, p_gate):
    n = len(outs)
    M = outs[0].shape[0]
    D = w_branch.shape[1]
    kb = outs[0].shape[1]
    assert all(o.shape == (M, kb) for o in outs) and w_branch.shape[0] == n * kb and p_gate.shape == (M, n * D)
    tm = min(MERGE_TILE_M, _round_up(M, BF16_SUBLANES))
    Mp = _round_up(M, tm)
    tn = MERGE_TILE_N
    obs = [o.astype(jnp.bfloat16) for o in outs]
    if Mp != M:
        obs = [jnp.pad(o, ((0, Mp - M), (0, 0))) for o in obs]
        p_gate = jnp.pad(p_gate, ((0, Mp - M), (0, 0)))
    o_specs = [pl.BlockSpec((tm, kb), lambda i, j: (i, 0)) for _ in range(n)]
    w_specs = [pl.BlockSpec((kb, tn), functools.partial(lambda i, j, b: (b, j), b=b)) for b in range(n)]
    g_specs = [pl.BlockSpec((tm, tn), functools.partial(lambda i, j, b: (i, b * (D // tn) + j), b=b))
               for b in range(n)]
    out = pl.pallas_call(
        _merge_kernel,
        grid=(Mp // tm, D // tn),
        in_specs=o_specs + w_specs + g_specs,
        out_specs=pl.BlockSpec((tm, tn), lambda i, j: (i, j)),
        out_shape=jax.ShapeDtypeStruct((Mp, D), jnp.bfloat16),
        compiler_params=pltpu.CompilerParams(dimension_semantics=("parallel", "parallel"),
                                             vmem_limit_bytes=VMEM_LIMIT_BYTES),
        name="merge_branches",
    )(*obs, *([w_branch] * n), *([p_gate] * n))
    return out if Mp == M else out[:M]


def _extract_top(x, order, n_out):
    outs, picks = [], []
    rank = jnp.full(x.shape, float(n_out), jnp.float32)
    for r in range(n_out):
        m = jnp.max(x, axis=0, keepdims=True)
        c = jnp.min(jnp.where(x == m, order, jnp.int32(2 ** 30)), axis=0, keepdims=True)
        hit = order == c
        x = jnp.where(hit, NEG_INF, x)
        rank = jnp.where(hit, float(r), rank)
        outs.append(m)
        picks.append(c)
    return outs, picks, rank


def _peer_select_kernel(q_ref, sk_ref, ea_ref, rk_ref, sv_ref):
    Tt = q_ref.shape[0]
    NK, DK2 = sk_ref.shape[2], sk_ref.shape[3]
    key_iota = lax.broadcasted_iota(jnp.int32, (NK, Tt), 0)
    scores, tops, ranks = [], [], []
    for p in range(2):
        s_t = lax.dot_general(sk_ref[0, p], q_ref[:, p * DK2:(p + 1) * DK2], (((1,), (1,)), ((), ())),
                              precision=HIGHEST, preferred_element_type=jnp.float32)
        sv, _, rank = _extract_top(s_t, key_iota, PEER_TOPK)
        for r in range(PEER_TOPK):
            sv_ref[p, r:r + 1, :] = sv[r]
        scores.append(s_t)
        tops.append(sv)
        ranks.append(rank)
    row8 = lax.broadcasted_iota(jnp.int32, (8, Tt), 0)
    row16 = lax.broadcasted_iota(jnp.int32, (PEER_TOPK, Tt), 0)
    strips = [sv_ref[0] + tops[1][0]]
    order = [row16 * PEER_TOPK]
    for b in range(1, 8):
        n_a = PEER_TOPK // (b + 1)
        strips.append(jnp.where(row8 < n_a, sv_ref[0, 0:8, :] + tops[1][b], NEG_INF))
        order.append(row8 * PEER_TOPK + b)
    strips.append(tops[0][0] + sv_ref[1, 8:16, :])
    order.append(row8 + 8)
    cv, picks, _ = _extract_top(jnp.concatenate(strips, axis=0), jnp.concatenate(order, axis=0), PEER_TOPK)
    z = jnp.ones_like(cv[0])
    for r in range(1, PEER_TOPK):
        z = z + jnp.exp(cv[r] - cv[0])
    n_b = jnp.zeros((NK, Tt), jnp.float32)
    for c in picks:
        n_b = n_b + jnp.where(ranks[0] == jnp.right_shift(c, 4).astype(jnp.float32), 1.0, 0.0)
    rk_ref[0] = n_b
    rk_ref[1] = ranks[1]
    ea_ref[0] = jnp.exp(scores[0] - tops[0][0])
    ea_ref[1] = jnp.exp(scores[1] - tops[1][0]) / z


def _peer_select(q, subkeys, tile):
    n = q.shape[0]
    H, _, NK, DK2 = subkeys.shape
    assert PEER_TOPK == 16 and NK % 8 == 0 and n % tile == 0
    tab = jax.ShapeDtypeStruct((2 * H, NK, n), jnp.float32)
    return pl.pallas_call(
        _peer_select_kernel,
        grid=(n // tile, H),
        in_specs=[pl.BlockSpec((tile, 2 * DK2), lambda i, h: (i, h)),
                  pl.BlockSpec((1, 2, NK, DK2), lambda i, h: (h, 0, 0, 0))],
        out_specs=[pl.BlockSpec((2, NK, tile), lambda i, h: (h, 0, i)),
                   pl.BlockSpec((2, NK, tile), lambda i, h: (h, 0, i))],
        out_shape=[tab, tab],
        scratch_shapes=[pltpu.VMEM((2, PEER_TOPK, tile), jnp.float32)],
        compiler_params=pltpu.CompilerParams(dimension_semantics=("parallel", "parallel"),
                                             vmem_limit_bytes=VMEM_LIMIT_BYTES),
        name="peer_select",
    )(q, subkeys)


def _gelu_exact(x):
    return 0.5 * x * (1.0 + lax.erf(x * (2.0 ** -0.5)))


def _peer_gate_kernel(h_ref, u_ref, ea_ref, rk_ref, g_ref):
    j = pl.program_id(1)
    Tt = h_ref.shape[0]
    Et = u_ref.shape[0]
    H2, NK, _ = ea_ref.shape
    n_i0 = Et // NK
    act_t = lax.dot_general(u_ref[...].astype(jnp.bfloat16), h_ref[...], (((1,), (1,)), ((), ())),
                            preferred_element_type=jnp.float32)
    for r in range(n_i0):
        i0 = j * n_i0 + r
        rows = slice(r * NK, (r + 1) * NK)
        a0 = [ea_ref[2 * h, pl.ds(i0, 1), :] for h in range(H2 // 2)]
        n_b = [rk_ref[2 * h, pl.ds(i0, 1), :] for h in range(H2 // 2)]
        for c in range(Tt // LANES):
            cols = slice(c * LANES, (c + 1) * LANES)
            acc = None
            for h in range(H2 // 2):
                keep = rk_ref[2 * h + 1, :, cols] < n_b[h][:, cols]
                term = jnp.where(keep, ea_ref[2 * h + 1, :, cols], 0.0) * a0[h][:, cols]
                acc = term if acc is None else acc + term
            g_ref[rows, cols] = (acc * _gelu_exact(act_t[rows, cols])).astype(jnp.bfloat16)


def _peer_gate(h, u, layer, ea, rk, token_tile, expert_tile):
    n, D = h.shape
    E = u.shape[1]
    H2, NK, _ = ea.shape
    assert E == NK * NK and expert_tile % NK == 0 and E % expert_tile == 0
    assert n % token_tile == 0 and token_tile % LANES == 0
    const = pl.Buffered(1)
    return pl.pallas_call(
        _peer_gate_kernel,
        grid=(n // token_tile, E // expert_tile),
        in_specs=[pl.BlockSpec((token_tile, D), lambda i, j: (i, 0), pipeline_mode=const),
                  pl.BlockSpec((None, expert_tile, D), lambda i, j: (layer, j, 0)),
                  pl.BlockSpec((H2, NK, token_tile), lambda i, j: (0, 0, i), pipeline_mode=const),
                  pl.BlockSpec((H2, NK, token_tile), lambda i, j: (0, 0, i), pipeline_mode=const)],
        out_specs=pl.BlockSpec((expert_tile, token_tile), lambda i, j: (j, i)),
        out_shape=jax.ShapeDtypeStruct((E, n), jnp.bfloat16),
        compiler_params=pltpu.CompilerParams(dimension_semantics=("parallel", "parallel"),
                                             vmem_limit_bytes=VMEM_LIMIT_BYTES),
        name="peer_gate",
    )(h, u, ea, rk)


def _mm_tn_kernel(a_ref, b_ref, o_ref, acc_ref):
    k = pl.program_id(2)
    part = lax.dot_general(a_ref[...], b_ref[...].astype(jnp.bfloat16), (((0,), (0,)), ((), ())),
                           preferred_element_type=jnp.float32)

    @pl.when(k == 0)
    def _():
        acc_ref[...] = part

    @pl.when(k > 0)
    def _():
        acc_ref[...] += part

    @pl.when(k == pl.num_programs(2) - 1)
    def _():
        o_ref[...] = acc_ref[...]


def _mm_tn(a_t, b, layer, tm, tn, tk):
    K, M = a_t.shape
    N = b.shape[2]
    assert M % tm == 0 and N % tn == 0 and K % tk == 0
    return pl.pallas_call(
        _mm_tn_kernel,
        grid=(M // tm, N // tn, K // tk),
        in_specs=[pl.BlockSpec((tk, tm), lambda i, j, k: (k, i)),
                  pl.BlockSpec((None, tk, tn), lambda i, j, k: (layer, k, j))],
        out_specs=pl.BlockSpec((tm, tn), lambda i, j, k: (i, j)),
        out_shape=jax.ShapeDtypeStruct((M, N), jnp.float32),
        scratch_shapes=[pltpu.VMEM((tm, tn), jnp.float32)],
        compiler_params=pltpu.CompilerParams(dimension_semantics=("parallel", "parallel", "arbitrary"),
                                             vmem_limit_bytes=VMEM_LIMIT_BYTES),
        name="mm_tn",
    )(a_t, b)


def peer_ffn(h, wq, subkeys, u_all, v_all, layer):
    B, T, D = h.shape
    n = B * T
    sel_tile = min(PEER_SEL_TILE, _round_up(n, 128))
    tok_tile = min(PEER_TOKEN_TILE, _round_up(n, 128))
    n_pad = _round_up(n, max(sel_tile, tok_tile))
    xt = h.reshape(n, D).astype(jnp.bfloat16)
    if n_pad != n:
        xt = jnp.pad(xt, ((0, n_pad - n), (0, 0)))
    q = _mm(xt, wq)
    ea, rk = _peer_select(q, subkeys, sel_tile)
    g_t = _peer_gate(xt, u_all, layer, ea, rk, tok_tile, PEER_EXPERT_TILE)
    out_tile_m = PEER_OUT_TILE_M if n_pad % PEER_OUT_TILE_M == 0 else tok_tile
    y = _mm_tn(g_t, v_all, layer, out_tile_m, min(D, PEER_OUT_TILE_N), PEER_OUT_TILE_K)
    return y[:n].reshape(B, T, D)


def _count_rows(mask):
    return jnp.sum(jnp.where(mask, 1.0, 0.0), axis=1, keepdims=True)


def _dsa_prompt_kernel(q_ref, k_ref, v_ref, qi_ref, ki_ref, wi_ref, o_ref, *, ktop, first_q_tile):
    Tq = q_ref.shape[1]
    T = k_ref.shape[1]
    n_bits = int(T - 1).bit_length()
    ki = ki_ref[0]
    wi = wi_ref[0]
    score = jnp.zeros((Tq, T), jnp.float32)
    for h in range(IDX_HEADS):
        d = lax.dot_general(qi_ref[0, :, h * IDX_HD:(h + 1) * IDX_HD], ki, (((1,), (1,)), ((), ())),
                            precision=HIGHEST, preferred_element_type=jnp.float32)
        score = score + wi[:, h:h + 1] * jnp.maximum(d, 0.0)
    bits = lax.bitcast_convert_type(score, jnp.int32)
    key = jnp.where(bits < 0, bits ^ jnp.int32(0x7FFFFFFF), bits)
    key = jnp.where(bits == jnp.int32(INT_MIN), 0, key)
    qpos = (first_q_tile + pl.program_id(1)) * Tq + lax.broadcasted_iota(jnp.int32, (Tq, T), 0)
    col = lax.broadcasted_iota(jnp.int32, (Tq, T), 1)
    allowed = col <= qpos
    key = jnp.where(allowed, key, jnp.int32(INT_MIN))
    kf = jnp.float32(ktop)

    tau = jnp.where(_count_rows(key >= 0) >= kf, jnp.int32(0), jnp.int32(INT_MIN))

    def tau_step(it, tau):
        cand = tau | jnp.left_shift(jnp.int32(1), 30 - it)
        return jnp.where(_count_rows(key >= cand) >= kf, cand, tau)

    tau = lax.fori_loop(0, 31, tau_step, tau)
    need = kf - _count_rows(key > tau)
    tie = (key == tau) & allowed

    def tie_step(it, p):
        cand = p | jnp.left_shift(jnp.int32(1), n_bits - 1 - it)
        return jnp.where(_count_rows(tie & (col < cand)) < need, cand, p)

    p = lax.fori_loop(0, n_bits, tie_step, jnp.zeros((Tq, 1), jnp.int32))
    sel = allowed & ((key > tau) | (tie & (col <= p)))

    group = ATT_HEADS // ATT_KV_HEADS
    scale = ATT_HD ** -0.5
    for g in range(ATT_KV_HEADS):
        kg = k_ref[0, :, g * ATT_HD:(g + 1) * ATT_HD].astype(jnp.bfloat16)
        vg = v_ref[0, :, g * ATT_HD:(g + 1) * ATT_HD].astype(jnp.bfloat16)
        for hh in range(group):
            lo = (g * group + hh) * ATT_HD
            qh = q_ref[0, :, lo:lo + ATT_HD].astype(jnp.bfloat16)
            logits = lax.dot_general(qh, kg, (((1,), (1,)), ((), ())), preferred_element_type=jnp.float32) * scale
            logits = jnp.where(sel, logits, NEG_INF)
            e = jnp.exp(logits - jnp.max(logits, axis=1, keepdims=True))
            out = jnp.dot(e.astype(jnp.bfloat16), vg, preferred_element_type=jnp.float32)
            o_ref[0, :, lo:lo + ATT_HD] = out / jnp.sum(e, axis=1, keepdims=True)


def dsa_prompt(q, k, v, qi, ki, wi):
    B, T = q.shape[:2]
    ktop = min(TOPK_MAX, T // 4)
    tq = min(DSA_Q_TILE, T)
    flat = lambda a: a.reshape(B, T, -1)
    args = (flat(q), flat(k), flat(v), flat(qi), ki, wi)
    n_groups = DSA_KEY_GROUPS if (T // tq) % DSA_KEY_GROUPS == 0 else 1
    tiles = T // tq // n_groups
    outs = []
    for grp in range(n_groups):
        first = grp * tiles
        t_keys = (first + tiles) * tq
        q_map = functools.partial(lambda b, i, first: (b, first + i, 0), first=first)
        kv_map = lambda b, i: (b, 0, 0)
        outs.append(pl.pallas_call(
            functools.partial(_dsa_prompt_kernel, ktop=ktop, first_q_tile=first),
            grid=(B, tiles),
            in_specs=[pl.BlockSpec((1, tq, ATT_Q), q_map),
                      pl.BlockSpec((1, t_keys, ATT_KV), kv_map),
                      pl.BlockSpec((1, t_keys, ATT_KV), kv_map),
                      pl.BlockSpec((1, tq, IDX_HEADS * IDX_HD), q_map),
                      pl.BlockSpec((1, t_keys, IDX_HD), kv_map),
                      pl.BlockSpec((1, tq, IDX_HEADS), q_map)],
            out_specs=pl.BlockSpec((1, tq, ATT_Q), lambda b, i: (b, i, 0)),
            out_shape=jax.ShapeDtypeStruct((B, tiles * tq, ATT_Q), jnp.float32),
            compiler_params=pltpu.CompilerParams(dimension_semantics=("parallel", "parallel"),
                                                 vmem_limit_bytes=VMEM_LIMIT_BYTES),
            name="dsa_prompt",
        )(*args))
    return jnp.concatenate(outs, axis=1).reshape(q.shape)


_NN = (((1,), (0,)), ((), ()))
_NT = (((1,), (1,)), ((), ()))
_TN = (((0,), (0,)), ((), ()))


def _dot_f32(x, y, dims=_NN):
    return lax.dot_general(x, y, dims, precision=HIGHEST, preferred_element_type=jnp.float32)


def _split_bf16(x):
    hi = x.astype(jnp.bfloat16)
    return hi, (x - hi.astype(jnp.float32)).astype(jnp.bfloat16)


def _dot3(x, y, dims=_NN):
    xh, xl = x if isinstance(x, tuple) else _split_bf16(x)
    yh, yl = y if isinstance(y, tuple) else _split_bf16(y)
    d = lambda p, q: lax.dot_general(p, q, dims, preferred_element_type=jnp.float32)
    return d(xh, yh) + (d(xh, yl) + d(xl, yh))


def _dot1(x, y, dims=_NN):
    return lax.dot_general(x.astype(jnp.bfloat16), y.astype(jnp.bfloat16), dims, preferred_element_type=jnp.float32)


def _unit_lower_solve(l_mats, rhs, size):
    xs = list(rhs)
    for step in range(max(1, int(size - 1).bit_length())):
        if step:
            l_mats = [_dot3(sp, sp) for sp in splits]
        splits = [_split_bf16(l) for l in l_mats]
        xs = [x + _dot3(sp, x) for sp, x in zip(splits, xs)]
    return xs


def _rwkv_kernel(r_ref, lw_ref, k_ref, v_ref, a_ref, b_ref, s0_ref, y_ref, s_ref):
    C = r_ref.shape[1]
    HG, N = s_ref.shape[1], s_ref.shape[2]
    row = lax.broadcasted_iota(jnp.int32, (C, C), 0)
    col = lax.broadcasted_iota(jnp.int32, (C, C), 1)
    incl = row >= col
    strict = row > col
    tri = jnp.where(incl, 1.0, 0.0)
    incl2 = (lax.broadcasted_iota(jnp.int32, (C, 2 * C), 0)
             >= (lax.broadcasted_iota(jnp.int32, (C, 2 * C), 1) & (C - 1)))

    @pl.when(pl.program_id(2) == 0)
    def _():
        s_ref[...] = s0_ref[...]

    heads = range(HG)
    head = lambda x, h: x[:, h * N:(h + 1) * N]
    lw = lw_ref[0]
    log_p = _dot_f32(tri, lw)
    inv_p = jnp.exp(-log_p)
    to_end = jnp.exp(log_p[C - 1:C, :] - log_p)
    p_end = jnp.exp(log_p[C - 1:C, :])
    a_t, r_t = a_ref[0] * jnp.exp(log_p - lw), r_ref[0] * jnp.exp(log_p)
    b_t, k_t = b_ref[0] * inv_p, k_ref[0] * inv_p
    b_end, k_end = b_ref[0] * to_end, k_ref[0] * to_end
    v_all = v_ref[0]
    v = [head(v_all, h) for h in heads]
    s0 = [s_ref[0, h] for h in heads]
    ar = [_split_bf16(jnp.concatenate([head(a_t, h), head(r_t, h)], axis=0)) for h in heads]
    g = [_dot3(ar[h], jnp.concatenate([head(b_t, h), head(k_t, h)], axis=0), _NT) for h in heads]
    ars = [_dot3(ar[h], s0[h], _NT) for h in heads]
    rhs = [ars[h][:C] + _dot3(jnp.where(strict, g[h][:C, C:], 0.0), v[h]) for h in heads]
    u = _unit_lower_solve([jnp.where(strict, g[h][:C, :C], 0.0) for h in heads], rhs, C)
    uv = [jnp.concatenate([u[h], v[h]], axis=0) for h in heads]
    for h in heads:
        y_ref[0, :, h * N:(h + 1) * N] = ars[h][C:] + _dot1(jnp.where(incl2, g[h][C:], 0.0), uv[h])
    for h in heads:
        bk_end = jnp.concatenate([head(b_end, h), head(k_end, h)], axis=0)
        s_ref[0, h] = s0[h] * head(p_end, h) + _dot3(uv[h], bk_end, _TN)


def rwkv7_scan(r, w_log, k, v, a, b, s0):
    B, T, H, N = r.shape
    C = min(RWKV_CHUNK, _round_up(T, 8))
    assert C & (C - 1) == 0
    Tp = _round_up(T, C)

    def prep(x):
        x = x.astype(jnp.float32).reshape(B, T, H * N)
        return jnp.pad(x, ((0, 0), (0, Tp - T), (0, 0))) if Tp != T else x

    hg = RWKV_HEAD_GROUP
    seq = pl.BlockSpec((1, C, hg * N), lambda bi, gi, ci: (bi, ci, gi))
    st = pl.BlockSpec((1, hg, N, N), lambda bi, gi, ci: (bi, gi, 0, 0))
    y, s_new = pl.pallas_call(
        _rwkv_kernel,
        grid=(B, H // hg, Tp // C),
        in_specs=[seq] * 6 + [st],
        out_specs=[seq, st],
        out_shape=[jax.ShapeDtypeStruct((B, Tp, H * N), jnp.float32),
                   jax.ShapeDtypeStruct((B, H, N, N), jnp.float32)],
        compiler_params=pltpu.CompilerParams(dimension_semantics=("parallel", "parallel", "arbitrary"),
                                             vmem_limit_bytes=VMEM_LIMIT_BYTES),
        name="rwkv7_scan",
    )(prep(r), prep(w_log), prep(k), prep(v), prep(a), prep(b), s0.astype(jnp.float32))
    return y[:, :T].reshape(B, T, H, N), s_new


def _gdn_kernel(q_ref, k_ref, v_ref, b_ref, g_ref, gr_ref, s0_ref, o_ref, s_ref):
    C = q_ref.shape[1]
    HG = s_ref.shape[1]
    DK, DV = s_ref.shape[2], s_ref.shape[3]
    row = lax.broadcasted_iota(jnp.int32, (C, C), 0)
    col = lax.broadcasted_iota(jnp.int32, (C, C), 1)
    incl = row >= col
    strict = row > col
    tri = jnp.where(incl, 1.0, 0.0)
    tri_t = jnp.where(row <= col, 1.0, 0.0)

    @pl.when(pl.program_id(2) == 0)
    def _():
        s_ref[...] = s0_ref[...]

    def unit(x):
        return x * lax.rsqrt(jnp.sum(x * x, axis=-1, keepdims=True) + EPS)

    heads = range(HG)
    lanes = lambda ref, h: jnp.broadcast_to(ref[0, 0, :, h:h + 1], (C, LANES))
    beta = [lanes(b_ref, h) for h in heads]
    k = [unit(k_ref[0, :, h * DK:(h + 1) * DK]) for h in heads]
    s0 = [s_ref[0, h] for h in heads]
    gc = [_dot_f32(tri, lanes(g_ref, h)) for h in heads]
    gc_row = [_dot_f32(gr_ref[0, h, 0], tri_t)[0:1, :] for h in heads]
    decay = [jnp.exp(jnp.where(incl, gc[h][:, :C] - gc_row[h], NEG_INF)) for h in heads]
    kb = [k[h] * beta[h] for h in heads]
    a_low = [jnp.where(strict, _dot3(kb[h], k[h], _NT) * decay[h], 0.0) for h in heads]
    rhs = [jnp.concatenate([v_ref[0, :, h * DV:(h + 1) * DV] * beta[h], kb[h] * jnp.exp(gc[h])], axis=1)
           for h in heads]
    sol = _unit_lower_solve([-a for a in a_low], rhs, C)
    v_new = [sol[h][:, :DV] - _dot3(sol[h][:, DV:], s0[h]) for h in heads]
    for h in heads:
        q = unit(q_ref[0, :, h * DK:(h + 1) * DK]) * (DK ** -0.5)
        qk = _dot1(q, k[h], _NT) * decay[h]
        o_ref[0, :, h * DV:(h + 1) * DV] = _dot1(q * jnp.exp(gc[h]), s0[h]) + _dot1(qk, v_new[h])
    for h in heads:
        g_last = gc[h][C - 1:C, :]
        s_ref[0, h] = s0[h] * jnp.exp(g_last) + _dot3(k[h] * jnp.exp(g_last - gc[h]), v_new[h], _TN)


def gated_delta_rule(qkv, beta, g, s0):
    B, T, _ = qkv.shape
    _, H, DK, DV = s0.shape
    assert DK == LANES and DV == LANES and qkv.shape[2] == (2 * DK + DV) * H
    C = min(GDN_CHUNK, _round_up(T, 8))
    Tp = _round_up(T, C)
    nc = Tp // C
    hg = GDN_HEAD_GROUP
    ng = H // hg
    pad_t = lambda x: jnp.pad(x, ((0, 0), (0, Tp - T), (0, 0))) if Tp != T else x
    qkv = pad_t(qkv.astype(jnp.float32))
    per_group = lambda x: jnp.transpose(pad_t(x.astype(jnp.float32)).reshape(B, Tp, ng, hg), (0, 2, 1, 3))
    g_rows = jnp.broadcast_to(jnp.moveaxis(pad_t(g.astype(jnp.float32)), 2, 1).reshape(B, H, nc, 1, C),
                              (B, H, nc, 8, C))
    cols = lambda first, d: pl.BlockSpec((1, C, hg * d), functools.partial(
        lambda bi, gi, ci, first: (bi, ci, first + gi), first=first))
    tok = pl.BlockSpec((1, 1, C, hg), lambda bi, gi, ci: (bi, gi, ci, 0))
    st = pl.BlockSpec((1, hg, DK, DV), lambda bi, gi, ci: (bi, gi, 0, 0))
    o, s_new = pl.pallas_call(
        _gdn_kernel,
        grid=(B, ng, nc),
        in_specs=[cols(0, DK), cols(ng, DK), cols(2 * ng, DV), tok, tok,
                  pl.BlockSpec((1, hg, 1, 8, C), lambda bi, gi, ci: (bi, gi, ci, 0, 0)), st],
        out_specs=[cols(0, DV), st],
        out_shape=[jax.ShapeDtypeStruct((B, Tp, H * DV), jnp.float32),
                   jax.ShapeDtypeStruct((B, H, DK, DV), jnp.float32)],
        compiler_params=pltpu.CompilerParams(dimension_semantics=("parallel", "parallel", "arbitrary"),
                                             vmem_limit_bytes=VMEM_LIMIT_BYTES),
        name="gated_delta_rule",
    )(qkv, qkv, qkv, per_group(beta), per_group(g), g_rows, s0.astype(jnp.float32))
    return o[:, :T], s_new


def _split(a, sizes, axis=-1):
    ends = [int(e) for e in np.cumsum(sizes)]
    return [lax.slice_in_dim(a, e - int(s), e, axis=axis % a.ndim) for s, e in zip(sizes, ends)]


def rmsnorm(x, gain):
    xf = x.astype(jnp.float32)
    y = xf * lax.rsqrt(jnp.mean(xf * xf, axis=-1, keepdims=True) + EPS)
    return (y * gain.astype(jnp.float32)).astype(x.dtype)


def l2norm(x):
    xf = x.astype(jnp.float32)
    return xf * lax.rsqrt(jnp.sum(xf * xf, axis=-1, keepdims=True) + EPS)


def partial_rope(x, pos):
    rd = x.shape[-1] // ROPE_DIV
    half = rd // 2
    inv_freq = ROPE_THETA ** (-jnp.arange(half, dtype=jnp.float32) * 2.0 / rd)
    ang = pos.astype(jnp.float32)[:, None] * inv_freq[None, :]
    cos = jnp.cos(ang)[:, None, :]
    sin = jnp.sin(ang)[:, None, :]
    xf = x[..., :rd].astype(jnp.float32)
    x1, x2 = xf[..., :half], xf[..., half:]
    rot = jnp.concatenate([x1 * cos - x2 * sin, x2 * cos + x1 * sin], axis=-1)
    return jnp.concatenate([rot.astype(x.dtype), x[..., rd:]], axis=-1)


def causal_conv(x, buf, w):
    T = x.shape[1]
    xp = jnp.concatenate([buf.astype(x.dtype), x], axis=1)
    y = xp[:, 0:T] * w[0]
    for i in range(1, CONV_W):
        y = y + xp[:, i:i + T] * w[i]
    return jax.nn.silu(y), xp[:, T:]


def _to_chunks(a, C):
    B, T = a.shape[:2]
    pad = (-T) % C
    a = jnp.pad(a.astype(jnp.float32), [(0, 0), (0, pad)] + [(0, 0)] * (a.ndim - 2))
    n = (T + pad) // C
    a = a.reshape((B, n, C) + a.shape[2:])
    return jnp.transpose(a, (1, 0, 3, 2) + tuple(range(4, a.ndim)))


def _from_chunks(o, T):
    n, B, H, C, X = o.shape
    return jnp.transpose(o, (1, 0, 3, 2, 4)).reshape(B, n * C, H, X)[:, :T]


def gla_chunked(q, k, v, log_a, s0):
    T = q.shape[1]
    C = min(GLA_CHUNK, T)
    qc, kc, vc = _to_chunks(q, C), _to_chunks(k, C), _to_chunks(v, C)
    bc = jnp.cumsum(_to_chunks(log_a, C), axis=-2)
    tri = jnp.tril(jnp.ones((C, C), bool))[..., None]

    def step(S, xs):
        q_n, k_n, v_n, b_n = xs
        rel = jnp.exp(jnp.where(tri, b_n[..., :, None, :] - b_n[..., None, :, :], -jnp.inf))
        att = jnp.einsum('bhik,bhijk,bhjk->bhij', q_n, rel, k_n)
        o = jnp.einsum('bhik,bhkv->bhiv', q_n * jnp.exp(b_n), S) + jnp.einsum('bhij,bhjv->bhiv', att, v_n)
        b_last = b_n[..., -1:, :]
        S = S * jnp.exp(b_last)[..., 0, :, None] + jnp.einsum('bhck,bhcv->bhkv', k_n * jnp.exp(b_last - b_n), v_n)
        return S, o

    S, o = lax.scan(step, s0.astype(jnp.float32), (qc, kc, vc, bc))
    return _from_chunks(o, T), S


def head_group_norm(y, w, b):
    B, T, H, N = y.shape
    mu = jnp.mean(y, axis=-1, keepdims=True)
    var = jnp.mean(jnp.square(y - mu), axis=-1, keepdims=True)
    yn = ((y - mu) * lax.rsqrt(var + RWKV_GN_EPS)).reshape(B, T, H * N)
    return yn * w.astype(jnp.float32) + b.astype(jnp.float32)


def indexer_scores(qi, wi, ki):
    dots = jnp.einsum('bqhd,bsd->bqhs', qi, ki).astype(jnp.float32)
    return jnp.einsum('bqh,bqhs->bqs', wi.astype(jnp.float32), jax.nn.relu(dots))


def select_keys(scores, qpos, ktop):
    S = scores.shape[-1]
    allowed = jnp.arange(S)[None, None, :] <= qpos[None, :, None]
    _, idx = lax.top_k(jnp.where(allowed, scores, -jnp.inf), ktop)
    return idx, idx <= qpos[None, :, None]


def sparse_attend(q, k_sel, v_sel, valid):
    B, Q, HQ, HD = q.shape
    qg = q.reshape(B, Q, ATT_KV_HEADS, HQ // ATT_KV_HEADS, HD)
    logits = jnp.einsum('bqhgd,bqkhd->bqhgk', qg, k_sel).astype(jnp.float32) * (HD ** -0.5)
    logits = jnp.where(valid[:, :, None, None, :], logits, -jnp.inf)
    p = jax.nn.softmax(logits, axis=-1).astype(v_sel.dtype)
    return jnp.einsum('bqhgk,bqkhd->bqhgd', p, v_sel).reshape(B, Q, HQ, HD)


def dsa_sample(q, k, v, qi, ki, wi, cache_k, cache_v, cache_ki, page_table):
    DB, DS = q.shape[:2]
    past = page_table.shape[1] * PAGE_SIZE
    ktop = min(TOPK_MAX, (past + DS) // 4)
    ki_past = cache_ki[page_table].reshape(DB, past, IDX_HD).astype(ki.dtype)
    ki_all = jnp.concatenate([ki_past, ki], axis=1)
    qpos = past + jnp.arange(DS)
    idx, valid = select_keys(indexer_scores(qi, wi, ki_all), qpos, ktop)
    is_new = idx >= past
    pidx = jnp.minimum(idx, past - 1)
    phys = jax.vmap(lambda pt, i: pt[i])(page_table, pidx // PAGE_SIZE)
    off = pidx % PAGE_SIZE
    nidx = jnp.clip(idx - past, 0, DS - 1)
    gather = jax.vmap(lambda rows, i: rows[i])
    sel = lambda cache, new: jnp.where(is_new[..., None, None], gather(new, nidx), cache[phys, off].astype(new.dtype))
    return sparse_attend(q, sel(cache_k, k), sel(cache_v, v), valid)


def gdn_branch(p, conv_buf, s0, lp):
    B, T, _ = p.shape
    conv_in, z, b_raw, a_raw = _split(p, (GDN_CONV_COLS, GDN_V, GDN_HEADS, GDN_HEADS))
    conv_out, new_buf = causal_conv(conv_in, conv_buf, lp['gdn_conv_w'])
    beta = jax.nn.sigmoid(b_raw.astype(jnp.float32))
    g = -jnp.exp(lp['gdn_A_log'].astype(jnp.float32)) * jax.nn.softplus(a_raw.astype(jnp.float32) + lp['gdn_dt_bias'].astype(jnp.float32))
    o, s_new = gated_delta_rule(conv_out, beta, g, s0)
    o = o.reshape(B, T, GDN_HEADS, GDN_DV)
    o = rmsnorm(o, lp['gdn_norm']) * jax.nn.silu(z.reshape(B, T, GDN_HEADS, GDN_DV).astype(jnp.float32))
    return o.reshape(B, T, GDN_V), new_buf, s_new


def gla_branch(p, s0, lp):
    B, T, _ = p.shape
    q, k, v, r, gd = _split(p, (GLA_QK, GLA_QK, GLA_V, GLA_V, GLA_RANK))
    hs = (B, T, GLA_HEADS)
    log_a = jax.nn.log_sigmoid((gd @ lp['gla_gate_up'] + lp['gla_gate_bias']).astype(jnp.float32)) / GLA_TAU
    o, s_new = gla_chunked(q.reshape(hs + (GLA_DK,)) * (GLA_DK ** -0.5), k.reshape(hs + (GLA_DK,)),
                           v.reshape(hs + (GLA_DV,)), log_a.reshape(hs + (GLA_DK,)), s0)
    o = rmsnorm(o, lp['gla_norm']) * jax.nn.silu(r.reshape(hs + (GLA_DV,)).astype(jnp.float32))
    return o.reshape(B, T, GLA_V), s_new


def dsa_branch(p, pos, attend):
    B, T, _ = p.shape
    q, k, v, qi, ki, wi = _split(p, (ATT_Q, ATT_KV, ATT_KV, IDX_HEADS * IDX_HD, IDX_HD, IDX_HEADS))
    q = partial_rope(q.reshape(B, T, ATT_HEADS, ATT_HD), pos)
    k = partial_rope(k.reshape(B, T, ATT_KV_HEADS, ATT_HD), pos)
    v = v.reshape(B, T, ATT_KV_HEADS, ATT_HD)
    qi = partial_rope(qi.reshape(B, T, IDX_HEADS, IDX_HD), pos)
    ki = partial_rope(ki.reshape(B, T, 1, IDX_HD), pos)[:, :, 0]
    o = attend(q, k, v, qi, ki, wi)
    return o.reshape(B, T, ATT_Q), (k, v, ki)


def rwkv_branch(p, shift_buf, s0, lp):
    B, T, _ = p.shape
    prev = jnp.concatenate([shift_buf.astype(p.dtype), p[:, :-1]], axis=1)
    pm = p + (prev - p) * lp['rwkv_mu']
    r, wd, k, v, ad, gd = _split(pm, (RWKV_W, RWKV_W_RANK, RWKV_W, RWKV_W, RWKV_A_RANK, RWKV_G_RANK))
    hs = (B, T, RWKV_HEADS, RWKV_HD)
    w_log = -jax.nn.softplus(-(lp['rwkv_w0'] + jnp.tanh(wd) @ lp['rwkv_w2']).astype(jnp.float32)) - 0.5
    log_decay = -jnp.exp(w_log)
    a = jax.nn.sigmoid((lp['rwkv_a0'] + ad @ lp['rwkv_a2']).astype(jnp.float32))
    g = (jax.nn.sigmoid(gd) @ lp['rwkv_g2']).astype(jnp.float32)
    kf = k.astype(jnp.float32)
    kk = l2norm((kf * lp['rwkv_kk']).reshape(hs))
    kf = (kf * (1.0 + (a - 1.0) * lp['rwkv_ka'])).reshape(hs)
    rf = r.astype(jnp.float32).reshape(hs)
    vf = v.astype(jnp.float32).reshape(hs)
    y, s_new = rwkv7_scan(rf, log_decay.reshape(hs), kf, vf, -kk, kk * a.reshape(hs), s0)
    y = head_group_norm(y, lp['rwkv_ln_w'], lp['rwkv_ln_b'])
    bonus = (jnp.sum(rf * kf * lp['rwkv_rk'], axis=-1, keepdims=True) * vf).reshape(B, T, RWKV_W)
    return (y + bonus) * g, p[:, -1:], s_new


def token_mixers(h, pos, st, lp, attend):
    gdn_s, gdn_buf, gla_s, rwkv_s, rwkv_buf = st
    B, T, _ = h.shape
    hb = h.astype(jnp.bfloat16)
    p_gdn, p_gla, p_att, p_rwkv, p_gate = [_mm3(hb, w) for w in lp['w_in_groups']]
    p_rwkv = p_rwkv[..., :RWKV_COLS]
    o_gdn, gdn_buf, gdn_s = gdn_branch(p_gdn, gdn_buf, gdn_s, lp)
    o_gla, gla_s = gla_branch(p_gla, gla_s, lp)
    o_att, rows = dsa_branch(p_att, pos, attend)
    o_rwkv, rwkv_buf, rwkv_s = rwkv_branch(p_rwkv, rwkv_buf, rwkv_s, lp)
    assert len(set(BRANCH_SPLITS)) == 1
    merged = _merge_branches([o.reshape(B * T, -1) for o in (o_gdn, o_gla, o_att, o_rwkv)],
                             lp['w_branch'], p_gate.reshape(B * T, GATE_COLS))
    out = _mm(merged, lp['w_out']).reshape(B, T, D_MODEL)
    return out, rows + (gdn_s, gdn_buf, gla_s, rwkv_s, rwkv_buf)


def trunk_layer(x, mod, pos, st, lp, attend):
    B = x.shape[0]
    mod = mod.reshape(B, N_MOD, 1, D_MODEL)
    shift1, scale1, gate1, shift2, scale2, gate2 = [mod[:, i] for i in range(N_MOD)]
    h = rmsnorm(x, lp['norm1']) * (1.0 + scale1) + shift1
    mix, new_st = token_mixers(h, pos, st, lp, attend)
    x = x + gate1 * mix
    h = rmsnorm(x, lp['norm2']) * (1.0 + scale2) + shift2
    x = x + gate2 * peer_ffn(h, lp['peer_wq'], lp['peer_subkeys'], lp['peer_u_all'], lp['peer_v_all'], lp['layer'])
    return x, new_st


def kernel(x_prompt, x_sample, cache_k, cache_v, cache_kidx, state_gdn, state_gdn_conv, state_gla, state_rwkv, state_rwkv_shift, page_table, c_prompt, c_sample, w_ada, b_ada, norm1, norm2, w_in, gdn_conv_w, gdn_A_log, gdn_dt_bias, gdn_norm, gla_gate_up, gla_gate_bias, gla_norm, rwkv_mu, rwkv_w0, rwkv_w2, rwkv_a0, rwkv_a2, rwkv_g2, rwkv_kk, rwkv_ka, rwkv_rk, rwkv_ln_w, rwkv_ln_b, w_branch, w_out, peer_wq, peer_subkeys, peer_u, peer_v, final_norm):
    Bp, T, _ = x_prompt.shape
    DS = x_sample.shape[1]
    depth = w_in.shape[0]
    past = page_table.shape[1] * PAGE_SIZE
    pos_p = jnp.arange(T, dtype=jnp.int32)
    pos_s = past + jnp.arange(DS, dtype=jnp.int32)
    st_p0 = (jnp.zeros((Bp, GDN_HEADS, GDN_DK, GDN_DV), jnp.float32),
             jnp.zeros((Bp, CONV_W - 1, GDN_CONV_COLS), x_prompt.dtype),
             jnp.zeros((Bp, GLA_HEADS, GLA_DK, GLA_DV), jnp.float32),
             jnp.zeros((Bp, RWKV_HEADS, RWKV_HD, RWKV_HD), jnp.float32),
             jnp.zeros((Bp, 1, RWKV_COLS), x_prompt.dtype))
    weights = {'w_ada': w_ada, 'b_ada': b_ada, 'norm1': norm1, 'norm2': norm2, 'w_in': w_in,
               'gdn_conv_w': gdn_conv_w, 'gdn_A_log': gdn_A_log, 'gdn_dt_bias': gdn_dt_bias, 'gdn_norm': gdn_norm,
               'gla_gate_up': gla_gate_up, 'gla_gate_bias': gla_gate_bias, 'gla_norm': gla_norm,
               'rwkv_mu': rwkv_mu, 'rwkv_w0': rwkv_w0, 'rwkv_w2': rwkv_w2, 'rwkv_a0': rwkv_a0, 'rwkv_a2': rwkv_a2,
               'rwkv_g2': rwkv_g2, 'rwkv_kk': rwkv_kk, 'rwkv_ka': rwkv_ka, 'rwkv_rk': rwkv_rk,
               'rwkv_ln_w': rwkv_ln_w, 'rwkv_ln_b': rwkv_ln_b, 'w_branch': w_branch, 'w_out': w_out,
               'peer_wq': peer_wq, 'peer_subkeys': peer_subkeys, 'peer_u': peer_u, 'peer_v': peer_v}
    xp, xs = x_prompt, x_sample
    out_p, out_s = [], []
    for l in range(depth):
        lp = {name: arr[l] for name, arr in weights.items()}
        lp.update(layer=l, peer_u_all=peer_u, peer_v_all=peer_v)
        for name in ('w_branch', 'w_out', 'peer_wq'):
            lp[name] = lp[name].astype(jnp.bfloat16)
        lp['w_in_groups'] = [_pad_cols_bf16(w) for w in _split(lp['w_in'], IN_SPLITS)]
        mod = _mm_few_rows_f32w(jax.nn.silu(jnp.concatenate([c_prompt, c_sample], axis=0)), w_ada, l) + lp['b_ada']
        xp, new_p = trunk_layer(xp, mod[:Bp], pos_p, st_p0, lp, dsa_prompt)
        attend_s = functools.partial(dsa_sample, cache_k=cache_k[l], cache_v=cache_v[l],
                                     cache_ki=cache_kidx[l], page_table=page_table)
        st_s0 = (state_gdn[l], state_gdn_conv[l], state_gla[l], state_rwkv[l], state_rwkv_shift[l])
        xs, new_s = trunk_layer(xs, mod[Bp:], pos_s, st_s0, lp, attend_s)
        out_p.append(new_p)
        out_s.append(new_s)
    y_prompt = rmsnorm(xp, final_norm)
    y_sample = rmsnorm(xs, final_norm)
    k_p, v_p, kidx_p, gdn_p, gdn_conv_p, gla_p, rwkv_p, rwkv_shift_p = [jnp.stack(z) for z in zip(*out_p)]
    k_s, v_s, kidx_s, gdn_s, gdn_conv_s, gla_s, rwkv_s, rwkv_shift_s = [jnp.stack(z) for z in zip(*out_s)]
    return (y_prompt, y_sample, k_p, v_p, kidx_p, gdn_p, gdn_conv_p, gla_p, rwkv_p, rwkv_shift_p,
            k_s, v_s, kidx_s, gdn_s, gdn_conv_s, gla_s, rwkv_s, rwkv_shift_s)
```

```python
import math, functools
import jax, jax.numpy as jnp
from jax import lax
import numpy as np
from jax.experimental import pallas as pl
from jax.experimental.pallas import tpu as pltpu

D_MODEL = 4096
PAGE_SIZE = 128
N_MOD = 6
EPS = 1e-6
N_BRANCH = 4

GDN_HEADS = 8
GDN_DK = 128
GDN_DV = 128
CONV_W = 4
GDN_CHUNK = 64
GDN_QK = GDN_HEADS * GDN_DK
GDN_V = GDN_HEADS * GDN_DV
GDN_CONV_COLS = 2 * GDN_QK + GDN_V
GDN_COLS = GDN_CONV_COLS + GDN_V + 2 * GDN_HEADS

GLA_HEADS = 4
GLA_DK = 128
GLA_DV = 256
GLA_RANK = 16
GLA_TAU = 16.0
GLA_CHUNK = 64
GLA_QK = GLA_HEADS * GLA_DK
GLA_V = GLA_HEADS * GLA_DV
GLA_COLS = 2 * GLA_QK + 2 * GLA_V + GLA_RANK

ATT_HEADS = 8
ATT_KV_HEADS = 2
ATT_HD = 128
IDX_HEADS = 8
IDX_HD = 64
TOPK_MAX = 256
Q_BLOCK = 128
ROPE_THETA = 500000.0
ROPE_DIV = 4
ATT_Q = ATT_HEADS * ATT_HD
ATT_KV = ATT_KV_HEADS * ATT_HD
ATT_COLS = ATT_Q + 2 * ATT_KV + IDX_HEADS * IDX_HD + IDX_HD + IDX_HEADS

RWKV_HEADS = 16
RWKV_HD = 64
RWKV_W_RANK = 64
RWKV_A_RANK = 64
RWKV_G_RANK = 128
RWKV_GN_EPS = 64e-5
RWKV_W = RWKV_HEADS * RWKV_HD
RWKV_COLS = 3 * RWKV_W + RWKV_W_RANK + RWKV_A_RANK + RWKV_G_RANK

GATE_COLS = N_BRANCH * D_MODEL
IN_SPLITS = (GDN_COLS, GLA_COLS, ATT_COLS, RWKV_COLS, GATE_COLS)
IN_COLS = sum(IN_SPLITS)
BRANCH_SPLITS = (GDN_V, GLA_V, ATT_Q, RWKV_W)
MIX_WIDTH = sum(BRANCH_SPLITS)

PEER_HEADS = 8
PEER_NKEYS = 128
PEER_DKEY = 256
PEER_TOPK = 16
PEER_N = PEER_NKEYS * PEER_NKEYS
PEER_BLOCK = 64

VMEM_LIMIT_BYTES = 56 * 1024 * 1024
MM_TILE_M = 1024
MM_TILE_N = 512
MM_TILE_N_FEW_ROWS = 2048
MM_TILE_K_F32_WEIGHT = 128
MM_FEW_ROWS = 64
MERGE_TILE_M = 512
MERGE_TILE_N = 512
BF16_SUBLANES = 16
LANES = 128
PEER_SEL_TILE = 256
PEER_TOKEN_TILE = 512
PEER_EXPERT_TILE = 512
PEER_OUT_TILE_M = 1024
PEER_OUT_TILE_N = 1024
PEER_OUT_TILE_K = 2048
DSA_Q_TILE = 128
DSA_KEY_GROUPS = 4
RWKV_CHUNK = 64
RWKV_HEAD_GROUP = 8
GDN_HEAD_GROUP = 4

HIGHEST = lax.Precision.HIGHEST
NEG_INF = float("-inf")
INT_MIN = -2 ** 31


def _mm_kernel(x_ref, w_ref, o_ref):
    o_ref[...] = jnp.dot(x_ref[...], w_ref[...], preferred_element_type=jnp.float32)


def _mm_rows_kernel(x_ref, w_ref, o_ref):
    part = jnp.dot(x_ref[...], w_ref[...].astype(jnp.bfloat16), preferred_element_type=jnp.float32)

    @pl.when(pl.program_id(0) == 0)
    def _():
        o_ref[...] = part

    @pl.when(pl.program_id(0) > 0)
    def _():
        o_ref[...] += part


def _mm_few_rows_f32w(x, w, layer):
    M, K = x.shape
    N = w.shape[2]
    tk = MM_TILE_K_F32_WEIGHT
    assert K % tk == 0 and N % LANES == 0 and M <= MM_FEW_ROWS
    Mp = _round_up(M, BF16_SUBLANES)
    xb = jnp.pad(x.astype(jnp.bfloat16), ((0, Mp - M), (0, 0)))
    out = pl.pallas_call(
        _mm_rows_kernel,
        grid=(K // tk,),
        in_specs=[pl.BlockSpec((Mp, tk), lambda k: (0, k)),
                  pl.BlockSpec((None, tk, N), lambda k: (layer, k, 0))],
        out_specs=pl.BlockSpec((Mp, N), lambda k: (0, 0)),
        out_shape=jax.ShapeDtypeStruct((Mp, N), jnp.float32),
        compiler_params=pltpu.CompilerParams(dimension_semantics=("arbitrary",),
                                             vmem_limit_bytes=VMEM_LIMIT_BYTES),
        name="mm_few_rows",
    )(xb, w)
    return out[:M]


def _round_up(n, m):
    return (n + m - 1) // m * m


def _pad_cols_bf16(w, multiple=None):
    multiple = multiple or MM_TILE_N
    n_pad = _round_up(w.shape[1], multiple)
    wb = w.astype(jnp.bfloat16)
    return wb if n_pad == w.shape[1] else jnp.pad(wb, ((0, 0), (0, n_pad - w.shape[1])))


def _mm(x, w):
    M, K = x.shape
    N = w.shape[1]
    assert N % MM_TILE_N == 0
    tm = min(MM_TILE_M, _round_up(M, BF16_SUBLANES))
    Mp = _round_up(M, tm)
    tn = next(t for t in (MM_TILE_N_FEW_ROWS, MM_TILE_N) if N % t == 0) if Mp <= MM_FEW_ROWS else MM_TILE_N
    xb = x.astype(jnp.bfloat16)
    if Mp != M:
        xb = jnp.pad(xb, ((0, Mp - M), (0, 0)))
    out = pl.pallas_call(
        _mm_kernel,
        grid=(N // tn, Mp // tm),
        in_specs=[pl.BlockSpec((tm, K), lambda j, i: (i, 0)),
                  pl.BlockSpec((K, tn), lambda j, i: (0, j))],
        out_specs=pl.BlockSpec((tm, tn), lambda j, i: (i, j)),
        out_shape=jax.ShapeDtypeStruct((Mp, N), jnp.float32),
        compiler_params=pltpu.CompilerParams(
            dimension_semantics=("parallel", "parallel"),
            vmem_limit_bytes=VMEM_LIMIT_BYTES),
        name="mm",
    )(xb, w.astype(jnp.bfloat16))
    return out if Mp == M else out[:M]


def _mm3(x, w):
    B, T, K = x.shape
    return _mm(x.reshape(B * T, K), w).reshape(B, T, w.shape[1])


def _merge_kernel(*refs):
    n = (len(refs) - 1) // 3
    out_ref = refs[-1]
    acc = None
    for b in range(n):
        o_ref, w_ref, g_ref = refs[b], refs[n + b], refs[2 * n + b]
        term = jax.nn.sigmoid(g_ref[...]) * jnp.dot(o_ref[...], w_ref[...], preferred_element_type=jnp.float32)
        acc = term if acc is None else acc + term
    out_ref[...] = acc.astype(out_ref.dtype)


def _merge_branches(outs, w_branch, p_gate):
    n = len(outs)
    M = outs[0].shape[0]
    D = w_branch.shape[1]
    kb = outs[0].shape[1]
    assert all(o.shape == (M, kb) for o in outs) and w_branch.shape[0] == n * kb and p_gate.shape == (M, n * D)
    tm = min(MERGE_TILE_M, _round_up(M, BF16_SUBLANES))
    Mp = _round_up(M, tm)
    tn = MERGE_TILE_N
    obs = [o.astype(jnp.bfloat16) for o in outs]
    if Mp != M:
        obs = [jnp.pad(o, ((0, Mp - M), (0, 0))) for o in obs]
        p_gate = jnp.pad(p_gate, ((0, Mp - M), (0, 0)))
    o_specs = [pl.BlockSpec((tm, kb), lambda i, j: (i, 0)) for _ in range(n)]
    w_specs = [pl.BlockSpec((kb, tn), functools.partial(lambda i, j, b: (b, j), b=b)) for b in range(n)]
    g_specs = [pl.BlockSpec((tm, tn), functools.partial(lambda i, j, b: (i, b * (D // tn) + j), b=b))
               for b in range(n)]
    out = pl.pallas_call(
        _merge_kernel,
        grid=(Mp // tm, D // tn),
        in_specs=o_specs + w_specs + g_specs,
        out_specs=pl.BlockSpec((tm, tn), lambda i, j: (i, j)),
        out_shape=jax.ShapeDtypeStruct((Mp, D), jnp.bfloat16),
        compiler_params=pltpu.CompilerParams(dimension_semantics=("parallel", "parallel"),
                                             vmem_limit_bytes=VMEM_LIMIT_BYTES),
        name="merge_branches",
    )(*obs, *([w_branch] * n), *([p_gate] * n))
    return out if Mp == M else out[:M]


def _extract_top(x, order, n_out):
    outs, picks = [], []
    rank = jnp.full(x.shape, float(n_out), jnp.float32)
    for r in range(n_out):
        m = jnp.max(x, axis=0, keepdims=True)
        c = jnp.min(jnp.where(x == m, order, jnp.int32(2 ** 30)), axis=0, keepdims=True)
        hit = order == c
        x = jnp.where(hit, NEG_INF, x)
        rank = jnp.where(hit, float(r), rank)
        outs.append(m)
        picks.append(c)
    return outs, picks, rank


def _peer_select_kernel(q_ref, sk_ref, ea_ref, rk_ref, sv_ref):
    Tt = q_ref.shape[0]
    NK, DK2 = sk_ref.shape[2], sk_ref.shape[3]
    key_iota = lax.broadcasted_iota(jnp.int32, (NK, Tt), 0)
    scores, tops, ranks = [], [], []
    for p in range(2):
        s_t = lax.dot_general(sk_ref[0, p], q_ref[:, p * DK2:(p + 1) * DK2], (((1,), (1,)), ((), ())),
                              precision=HIGHEST, preferred_element_type=jnp.float32)
        sv, _, rank = _extract_top(s_t, key_iota, PEER_TOPK)
        for r in range(PEER_TOPK):
            sv_ref[p, r:r + 1, :] = sv[r]
        scores.append(s_t)
        tops.append(sv)
        ranks.append(rank)
    row8 = lax.broadcasted_iota(jnp.int32, (8, Tt), 0)
    row16 = lax.broadcasted_iota(jnp.int32, (PEER_TOPK, Tt), 0)
    strips = [sv_ref[0] + tops[1][0]]
    order = [row16 * PEER_TOPK]
    for b in range(1, 8):
        n_a = PEER_TOPK // (b + 1)
        strips.append(jnp.where(row8 < n_a, sv_ref[0, 0:8, :] + tops[1][b], NEG_INF))
        order.append(row8 * PEER_TOPK + b)
    strips.append(tops[0][0] + sv_ref[1, 8:16, :])
    order.append(row8 + 8)
    cv, picks, _ = _extract_top(jnp.concatenate(strips, axis=0), jnp.concatenate(order, axis=0), PEER_TOPK)
    z = jnp.ones_like(cv[0])
    for r in range(1, PEER_TOPK):
        z = z + jnp.exp(cv[r] - cv[0])
    n_b = jnp.zeros((NK, Tt), jnp.float32)
    for c in picks:
        n_b = n_b + jnp.where(ranks[0] == jnp.right_shift(c, 4).astype(jnp.float32), 1.0, 0.0)
    rk_ref[0] = n_b
    rk_ref[1] = ranks[1]
    ea_ref[0] = jnp.exp(scores[0] - tops[0][0])
    ea_ref[1] = jnp.exp(scores[1] - tops[1][0]) / z


def _peer_select(q, subkeys, tile):
    n = q.shape[0]
    H, _, NK, DK2 = subkeys.shape
    assert PEER_TOPK == 16 and NK % 8 == 0 and n % tile == 0
    tab = jax.ShapeDtypeStruct((2 * H, NK, n), jnp.float32)
    return pl.pallas_call(
        _peer_select_kernel,
        grid=(n // tile, H),
        in_specs=[pl.BlockSpec((tile, 2 * DK2), lambda i, h: (i, h)),
                  pl.BlockSpec((1, 2, NK, DK2), lambda i, h: (h, 0, 0, 0))],
        out_specs=[pl.BlockSpec((2, NK, tile), lambda i, h: (h, 0, i)),
                   pl.BlockSpec((2, NK, tile), lambda i, h: (h, 0, i))],
        out_shape=[tab, tab],
        scratch_shapes=[pltpu.VMEM((2, PEER_TOPK, tile), jnp.float32)],
        compiler_params=pltpu.CompilerParams(dimension_semantics=("parallel", "parallel"),
                                             vmem_limit_bytes=VMEM_LIMIT_BYTES),
        name="peer_select",
    )(q, subkeys)


def _gelu_exact(x):
    return 0.5 * x * (1.0 + lax.erf(x * (2.0 ** -0.5)))


def _peer_gate_kernel(h_ref, u_ref, ea_ref, rk_ref, g_ref):
    j = pl.program_id(1)
    Tt = h_ref.shape[0]
    Et = u_ref.shape[0]
    H2, NK, _ = ea_ref.shape
    n_i0 = Et // NK
    act_t = lax.dot_general(u_ref[...].astype(jnp.bfloat16), h_ref[...], (((1,), (1,)), ((), ())),
                            preferred_element_type=jnp.float32)
    for r in range(n_i0):
        i0 = j * n_i0 + r
        rows = slice(r * NK, (r + 1) * NK)
        a0 = [ea_ref[2 * h, pl.ds(i0, 1), :] for h in range(H2 // 2)]
        n_b = [rk_ref[2 * h, pl.ds(i0, 1), :] for h in range(H2 // 2)]
        for c in range(Tt // LANES):
            cols = slice(c * LANES, (c + 1) * LANES)
            acc = None
            for h in range(H2 // 2):
                keep = rk_ref[2 * h + 1, :, cols] < n_b[h][:, cols]
                term = jnp.where(keep, ea_ref[2 * h + 1, :, cols], 0.0) * a0[h][:, cols]
                acc = term if acc is None else acc + term
            g_ref[rows, cols] = (acc * _gelu_exact(act_t[rows, cols])).astype(jnp.bfloat16)


def _peer_gate(h, u, layer, ea, rk, token_tile, expert_tile):
    n, D = h.shape
    E = u.shape[1]
    H2, NK, _ = ea.shape
    assert E == NK * NK and expert_tile % NK == 0 and E % expert_tile == 0
    assert n % token_tile == 0 and token_tile % LANES == 0
    const = pl.Buffered(1)
    return pl.pallas_call(
        _peer_gate_kernel,
        grid=(n // token_tile, E // expert_tile),
        in_specs=[pl.BlockSpec((token_tile, D), lambda i, j: (i, 0), pipeline_mode=const),
                  pl.BlockSpec((None, expert_tile, D), lambda i, j: (layer, j, 0)),
                  pl.BlockSpec((H2, NK, token_tile), lambda i, j: (0, 0, i), pipeline_mode=const),
                  pl.BlockSpec((H2, NK, token_tile), lambda i, j: (0, 0, i), pipeline_mode=const)],
        out_specs=pl.BlockSpec((expert_tile, token_tile), lambda i, j: (j, i)),
        out_shape=jax.ShapeDtypeStruct((E, n), jnp.bfloat16),
        compiler_params=pltpu.CompilerParams(dimension_semantics=("parallel", "parallel"),
                                             vmem_limit_bytes=VMEM_LIMIT_BYTES),
        name="peer_gate",
    )(h, u, ea, rk)


def _mm_tn_kernel(a_ref, b_ref, o_ref, acc_ref):
    k = pl.program_id(2)
    part = lax.dot_general(a_ref[...], b_ref[...].astype(jnp.bfloat16), (((0,), (0,)), ((), ())),
                           preferred_element_type=jnp.float32)

    @pl.when(k == 0)
    def _():
        acc_ref[...] = part

    @pl.when(k > 0)
    def _():
        acc_ref[...] += part

    @pl.when(k == pl.num_programs(2) - 1)
    def _():
        o_ref[...] = acc_ref[...]


def _mm_tn(a_t, b, layer, tm, tn, tk):
    K, M = a_t.shape
    N = b.shape[2]
    assert M % tm == 0 and N % tn == 0 and K % tk == 0
    return pl.pallas_call(
        _mm_tn_kernel,
        grid=(M // tm, N // tn, K // tk),
        in_specs=[pl.BlockSpec((tk, tm), lambda i, j, k: (k, i)),
                  pl.BlockSpec((None, tk, tn), lambda i, j, k: (layer, k, j))],
        out_specs=pl.BlockSpec((tm, tn), lambda i, j, k: (i, j)),
        out_shape=jax.ShapeDtypeStruct((M, N), jnp.float32),
        scratch_shapes=[pltpu.VMEM((tm, tn), jnp.float32)],
        compiler_params=pltpu.CompilerParams(dimension_semantics=("parallel", "parallel", "arbitrary"),
                                             vmem_limit_bytes=VMEM_LIMIT_BYTES),
        name="mm_tn",
    )(a_t, b)


def peer_ffn(h, wq, subkeys, u_all, v_all, layer):
    B, T, D = h.shape
    n = B * T
    sel_tile = min(PEER_SEL_TILE, _round_up(n, 128))
    tok_tile = min(PEER_TOKEN_TILE, _round_up(n, 128))
    n_pad = _round_up(n, max(sel_tile, tok_tile))
    xt = h.reshape(n, D).astype(jnp.bfloat16)
    if n_pad != n:
        xt = jnp.pad(xt, ((0, n_pad - n), (0, 0)))
    q = _mm(xt, wq)
    ea, rk = _peer_select(q, subkeys, sel_tile)
    g_t = _peer_gate(xt, u_all, layer, ea, rk, tok_tile, PEER_EXPERT_TILE)
    out_tile_m = PEER_OUT_TILE_M if n_pad % PEER_OUT_TILE_M == 0 else tok_tile
    y = _mm_tn(g_t, v_all, layer, out_tile_m, min(D, PEER_OUT_TILE_N), PEER_OUT_TILE_K)
    return y[:n].reshape(B, T, D)


def _count_rows(mask):
    return jnp.sum(jnp.where(mask, 1.0, 0.0), axis=1, keepdims=True)


def _dsa_prompt_kernel(q_ref, k_ref, v_ref, qi_ref, ki_ref, wi_ref, o_ref, *, ktop, first_q_tile):
    Tq = q_ref.shape[1]
    T = k_ref.shape[1]
    n_bits = int(T - 1).bit_length()
    ki = ki_ref[0]
    wi = wi_ref[0]
    score = jnp.zeros((Tq, T), jnp.float32)
    for h in range(IDX_HEADS):
        d = lax.dot_general(qi_ref[0, :, h * IDX_HD:(h + 1) * IDX_HD], ki, (((1,), (1,)), ((), ())),
                            precision=HIGHEST, preferred_element_type=jnp.float32)
        score = score + wi[:, h:h + 1] * jnp.maximum(d, 0.0)
    bits = lax.bitcast_convert_type(score, jnp.int32)
    key = jnp.where(bits < 0, bits ^ jnp.int32(0x7FFFFFFF), bits)
    key = jnp.where(bits == jnp.int32(INT_MIN), 0, key)
    qpos = (first_q_tile + pl.program_id(1)) * Tq + lax.broadcasted_iota(jnp.int32, (Tq, T), 0)
    col = lax.broadcasted_iota(jnp.int32, (Tq, T), 1)
    allowed = col <= qpos
    key = jnp.where(allowed, key, jnp.int32(INT_MIN))
    kf = jnp.float32(ktop)

    tau = jnp.where(_count_rows(key >= 0) >= kf, jnp.int32(0), jnp.int32(INT_MIN))

    def tau_step(it, tau):
        cand = tau | jnp.left_shift(jnp.int32(1), 30 - it)
        return jnp.where(_count_rows(key >= cand) >= kf, cand, tau)

    tau = lax.fori_loop(0, 31, tau_step, tau)
    need = kf - _count_rows(key > tau)
    tie = (key == tau) & allowed

    def tie_step(it, p):
        cand = p | jnp.left_shift(jnp.int32(1), n_bits - 1 - it)
        return jnp.where(_count_rows(tie & (col < cand)) < need, cand, p)

    p = lax.fori_loop(0, n_bits, tie_step, jnp.zeros((Tq, 1), jnp.int32))
    sel = allowed & ((key > tau) | (tie & (col <= p)))

    group = ATT_HEADS // ATT_KV_HEADS
    scale = ATT_HD ** -0.5
    for g in range(ATT_KV_HEADS):
        kg = k_ref[0, :, g * ATT_HD:(g + 1) * ATT_HD].astype(jnp.bfloat16)
        vg = v_ref[0, :, g * ATT_HD:(g + 1) * ATT_HD].astype(jnp.bfloat16)
        for hh in range(group):
            lo = (g * group + hh) * ATT_HD
            qh = q_ref[0, :, lo:lo + ATT_HD].astype(jnp.bfloat16)
            logits = lax.dot_general(qh, kg, (((1,), (1,)), ((), ())), preferred_element_type=jnp.float32) * scale
            logits = jnp.where(sel, logits, NEG_INF)
            e = jnp.exp(logits - jnp.max(logits, axis=1, keepdims=True))
            out = jnp.dot(e.astype(jnp.bfloat16), vg, preferred_element_type=jnp.float32)
            o_ref[0, :, lo:lo + ATT_HD] = out / jnp.sum(e, axis=1, keepdims=True)


def dsa_prompt(q, k, v, qi, ki, wi):
    B, T = q.shape[:2]
    ktop = min(TOPK_MAX, T // 4)
    tq = min(DSA_Q_TILE, T)
    flat = lambda a: a.reshape(B, T, -1)
    args = (flat(q), flat(k), flat(v), flat(qi), ki, wi)
    n_groups = DSA_KEY_GROUPS if (T // tq) % DSA_KEY_GROUPS == 0 else 1
    tiles = T // tq // n_groups
    outs = []
    for grp in range(n_groups):
        first = grp * tiles
        t_keys = (first + tiles) * tq
        q_map = functools.partial(lambda b, i, first: (b, first + i, 0), first=first)
        kv_map = lambda b, i: (b, 0, 0)
        outs.append(pl.pallas_call(
            functools.partial(_dsa_prompt_kernel, ktop=ktop, first_q_tile=first),
            grid=(B, tiles),
            in_specs=[pl.BlockSpec((1, tq, ATT_Q), q_map),
                      pl.BlockSpec((1, t_keys, ATT_KV), kv_map),
                      pl.BlockSpec((1, t_keys, ATT_KV), kv_map),
                      pl.BlockSpec((1, tq, IDX_HEADS * IDX_HD), q_map),
                      pl.BlockSpec((1, t_keys, IDX_HD), kv_map),
                      pl.BlockSpec((1, tq, IDX_HEADS), q_map)],
            out_specs=pl.BlockSpec((1, tq, ATT_Q), lambda b, i: (b, i, 0)),
            out_shape=jax.ShapeDtypeStruct((B, tiles * tq, ATT_Q), jnp.float32),
            compiler_params=pltpu.CompilerParams(dimension_semantics=("parallel", "parallel"),
                                                 vmem_limit_bytes=VMEM_LIMIT_BYTES),
            name="dsa_prompt",
        )(*args))
    return jnp.concatenate(outs, axis=1).reshape(q.shape)


_NN = (((1,), (0,)), ((), ()))
_NT = (((1,), (1,)), ((), ()))
_TN = (((0,), (0,)), ((), ()))


def _dot_f32(x, y, dims=_NN):
    return lax.dot_general(x, y, dims, precision=HIGHEST, preferred_element_type=jnp.float32)


def _split_bf16(x):
    hi = x.astype(jnp.bfloat16)
    return hi, (x - hi.astype(jnp.float32)).astype(jnp.bfloat16)


def _dot3(x, y, dims=_NN):
    xh, xl = x if isinstance(x, tuple) else _split_bf16(x)
    yh, yl = y if isinstance(y, tuple) else _split_bf16(y)
    d = lambda p, q: lax.dot_general(p, q, dims, preferred_element_type=jnp.float32)
    return d(xh, yh) + (d(xh, yl) + d(xl, yh))


def _dot1(x, y, dims=_NN):
    return lax.dot_general(x.astype(jnp.bfloat16), y.astype(jnp.bfloat16), dims, preferred_element_type=jnp.float32)


def _unit_lower_solve(l_mats, rhs, size):
    xs = list(rhs)
    for step in range(max(1, int(size - 1).bit_length())):
        if step:
            l_mats = [_dot3(sp, sp) for sp in splits]
        splits = [_split_bf16(l) for l in l_mats]
        xs = [x + _dot3(sp, x) for sp, x in zip(splits, xs)]
    return xs


def _rwkv_kernel(r_ref, lw_ref, k_ref, v_ref, a_ref, b_ref, s0_ref, y_ref, s_ref):
    C = r_ref.shape[1]
    HG, N = s_ref.shape[1], s_ref.shape[2]
    row = lax.broadcasted_iota(jnp.int32, (C, C), 0)
    col = lax.broadcasted_iota(jnp.int32, (C, C), 1)
    incl = row >= col
    strict = row > col
    tri = jnp.where(incl, 1.0, 0.0)
    incl2 = (lax.broadcasted_iota(jnp.int32, (C, 2 * C), 0)
             >= (lax.broadcasted_iota(jnp.int32, (C, 2 * C), 1) & (C - 1)))

    @pl.when(pl.program_id(2) == 0)
    def _():
        s_ref[...] = s0_ref[...]

    heads = range(HG)
    head = lambda x, h: x[:, h * N:(h + 1) * N]
    lw = lw_ref[0]
    log_p = _dot_f32(tri, lw)
    inv_p = jnp.exp(-log_p)
    to_end = jnp.exp(log_p[C - 1:C, :] - log_p)
    p_end = jnp.exp(log_p[C - 1:C, :])
    a_t, r_t = a_ref[0] * jnp.exp(log_p - lw), r_ref[0] * jnp.exp(log_p)
    b_t, k_t = b_ref[0] * inv_p, k_ref[0] * inv_p
    b_end, k_end = b_ref[0] * to_end, k_ref[0] * to_end
    v_all = v_ref[0]
    v = [head(v_all, h) for h in heads]
    s0 = [s_ref[0, h] for h in heads]
    ar = [_split_bf16(jnp.concatenate([head(a_t, h), head(r_t, h)], axis=0)) for h in heads]
    g = [_dot3(ar[h], jnp.concatenate([head(b_t, h), head(k_t, h)], axis=0), _NT) for h in heads]
    ars = [_dot3(ar[h], s0[h], _NT) for h in heads]
    rhs = [ars[h][:C] + _dot3(jnp.where(strict, g[h][:C, C:], 0.0), v[h]) for h in heads]
    u = _unit_lower_solve([jnp.where(strict, g[h][:C, :C], 0.0) for h in heads], rhs, C)
    uv = [jnp.concatenate([u[h], v[h]], axis=0) for h in heads]
    for h in heads:
        y_ref[0, :, h * N:(h + 1) * N] = ars[h][C:] + _dot1(jnp.where(incl2, g[h][C:], 0.0), uv[h])
    for h in heads:
        bk_end = jnp.concatenate([head(b_end, h), head(k_end, h)], axis=0)
        s_ref[0, h] = s0[h] * head(p_end, h) + _dot3(uv[h], bk_end, _TN)


def rwkv7_scan(r, w_log, k, v, a, b, s0):
    B, T, H, N = r.shape
    C = min(RWKV_CHUNK, _round_up(T, 8))
    assert C & (C - 1) == 0
    Tp = _round_up(T, C)

    def prep(x):
        x = x.astype(jnp.float32).reshape(B, T, H * N)
        return jnp.pad(x, ((0, 0), (0, Tp - T), (0, 0))) if Tp != T else x

    hg = RWKV_HEAD_GROUP
    seq = pl.BlockSpec((1, C, hg * N), lambda bi, gi, ci: (bi, ci, gi))
    st = pl.BlockSpec((1, hg, N, N), lambda bi, gi, ci: (bi, gi, 0, 0))
    y, s_new = pl.pallas_call(
        _rwkv_kernel,
        grid=(B, H // hg, Tp // C),
        in_specs=[seq] * 6 + [st],
        out_specs=[seq, st],
        out_shape=[jax.ShapeDtypeStruct((B, Tp, H * N), jnp.float32),
                   jax.ShapeDtypeStruct((B, H, N, N), jnp.float32)],
        compiler_params=pltpu.CompilerParams(dimension_semantics=("parallel", "parallel", "arbitrary"),
                                             vmem_limit_bytes=VMEM_LIMIT_BYTES),
        name="rwkv7_scan",
    )(prep(r), prep(w_log), prep(k), prep(v), prep(a), prep(b), s0.astype(jnp.float32))
    return y[:, :T].reshape(B, T, H, N), s_new


def _gdn_kernel(q_ref, k_ref, v_ref, b_ref, g_ref, gr_ref, s0_ref, o_ref, s_ref):
    C = q_ref.shape[1]
    HG = s_ref.shape[1]
    DK, DV = s_ref.shape[2], s_ref.shape[3]
    row = lax.broadcasted_iota(jnp.int32, (C, C), 0)
    col = lax.broadcasted_iota(jnp.int32, (C, C), 1)
    incl = row >= col
    strict = row > col
    tri = jnp.where(incl, 1.0, 0.0)
    tri_t = jnp.where(row <= col, 1.0, 0.0)

    @pl.when(pl.program_id(2) == 0)
    def _():
        s_ref[...] = s0_ref[...]

    def unit(x):
        return x * lax.rsqrt(jnp.sum(x * x, axis=-1, keepdims=True) + EPS)

    heads = range(HG)
    lanes = lambda ref, h: jnp.broadcast_to(ref[0, 0, :, h:h + 1], (C, LANES))
    beta = [lanes(b_ref, h) for h in heads]
    k = [unit(k_ref[0, :, h * DK:(h + 1) * DK]) for h in heads]
    s0 = [s_ref[0, h] for h in heads]
    gc = [_dot_f32(tri, lanes(g_ref, h)) for h in heads]
    gc_row = [_dot_f32(gr_ref[0, h, 0], tri_t)[0:1, :] for h in heads]
    decay = [jnp.exp(jnp.where(incl, gc[h][:, :C] - gc_row[h], NEG_INF)) for h in heads]
    kb = [k[h] * beta[h] for h in heads]
    a_low = [jnp.where(strict, _dot3(kb[h], k[h], _NT) * decay[h], 0.0) for h in heads]
    rhs = [jnp.concatenate([v_ref[0, :, h * DV:(h + 1) * DV] * beta[h], kb[h] * jnp.exp(gc[h])], axis=1)
           for h in heads]
    sol = _unit_lower_solve([-a for a in a_low], rhs, C)
    v_new = [sol[h][:, :DV] - _dot3(sol[h][:, DV:], s0[h]) for h in heads]
    for h in heads:
        q = unit(q_ref[0, :, h * DK:(h + 1) * DK]) * (DK ** -0.5)
        qk = _dot1(q, k[h], _NT) * decay[h]
        o_ref[0, :, h * DV:(h + 1) * DV] = _dot1(q * jnp.exp(gc[h]), s0[h]) + _dot1(qk, v_new[h])
    for h in heads:
        g_last = gc[h][C - 1:C, :]
        s_ref[0, h] = s0[h] * jnp.exp(g_last) + _dot3(k[h] * jnp.exp(g_last - gc[h]), v_new[h], _TN)


def gated_delta_rule(qkv, beta, g, s0):
    B, T, _ = qkv.shape
    _, H, DK, DV = s0.shape
    assert DK == LANES and DV == LANES and qkv.shape[2] == (2 * DK + DV) * H
    C = min(GDN_CHUNK, _round_up(T, 8))
    Tp = _round_up(T, C)
    nc = Tp // C
    hg = GDN_HEAD_GROUP
    ng = H // hg
    pad_t = lambda x: jnp.pad(x, ((0, 0), (0, Tp - T), (0, 0))) if Tp != T else x
    qkv = pad_t(qkv.astype(jnp.float32))
    per_group = lambda x: jnp.transpose(pad_t(x.astype(jnp.float32)).reshape(B, Tp, ng, hg), (0, 2, 1, 3))
    g_rows = jnp.broadcast_to(jnp.moveaxis(pad_t(g.astype(jnp.float32)), 2, 1).reshape(B, H, nc, 1, C),
                              (B, H, nc, 8, C))
    cols = lambda first, d: pl.BlockSpec((1, C, hg * d), functools.partial(
        lambda bi, gi, ci, first: (bi, ci, first + gi), first=first))
    tok = pl.BlockSpec((1, 1, C, hg), lambda bi, gi, ci: (bi, gi, ci, 0))
    st = pl.BlockSpec((1, hg, DK, DV), lambda bi, gi, ci: (bi, gi, 0, 0))
    o, s_new = pl.pallas_call(
        _gdn_kernel,
        grid=(B, ng, nc),
        in_specs=[cols(0, DK), cols(ng, DK), cols(2 * ng, DV), tok, tok,
                  pl.BlockSpec((1, hg, 1, 8, C), lambda bi, gi, ci: (bi, gi, ci, 0, 0)), st],
        out_specs=[cols(0, DV), st],
        out_shape=[jax.ShapeDtypeStruct((B, Tp, H * DV), jnp.float32),
                   jax.ShapeDtypeStruct((B, H, DK, DV), jnp.float32)],
        compiler_params=pltpu.CompilerParams(dimension_semantics=("parallel", "parallel", "arbitrary"),
                                             vmem_limit_bytes=VMEM_LIMIT_BYTES),
        name="gated_delta_rule",
    )(qkv, qkv, qkv, per_group(beta), per_group(g), g_rows, s0.astype(jnp.float32))
    return o[:, :T], s_new


def _split(a, sizes, axis=-1):
    ends = [int(e) for e in np.cumsum(sizes)]
    return [lax.slice_in_dim(a, e - int(s), e, axis=axis % a.ndim) for s, e in zip(sizes, ends)]


def rmsnorm(x, gain):
    xf = x.astype(jnp.float32)
    y = xf * lax.rsqrt(jnp.mean(xf * xf, axis=-1, keepdims=True) + EPS)
    return (y * gain.astype(jnp.float32)).astype(x.dtype)


def l2norm(x):
    xf = x.astype(jnp.float32)
    return xf * lax.rsqrt(jnp.sum(xf * xf, axis=-1, keepdims=True) + EPS)


def partial_rope(x, pos):
    rd = x.shape[-1] // ROPE_DIV
    half = rd // 2
    inv_freq = ROPE_THETA ** (-jnp.arange(half, dtype=jnp.float32) * 2.0 / rd)
    ang = pos.astype(jnp.float32)[:, None] * inv_freq[None, :]
    cos = jnp.cos(ang)[:, None, :]
    sin = jnp.sin(ang)[:, None, :]
    xf = x[..., :rd].astype(jnp.float32)
    x1, x2 = xf[..., :half], xf[..., half:]
    rot = jnp.concatenate([x1 * cos - x2 * sin, x2 * cos + x1 * sin], axis=-1)
    return jnp.concatenate([rot.astype(x.dtype), x[..., rd:]], axis=-1)


def causal_conv(x, buf, w):
    T = x.shape[1]
    xp = jnp.concatenate([buf.astype(x.dtype), x], axis=1)
    y = xp[:, 0:T] * w[0]
    for i in range(1, CONV_W):
        y = y + xp[:, i:i + T] * w[i]
    return jax.nn.silu(y), xp[:, T:]


def _to_chunks(a, C):
    B, T = a.shape[:2]
    pad = (-T) % C
    a = jnp.pad(a.astype(jnp.float32), [(0, 0), (0, pad)] + [(0, 0)] * (a.ndim - 2))
    n = (T + pad) // C
    a = a.reshape((B, n, C) + a.shape[2:])
    return jnp.transpose(a, (1, 0, 3, 2) + tuple(range(4, a.ndim)))


def _from_chunks(o, T):
    n, B, H, C, X = o.shape
    return jnp.transpose(o, (1, 0, 3, 2, 4)).reshape(B, n * C, H, X)[:, :T]


def gla_chunked(q, k, v, log_a, s0):
    T = q.shape[1]
    C = min(GLA_CHUNK, T)
    qc, kc, vc = _to_chunks(q, C), _to_chunks(k, C), _to_chunks(v, C)
    bc = jnp.cumsum(_to_chunks(log_a, C), axis=-2)
    tri = jnp.tril(jnp.ones((C, C), bool))[..., None]

    def step(S, xs):
        q_n, k_n, v_n, b_n = xs
        rel = jnp.exp(jnp.where(tri, b_n[..., :, None, :] - b_n[..., None, :, :], -jnp.inf))
        att = jnp.einsum('bhik,bhijk,bhjk->bhij', q_n, rel, k_n)
        o = jnp.einsum('bhik,bhkv->bhiv', q_n * jnp.exp(b_n), S) + jnp.einsum('bhij,bhjv->bhiv', att, v_n)
        b_last = b_n[..., -1:, :]
        S = S * jnp.exp(b_last)[..., 0, :, None] + jnp.einsum('bhck,bhcv->bhkv', k_n * jnp.exp(b_last - b_n), v_n)
        return S, o

    S, o = lax.scan(step, s0.astype(jnp.float32), (qc, kc, vc, bc))
    return _from_chunks(o, T), S


def head_group_norm(y, w, b):
    B, T, H, N = y.shape
    mu = jnp.mean(y, axis=-1, keepdims=True)
    var = jnp.mean(jnp.square(y - mu), axis=-1, keepdims=True)
    yn = ((y - mu) * lax.rsqrt(var + RWKV_GN_EPS)).reshape(B, T, H * N)
    return yn * w.astype(jnp.float32) + b.astype(jnp.float32)


def indexer_scores(qi, wi, ki):
    dots = jnp.einsum('bqhd,bsd->bqhs', qi, ki).astype(jnp.float32)
    return jnp.einsum('bqh,bqhs->bqs', wi.astype(jnp.float32), jax.nn.relu(dots))


def select_keys(scores, qpos, ktop):
    S = scores.shape[-1]
    allowed = jnp.arange(S)[None, None, :] <= qpos[None, :, None]
    _, idx = lax.top_k(jnp.where(allowed, scores, -jnp.inf), ktop)
    return idx, idx <= qpos[None, :, None]


def sparse_attend(q, k_sel, v_sel, valid):
    B, Q, HQ, HD = q.shape
    qg = q.reshape(B, Q, ATT_KV_HEADS, HQ // ATT_KV_HEADS, HD)
    logits = jnp.einsum('bqhgd,bqkhd->bqhgk', qg, k_sel).astype(jnp.float32) * (HD ** -0.5)
    logits = jnp.where(valid[:, :, None, None, :], logits, -jnp.inf)
    p = jax.nn.softmax(logits, axis=-1).astype(v_sel.dtype)
    return jnp.einsum('bqhgk,bqkhd->bqhgd', p, v_sel).reshape(B, Q, HQ, HD)


def dsa_sample(q, k, v, qi, ki, wi, cache_k, cache_v, cache_ki, page_table):
    DB, DS = q.shape[:2]
    past = page_table.shape[1] * PAGE_SIZE
    ktop = min(TOPK_MAX, (past + DS) // 4)
    ki_past = cache_ki[page_table].reshape(DB, past, IDX_HD).astype(ki.dtype)
    ki_all = jnp.concatenate([ki_past, ki], axis=1)
    qpos = past + jnp.arange(DS)
    idx, valid = select_keys(indexer_scores(qi, wi, ki_all), qpos, ktop)
    is_new = idx >= past
    pidx = jnp.minimum(idx, past - 1)
    phys = jax.vmap(lambda pt, i: pt[i])(page_table, pidx // PAGE_SIZE)
    off = pidx % PAGE_SIZE
    nidx = jnp.clip(idx - past, 0, DS - 1)
    gather = jax.vmap(lambda rows, i: rows[i])
    sel = lambda cache, new: jnp.where(is_new[..., None, None], gather(new, nidx), cache[phys, off].astype(new.dtype))
    return sparse_attend(q, sel(cache_k, k), sel(cache_v, v), valid)


def gdn_branch(p, conv_buf, s0, lp):
    B, T, _ = p.shape
    conv_in, z, b_raw, a_raw = _split(p, (GDN_CONV_COLS, GDN_V, GDN_HEADS, GDN_HEADS))
    conv_out, new_buf = causal_conv(conv_in, conv_buf, lp['gdn_conv_w'])
    beta = jax.nn.sigmoid(b_raw.astype(jnp.float32))
    g = -jnp.exp(lp['gdn_A_log'].astype(jnp.float32)) * jax.nn.softplus(a_raw.astype(jnp.float32) + lp['gdn_dt_bias'].astype(jnp.float32))
    o, s_new = gated_delta_rule(conv_out, beta, g, s0)
    o = o.reshape(B, T, GDN_HEADS, GDN_DV)
    o = rmsnorm(o, lp['gdn_norm']) * jax.nn.silu(z.reshape(B, T, GDN_HEADS, GDN_DV).astype(jnp.float32))
    return o.reshape(B, T, GDN_V), new_buf, s_new


def gla_branch(p, s0, lp):
    B, T, _ = p.shape
    q, k, v, r, gd = _split(p, (GLA_QK, GLA_QK, GLA_V, GLA_V, GLA_RANK))
    hs = (B, T, GLA_HEADS)
    log_a = jax.nn.log_sigmoid((gd @ lp['gla_gate_up'] + lp['gla_gate_bias']).astype(jnp.float32)) / GLA_TAU
    o, s_new = gla_chunked(q.reshape(hs + (GLA_DK,)) * (GLA_DK ** -0.5), k.reshape(hs + (GLA_DK,)),
                           v.reshape(hs + (GLA_DV,)), log_a.reshape(hs + (GLA_DK,)), s0)
    o = rmsnorm(o, lp['gla_norm']) * jax.nn.silu(r.reshape(hs + (GLA_DV,)).astype(jnp.float32))
    return o.reshape(B, T, GLA_V), s_new


def dsa_branch(p, pos, attend):
    B, T, _ = p.shape
    q, k, v, qi, ki, wi = _split(p, (ATT_Q, ATT_KV, ATT_KV, IDX_HEADS * IDX_HD, IDX_HD, IDX_HEADS))
    q = partial_rope(q.reshape(B, T, ATT_HEADS, ATT_HD), pos)
    k = partial_rope(k.reshape(B, T, ATT_KV_HEADS, ATT_HD), pos)
    v = v.reshape(B, T, ATT_KV_HEADS, ATT_HD)
    qi = partial_rope(qi.reshape(B, T, IDX_HEADS, IDX_HD), pos)
    ki = partial_rope(ki.reshape(B, T, 1, IDX_HD), pos)[:, :, 0]
    o = attend(q, k, v, qi, ki, wi)
    return o.reshape(B, T, ATT_Q), (k, v, ki)


def rwkv_branch(p, shift_buf, s0, lp):
    B, T, width = p.shape
    pad_cols = lambda x: jnp.pad(x, [(0, 0)] * (x.ndim - 1) + [(0, width - RWKV_COLS)])
    prev = jnp.concatenate([pad_cols(shift_buf.astype(p.dtype)), p[:, :-1]], axis=1)
    pm = p + (prev - p) * pad_cols(lp['rwkv_mu'])
    r, wd, k, v, ad, gd = _split(pm, (RWKV_W, RWKV_W_RANK, RWKV_W, RWKV_W, RWKV_A_RANK, RWKV_G_RANK))
    hs = (B, T, RWKV_HEADS, RWKV_HD)
    w_log = -jax.nn.softplus(-(lp['rwkv_w0'] + jnp.tanh(wd) @ lp['rwkv_w2']).astype(jnp.float32)) - 0.5
    log_decay = -jnp.exp(w_log)
    a = jax.nn.sigmoid((lp['rwkv_a0'] + ad @ lp['rwkv_a2']).astype(jnp.float32))
    g = (jax.nn.sigmoid(gd) @ lp['rwkv_g2']).astype(jnp.float32)
    kf = k.astype(jnp.float32)
    kk = l2norm((kf * lp['rwkv_kk']).reshape(hs))
    kf = (kf * (1.0 + (a - 1.0) * lp['rwkv_ka'])).reshape(hs)
    rf = r.astype(jnp.float32).reshape(hs)
    vf = v.astype(jnp.float32).reshape(hs)
    y, s_new = rwkv7_scan(rf, log_decay.reshape(hs), kf, vf, -kk, kk * a.reshape(hs), s0)
    y = head_group_norm(y, lp['rwkv_ln_w'], lp['rwkv_ln_b'])
    bonus = (jnp.sum(rf * kf * lp['rwkv_rk'], axis=-1, keepdims=True) * vf).reshape(B, T, RWKV_W)
    return (y + bonus) * g, p[:, -1:, :RWKV_COLS], s_new


def token_mixers(h, pos, st, lp, attend):
    gdn_s, gdn_buf, gla_s, rwkv_s, rwkv_buf = st
    B, T, _ = h.shape
    hb = h.astype(jnp.bfloat16)
    p_gdn, p_gla, p_att, p_rwkv, p_gate = [_mm3(hb, w) for w in lp['w_in_groups']]
    o_gdn, gdn_buf, gdn_s = gdn_branch(p_gdn, gdn_buf, gdn_s, lp)
    o_gla, gla_s = gla_branch(p_gla, gla_s, lp)
    o_att, rows = dsa_branch(p_att, pos, attend)
    o_rwkv, rwkv_buf, rwkv_s = rwkv_branch(p_rwkv, rwkv_buf, rwkv_s, lp)
    assert len(set(BRANCH_SPLITS)) == 1
    merged = _merge_branches([o.reshape(B * T, -1) for o in (o_gdn, o_gla, o_att, o_rwkv)],
                             lp['w_branch'], p_gate.reshape(B * T, GATE_COLS))
    out = _mm(merged, lp['w_out']).reshape(B, T, D_MODEL)
    return out, rows + (gdn_s, gdn_buf, gla_s, rwkv_s, rwkv_buf)


def trunk_layer(x, mod, pos, st, lp, attend):
    B = x.shape[0]
    mod = mod.reshape(B, N_MOD, 1, D_MODEL)
    shift1, scale1, gate1, shift2, scale2, gate2 = [mod[:, i] for i in range(N_MOD)]
    h = rmsnorm(x, lp['norm1']) * (1.0 + scale1) + shift1
    mix, new_st = token_mixers(h, pos, st, lp, attend)
    x = x + gate1 * mix
    h = rmsnorm(x, lp['norm2']) * (1.0 + scale2) + shift2
    x = x + gate2 * peer_ffn(h, lp['peer_wq'], lp['peer_subkeys'], lp['peer_u_all'], lp['peer_v_all'], lp['layer'])
    return x, new_st


def kernel(x_prompt, x_sample, cache_k, cache_v, cache_kidx, state_gdn, state_gdn_conv, state_gla, state_rwkv, state_rwkv_shift, page_table, c_prompt, c_sample, w_ada, b_ada, norm1, norm2, w_in, gdn_conv_w, gdn_A_log, gdn_dt_bias, gdn_norm, gla_gate_up, gla_gate_bias, gla_norm, rwkv_mu, rwkv_w0, rwkv_w2, rwkv_a0, rwkv_a2, rwkv_g2, rwkv_kk, rwkv_ka, rwkv_rk, rwkv_ln_w, rwkv_ln_b, w_branch, w_out, peer_wq, peer_subkeys, peer_u, peer_v, final_norm):
    Bp, T, _ = x_prompt.shape
    DS = x_sample.shape[1]
    depth = w_in.shape[0]
    past = page_table.shape[1] * PAGE_SIZE
    pos_p = jnp.arange(T, dtype=jnp.int32)
    pos_s = past + jnp.arange(DS, dtype=jnp.int32)
    st_p0 = (jnp.zeros((Bp, GDN_HEADS, GDN_DK, GDN_DV), jnp.float32),
             jnp.zeros((Bp, CONV_W - 1, GDN_CONV_COLS), x_prompt.dtype),
             jnp.zeros((Bp, GLA_HEADS, GLA_DK, GLA_DV), jnp.float32),
             jnp.zeros((Bp, RWKV_HEADS, RWKV_HD, RWKV_HD), jnp.float32),
             jnp.zeros((Bp, 1, RWKV_COLS), x_prompt.dtype))
    weights = {'w_ada': w_ada, 'b_ada': b_ada, 'norm1': norm1, 'norm2': norm2, 'w_in': w_in,
               'gdn_conv_w': gdn_conv_w, 'gdn_A_log': gdn_A_log, 'gdn_dt_bias': gdn_dt_bias, 'gdn_norm': gdn_norm,
               'gla_gate_up': gla_gate_up, 'gla_gate_bias': gla_gate_bias, 'gla_norm': gla_norm,
               'rwkv_mu': rwkv_mu, 'rwkv_w0': rwkv_w0, 'rwkv_w2': rwkv_w2, 'rwkv_a0': rwkv_a0, 'rwkv_a2': rwkv_a2,
               'rwkv_g2': rwkv_g2, 'rwkv_kk': rwkv_kk, 'rwkv_ka': rwkv_ka, 'rwkv_rk': rwkv_rk,
               'rwkv_ln_w': rwkv_ln_w, 'rwkv_ln_b': rwkv_ln_b, 'w_branch': w_branch, 'w_out': w_out,
               'peer_wq': peer_wq, 'peer_subkeys': peer_subkeys, 'peer_u': peer_u, 'peer_v': peer_v}
    xp, xs = x_prompt, x_sample
    out_p, out_s = [], []
    for l in range(depth):
        lp = {name: arr[l] for name, arr in weights.items()}
        lp.update(layer=l, peer_u_all=peer_u, peer_v_all=peer_v)
        for name in ('w_branch', 'w_out', 'peer_wq'):
            lp[name] = lp[name].astype(jnp.bfloat16)
        lp['w_in_groups'] = [_pad_cols_bf16(w) for w in _split(lp['w_in'], IN_SPLITS)]
        mod = _mm_few_rows_f32w(jax.nn.silu(jnp.concatenate([c_prompt, c_sample], axis=0)), w_ada, l) + lp['b_ada']
        xp, new_p = trunk_layer(xp, mod[:Bp], pos_p, st_p0, lp, dsa_prompt)
        attend_s = functools.partial(dsa_sample, cache_k=cache_k[l], cache_v=cache_v[l],
                                     cache_ki=cache_kidx[l], page_table=page_table)
        st_s0 = (state_gdn[l], state_gdn_conv[l], state_gla[l], state_rwkv[l], state_rwkv_shift[l])
        xs, new_s = trunk_layer(xs, mod[Bp:], pos_s, st_s0, lp, attend_s)
        out_p.append(new_p)
        out_s.append(new_s)
    y_prompt = rmsnorm(xp, final_norm)
    y_sample = rmsnorm(xs, final_norm)
    k_p, v_p, kidx_p, gdn_p, gdn_conv_p, gla_p, rwkv_p, rwkv_shift_p = [jnp.stack(z) for z in zip(*out_p)]
    k_s, v_s, kidx_s, gdn_s, gdn_conv_s, gla_s, rwkv_s, rwkv_shift_s = [jnp.stack(z) for z in zip(*out_s)]
    return (y_prompt, y_sample, k_p, v_p, kidx_p, gdn_p, gdn_conv_p, gla_p, rwkv_p, rwkv_shift_p,
            k_s, v_s, kidx_s, gdn_s, gdn_conv_s, gla_s, rwkv_s, rwkv_shift_s)
```

```python
import math, functools
import jax, jax.numpy as jnp
from jax import lax
import numpy as np
from jax.experimental import pallas as pl
from jax.experimental.pallas import tpu as pltpu

D_MODEL = 4096
PAGE_SIZE = 128
N_MOD = 6
EPS = 1e-6
N_BRANCH = 4

GDN_HEADS = 8
GDN_DK = 128
GDN_DV = 128
CONV_W = 4
GDN_CHUNK = 64
GDN_QK = GDN_HEADS * GDN_DK
GDN_V = GDN_HEADS * GDN_DV
GDN_CONV_COLS = 2 * GDN_QK + GDN_V
GDN_COLS = GDN_CONV_COLS + GDN_V + 2 * GDN_HEADS

GLA_HEADS = 4
GLA_DK = 128
GLA_DV = 256
GLA_RANK = 16
GLA_TAU = 16.0
GLA_CHUNK = 64
GLA_QK = GLA_HEADS * GLA_DK
GLA_V = GLA_HEADS * GLA_DV
GLA_COLS = 2 * GLA_QK + 2 * GLA_V + GLA_RANK

ATT_HEADS = 8
ATT_KV_HEADS = 2
ATT_HD = 128
IDX_HEADS = 8
IDX_HD = 64
TOPK_MAX = 256
Q_BLOCK = 128
ROPE_THETA = 500000.0
ROPE_DIV = 4
ATT_Q = ATT_HEADS * ATT_HD
ATT_KV = ATT_KV_HEADS * ATT_HD
ATT_COLS = ATT_Q + 2 * ATT_KV + IDX_HEADS * IDX_HD + IDX_HD + IDX_HEADS

RWKV_HEADS = 16
RWKV_HD = 64
RWKV_W_RANK = 64
RWKV_A_RANK = 64
RWKV_G_RANK = 128
RWKV_GN_EPS = 64e-5
RWKV_W = RWKV_HEADS * RWKV_HD
RWKV_COLS = 3 * RWKV_W + RWKV_W_RANK + RWKV_A_RANK + RWKV_G_RANK

GATE_COLS = N_BRANCH * D_MODEL
IN_SPLITS = (GDN_COLS, GLA_COLS, ATT_COLS, RWKV_COLS, GATE_COLS)
IN_COLS = sum(IN_SPLITS)
BRANCH_SPLITS = (GDN_V, GLA_V, ATT_Q, RWKV_W)
MIX_WIDTH = sum(BRANCH_SPLITS)

PEER_HEADS = 8
PEER_NKEYS = 128
PEER_DKEY = 256
PEER_TOPK = 16
PEER_N = PEER_NKEYS * PEER_NKEYS
PEER_BLOCK = 64

VMEM_LIMIT_BYTES = 56 * 1024 * 1024
MM_TILE_M = 1024
MM_TILE_N = 512
MM_TILE_N_FEW_ROWS = 2048
MM_TILE_K_F32_WEIGHT = 128
MM_FEW_ROWS = 64
MERGE_TILE_M = 512
MERGE_TILE_N = 512
BF16_SUBLANES = 16
LANES = 128
PEER_SEL_TILE = 256
PEER_TOKEN_TILE = 512
PEER_EXPERT_TILE = 512
PEER_OUT_TILE_M = 1024
PEER_OUT_TILE_N = 1024
PEER_OUT_TILE_K = 2048
DSA_Q_TILE = 128
DSA_KEY_GROUPS = 8
RWKV_CHUNK = 64
RWKV_HEAD_GROUP = 16
GDN_HEAD_GROUP = 8

HIGHEST = lax.Precision.HIGHEST
NEG_INF = float("-inf")
INT_MIN = -2 ** 31


def _mm_kernel(x_ref, w_ref, o_ref):
    o_ref[...] = jnp.dot(x_ref[...], w_ref[...], preferred_element_type=jnp.float32)


def _mm_rows_kernel(x_ref, w_ref, o_ref):
    part = jnp.dot(x_ref[...], w_ref[...].astype(jnp.bfloat16), preferred_element_type=jnp.float32)

    @pl.when(pl.program_id(0) == 0)
    def _():
        o_ref[...] = part

    @pl.when(pl.program_id(0) > 0)
    def _():
        o_ref[...] += part


def _mm_few_rows_f32w(x, w, layer):
    M, K = x.shape
    N = w.shape[2]
    tk = MM_TILE_K_F32_WEIGHT
    assert K % tk == 0 and N % LANES == 0 and M <= MM_FEW_ROWS
    Mp = _round_up(M, BF16_SUBLANES)
    xb = jnp.pad(x.astype(jnp.bfloat16), ((0, Mp - M), (0, 0)))
    out = pl.pallas_call(
        _mm_rows_kernel,
        grid=(K // tk,),
        in_specs=[pl.BlockSpec((Mp, tk), lambda k: (0, k)),
                  pl.BlockSpec((None, tk, N), lambda k: (layer, k, 0))],
        out_specs=pl.BlockSpec((Mp, N), lambda k: (0, 0)),
        out_shape=jax.ShapeDtypeStruct((Mp, N), jnp.float32),
        compiler_params=pltpu.CompilerParams(dimension_semantics=("arbitrary",),
                                             vmem_limit_bytes=VMEM_LIMIT_BYTES),
        name="mm_few_rows",
    )(xb, w)
    return out[:M]


def _round_up(n, m):
    return (n + m - 1) // m * m


def _pad_cols_bf16(w, multiple=None):
    multiple = multiple or MM_TILE_N
    n_pad = _round_up(w.shape[1], multiple)
    wb = w.astype(jnp.bfloat16)
    return wb if n_pad == w.shape[1] else jnp.pad(wb, ((0, 0), (0, n_pad - w.shape[1])))


def _mm(x, w):
    M, K = x.shape
    N = w.shape[1]
    assert N % MM_TILE_N == 0
    tm = min(MM_TILE_M, _round_up(M, BF16_SUBLANES))
    Mp = _round_up(M, tm)
    tn = next(t for t in (MM_TILE_N_FEW_ROWS, MM_TILE_N) if N % t == 0) if Mp <= MM_FEW_ROWS else MM_TILE_N
    xb = x.astype(jnp.bfloat16)
    if Mp != M:
        xb = jnp.pad(xb, ((0, Mp - M), (0, 0)))
    out = pl.pallas_call(
        _mm_kernel,
        grid=(N // tn, Mp // tm),
        in_specs=[pl.BlockSpec((tm, K), lambda j, i: (i, 0)),
                  pl.BlockSpec((K, tn), lambda j, i: (0, j))],
        out_specs=pl.BlockSpec((tm, tn), lambda j, i: (i, j)),
        out_shape=jax.ShapeDtypeStruct((Mp, N), jnp.float32),
        compiler_params=pltpu.CompilerParams(
            dimension_semantics=("parallel", "parallel"),
            vmem_limit_bytes=VMEM_LIMIT_BYTES),
        name="mm",
    )(xb, w.astype(jnp.bfloat16))
    return out if Mp == M else out[:M]


def _mm3(x, w):
    B, T, K = x.shape
    return _mm(x.reshape(B * T, K), w).reshape(B, T, w.shape[1])


def _merge_kernel(*refs):
    n = (len(refs) - 1) // 3
    out_ref = refs[-1]
    acc = None
    for b in range(n):
        o_ref, w_ref, g_ref = refs[b], refs[n + b], refs[2 * n + b]
        term = jax.nn.sigmoid(g_ref[...]) * jnp.dot(o_ref[...], w_ref[...], preferred_element_type=jnp.float32)
        acc = term if acc is None else acc + term
    out_ref[...] = acc.astype(out_ref.dtype)


def _merge_branches(outs, w_branch, p_gate):
    n = len(outs)
    M = outs[0].shape[0]
    D = w_branch.shape[1]
    kb = outs[0].shape[1]
    assert all(o.shape == (M, kb) for o in outs) and w_branch.shape[0] == n * kb and p_gate.shape == (M, n * D)
    tm = min(MERGE_TILE_M, _round_up(M, BF16_SUBLANES))
    Mp = _round_up(M, tm)
    tn = MERGE_TILE_N
    obs = [o.astype(jnp.bfloat16) for o in outs]
    if Mp != M:
        obs = [jnp.pad(o, ((0, Mp - M), (0, 0))) for o in obs]
        p_gate = jnp.pad(p_gate, ((0, Mp - M), (0, 0)))
    o_specs = [pl.BlockSpec((tm, kb), lambda i, j: (i, 0)) for _ in range(n)]
    w_specs = [pl.BlockSpec((kb, tn), functools.partial(lambda i, j, b: (b, j), b=b)) for b in range(n)]
    g_specs = [pl.BlockSpec((tm, tn), functools.partial(lambda i, j, b: (i, b * (D // tn) + j), b=b))
               for b in range(n)]
    out = pl.pallas_call(
        _merge_kernel,
        grid=(Mp // tm, D // tn),
        in_specs=o_specs + w_specs + g_specs,
        out_specs=pl.BlockSpec((tm, tn), lambda i, j: (i, j)),
        out_shape=jax.ShapeDtypeStruct((Mp, D), jnp.bfloat16),
        compiler_params=pltpu.CompilerParams(dimension_semantics=("parallel", "parallel"),
                                             vmem_limit_bytes=VMEM_LIMIT_BYTES),
        name="merge_branches",
    )(*obs, *([w_branch] * n), *([p_gate] * n))
    return out if Mp == M else out[:M]


def _extract_top(x, order, n_out):
    outs, picks = [], []
    rank = jnp.full(x.shape, float(n_out), jnp.float32)
    for r in range(n_out):
        m = jnp.max(x, axis=0, keepdims=True)
        c = jnp.min(jnp.where(x == m, order, jnp.int32(2 ** 30)), axis=0, keepdims=True)
        hit = order == c
        x = jnp.where(hit, NEG_INF, x)
        rank = jnp.where(hit, float(r), rank)
        outs.append(m)
        picks.append(c)
    return outs, picks, rank


def _peer_select_kernel(q_ref, sk_ref, ea_ref, rk_ref, sv_ref):
    Tt = q_ref.shape[0]
    NK, DK2 = sk_ref.shape[2], sk_ref.shape[3]
    key_iota = lax.broadcasted_iota(jnp.int32, (NK, Tt), 0)
    scores, tops, ranks = [], [], []
    for p in range(2):
        s_t = lax.dot_general(sk_ref[0, p], q_ref[:, p * DK2:(p + 1) * DK2], (((1,), (1,)), ((), ())),
                              precision=HIGHEST, preferred_element_type=jnp.float32)
        sv, _, rank = _extract_top(s_t, key_iota, PEER_TOPK)
        for r in range(PEER_TOPK):
            sv_ref[p, r:r + 1, :] = sv[r]
        scores.append(s_t)
        tops.append(sv)
        ranks.append(rank)
    row8 = lax.broadcasted_iota(jnp.int32, (8, Tt), 0)
    row16 = lax.broadcasted_iota(jnp.int32, (PEER_TOPK, Tt), 0)
    strips = [sv_ref[0] + tops[1][0]]
    order = [row16 * PEER_TOPK]
    for b in range(1, 8):
        n_a = PEER_TOPK // (b + 1)
        strips.append(jnp.where(row8 < n_a, sv_ref[0, 0:8, :] + tops[1][b], NEG_INF))
        order.append(row8 * PEER_TOPK + b)
    strips.append(tops[0][0] + sv_ref[1, 8:16, :])
    order.append(row8 + 8)
    cv, picks, _ = _extract_top(jnp.concatenate(strips, axis=0), jnp.concatenate(order, axis=0), PEER_TOPK)
    z = jnp.ones_like(cv[0])
    for r in range(1, PEER_TOPK):
        z = z + jnp.exp(cv[r] - cv[0])
    n_b = jnp.zeros((NK, Tt), jnp.float32)
    for c in picks:
        n_b = n_b + jnp.where(ranks[0] == jnp.right_shift(c, 4).astype(jnp.float32), 1.0, 0.0)
    rk_ref[0] = n_b
    rk_ref[1] = ranks[1]
    ea_ref[0] = jnp.exp(scores[0] - tops[0][0])
    ea_ref[1] = jnp.exp(scores[1] - tops[1][0]) / z


def _peer_select(q, subkeys, tile):
    n = q.shape[0]
    H, _, NK, DK2 = subkeys.shape
    assert PEER_TOPK == 16 and NK % 8 == 0 and n % tile == 0
    tab = jax.ShapeDtypeStruct((2 * H, NK, n), jnp.float32)
    return pl.pallas_call(
        _peer_select_kernel,
        grid=(n // tile, H),
        in_specs=[pl.BlockSpec((tile, 2 * DK2), lambda i, h: (i, h)),
                  pl.BlockSpec((1, 2, NK, DK2), lambda i, h: (h, 0, 0, 0))],
        out_specs=[pl.BlockSpec((2, NK, tile), lambda i, h: (h, 0, i)),
                   pl.BlockSpec((2, NK, tile), lambda i, h: (h, 0, i))],
        out_shape=[tab, tab],
        scratch_shapes=[pltpu.VMEM((2, PEER_TOPK, tile), jnp.float32)],
        compiler_params=pltpu.CompilerParams(dimension_semantics=("parallel", "parallel"),
                                             vmem_limit_bytes=VMEM_LIMIT_BYTES),
        name="peer_select",
    )(q, subkeys)


def _gelu_exact(x):
    return 0.5 * x * (1.0 + lax.erf(x * (2.0 ** -0.5)))


def _peer_gate_kernel(h_ref, u_ref, ea_ref, rk_ref, g_ref):
    j = pl.program_id(1)
    Tt = h_ref.shape[0]
    Et = u_ref.shape[0]
    H2, NK, _ = ea_ref.shape
    n_i0 = Et // NK
    act_t = lax.dot_general(u_ref[...].astype(jnp.bfloat16), h_ref[...], (((1,), (1,)), ((), ())),
                            preferred_element_type=jnp.float32)
    for r in range(n_i0):
        i0 = j * n_i0 + r
        rows = slice(r * NK, (r + 1) * NK)
        a0 = [ea_ref[2 * h, pl.ds(i0, 1), :] for h in range(H2 // 2)]
        n_b = [rk_ref[2 * h, pl.ds(i0, 1), :] for h in range(H2 // 2)]
        for c in range(Tt // LANES):
            cols = slice(c * LANES, (c + 1) * LANES)
            acc = None
            for h in range(H2 // 2):
                keep = rk_ref[2 * h + 1, :, cols] < n_b[h][:, cols]
                term = jnp.where(keep, ea_ref[2 * h + 1, :, cols], 0.0) * a0[h][:, cols]
                acc = term if acc is None else acc + term
            g_ref[rows, cols] = (acc * _gelu_exact(act_t[rows, cols])).astype(jnp.bfloat16)


def _peer_gate(h, u, layer, ea, rk, token_tile, expert_tile):
    n, D = h.shape
    E = u.shape[1]
    H2, NK, _ = ea.shape
    assert E == NK * NK and expert_tile % NK == 0 and E % expert_tile == 0
    assert n % token_tile == 0 and token_tile % LANES == 0
    const = pl.Buffered(1)
    return pl.pallas_call(
        _peer_gate_kernel,
        grid=(n // token_tile, E // expert_tile),
        in_specs=[pl.BlockSpec((token_tile, D), lambda i, j: (i, 0), pipeline_mode=const),
                  pl.BlockSpec((None, expert_tile, D), lambda i, j: (layer, j, 0)),
                  pl.BlockSpec((H2, NK, token_tile), lambda i, j: (0, 0, i), pipeline_mode=const),
                  pl.BlockSpec((H2, NK, token_tile), lambda i, j: (0, 0, i), pipeline_mode=const)],
        out_specs=pl.BlockSpec((expert_tile, token_tile), lambda i, j: (j, i)),
        out_shape=jax.ShapeDtypeStruct((E, n), jnp.bfloat16),
        compiler_params=pltpu.CompilerParams(dimension_semantics=("parallel", "parallel"),
                                             vmem_limit_bytes=VMEM_LIMIT_BYTES),
        name="peer_gate",
    )(h, u, ea, rk)


def _mm_tn_kernel(a_ref, b_ref, o_ref, acc_ref):
    k = pl.program_id(2)
    part = lax.dot_general(a_ref[...], b_ref[...].astype(jnp.bfloat16), (((0,), (0,)), ((), ())),
                           preferred_element_type=jnp.float32)

    @pl.when(k == 0)
    def _():
        acc_ref[...] = part

    @pl.when(k > 0)
    def _():
        acc_ref[...] += part

    @pl.when(k == pl.num_programs(2) - 1)
    def _():
        o_ref[...] = acc_ref[...]


def _mm_tn(a_t, b, layer, tm, tn, tk):
    K, M = a_t.shape
    N = b.shape[2]
    assert M % tm == 0 and N % tn == 0 and K % tk == 0
    return pl.pallas_call(
        _mm_tn_kernel,
        grid=(M // tm, N // tn, K // tk),
        in_specs=[pl.BlockSpec((tk, tm), lambda i, j, k: (k, i)),
                  pl.BlockSpec((None, tk, tn), lambda i, j, k: (layer, k, j))],
        out_specs=pl.BlockSpec((tm, tn), lambda i, j, k: (i, j)),
        out_shape=jax.ShapeDtypeStruct((M, N), jnp.float32),
        scratch_shapes=[pltpu.VMEM((tm, tn), jnp.float32)],
        compiler_params=pltpu.CompilerParams(dimension_semantics=("parallel", "parallel", "arbitrary"),
                                             vmem_limit_bytes=VMEM_LIMIT_BYTES),
        name="mm_tn",
    )(a_t, b)


def peer_ffn(h, wq, subkeys, u_all, v_all, layer):
    B, T, D = h.shape
    n = B * T
    sel_tile = min(PEER_SEL_TILE, _round_up(n, 128))
    tok_tile = min(PEER_TOKEN_TILE, _round_up(n, 128))
    n_pad = _round_up(n, max(sel_tile, tok_tile))
    xt = h.reshape(n, D).astype(jnp.bfloat16)
    if n_pad != n:
        xt = jnp.pad(xt, ((0, n_pad - n), (0, 0)))
    q = _mm(xt, wq)
    ea, rk = _peer_select(q, subkeys, sel_tile)
    g_t = _peer_gate(xt, u_all, layer, ea, rk, tok_tile, PEER_EXPERT_TILE)
    out_tile_m = PEER_OUT_TILE_M if n_pad % PEER_OUT_TILE_M == 0 else tok_tile
    y = _mm_tn(g_t, v_all, layer, out_tile_m, min(D, PEER_OUT_TILE_N), PEER_OUT_TILE_K)
    return y[:n].reshape(B, T, D)


def _count_rows(mask):
    return jnp.sum(jnp.where(mask, 1.0, 0.0), axis=1, keepdims=True)


def _dsa_prompt_kernel(q_ref, k_ref, v_ref, qi_ref, ki_ref, wi_ref, o_ref, *, ktop, first_q_tile):
    Tq = q_ref.shape[1]
    T = k_ref.shape[1]
    n_bits = int(T - 1).bit_length()
    ki = ki_ref[0]
    wi = wi_ref[0]
    score = jnp.zeros((Tq, T), jnp.float32)
    for h in range(IDX_HEADS):
        d = lax.dot_general(qi_ref[0, :, h * IDX_HD:(h + 1) * IDX_HD], ki, (((1,), (1,)), ((), ())),
                            precision=HIGHEST, preferred_element_type=jnp.float32)
        score = score + wi[:, h:h + 1] * jnp.maximum(d, 0.0)
    bits = lax.bitcast_convert_type(score, jnp.int32)
    key = jnp.where(bits < 0, bits ^ jnp.int32(0x7FFFFFFF), bits)
    key = jnp.where(bits == jnp.int32(INT_MIN), 0, key)
    qpos = (first_q_tile + pl.program_id(1)) * Tq + lax.broadcasted_iota(jnp.int32, (Tq, T), 0)
    col = lax.broadcasted_iota(jnp.int32, (Tq, T), 1)
    allowed = col <= qpos
    key = jnp.where(allowed, key, jnp.int32(INT_MIN))
    kf = jnp.float32(ktop)

    tau = jnp.where(_count_rows(key >= 0) >= kf, jnp.int32(0), jnp.int32(INT_MIN))

    def tau_step(it, tau):
        cand = tau | jnp.left_shift(jnp.int32(1), 30 - it)
        return jnp.where(_count_rows(key >= cand) >= kf, cand, tau)

    tau = lax.fori_loop(0, 31, tau_step, tau)
    need = kf - _count_rows(key > tau)
    tie = (key == tau) & allowed

    def tie_step(it, p):
        cand = p | jnp.left_shift(jnp.int32(1), n_bits - 1 - it)
        return jnp.where(_count_rows(tie & (col < cand)) < need, cand, p)

    p = lax.fori_loop(0, n_bits, tie_step, jnp.zeros((Tq, 1), jnp.int32))
    sel = allowed & ((key > tau) | (tie & (col <= p)))

    group = ATT_HEADS // ATT_KV_HEADS
    scale = ATT_HD ** -0.5
    for g in range(ATT_KV_HEADS):
        kg = k_ref[0, :, g * ATT_HD:(g + 1) * ATT_HD].astype(jnp.bfloat16)
        vg = v_ref[0, :, g * ATT_HD:(g + 1) * ATT_HD].astype(jnp.bfloat16)
        for hh in range(group):
            lo = (g * group + hh) * ATT_HD
            qh = q_ref[0, :, lo:lo + ATT_HD].astype(jnp.bfloat16)
            logits = lax.dot_general(qh, kg, (((1,), (1,)), ((), ())), preferred_element_type=jnp.float32) * scale
            logits = jnp.where(sel, logits, NEG_INF)
            e = jnp.exp(logits - jnp.max(logits, axis=1, keepdims=True))
            out = jnp.dot(e.astype(jnp.bfloat16), vg, preferred_element_type=jnp.float32)
            o_ref[0, :, lo:lo + ATT_HD] = out / jnp.sum(e, axis=1, keepdims=True)


def dsa_prompt(q, k, v, qi, ki, wi):
    B, T = q.shape[:2]
    ktop = min(TOPK_MAX, T // 4)
    tq = min(DSA_Q_TILE, T)
    flat = lambda a: a.reshape(B, T, -1)
    args = (flat(q), flat(k), flat(v), flat(qi), ki, wi)
    n_groups = DSA_KEY_GROUPS if (T // tq) % DSA_KEY_GROUPS == 0 else 1
    tiles = T // tq // n_groups
    outs = []
    for grp in range(n_groups):
        first = grp * tiles
        t_keys = (first + tiles) * tq
        q_map = functools.partial(lambda b, i, first: (b, first + i, 0), first=first)
        kv_map = lambda b, i: (b, 0, 0)
        outs.append(pl.pallas_call(
            functools.partial(_dsa_prompt_kernel, ktop=ktop, first_q_tile=first),
            grid=(B, tiles),
            in_specs=[pl.BlockSpec((1, tq, ATT_Q), q_map),
                      pl.BlockSpec((1, t_keys, ATT_KV), kv_map),
                      pl.BlockSpec((1, t_keys, ATT_KV), kv_map),
                      pl.BlockSpec((1, tq, IDX_HEADS * IDX_HD), q_map),
                      pl.BlockSpec((1, t_keys, IDX_HD), kv_map),
                      pl.BlockSpec((1, tq, IDX_HEADS), q_map)],
            out_specs=pl.BlockSpec((1, tq, ATT_Q), lambda b, i: (b, i, 0)),
            out_shape=jax.ShapeDtypeStruct((B, tiles * tq, ATT_Q), jnp.float32),
            compiler_params=pltpu.CompilerParams(dimension_semantics=("parallel", "parallel"),
                                                 vmem_limit_bytes=VMEM_LIMIT_BYTES),
            name="dsa_prompt",
        )(*args))
    return jnp.concatenate(outs, axis=1).reshape(q.shape)


_NN = (((1,), (0,)), ((), ()))
_NT = (((1,), (1,)), ((), ()))
_TN = (((0,), (0,)), ((), ()))


def _dot_f32(x, y, dims=_NN):
    return lax.dot_general(x, y, dims, precision=HIGHEST, preferred_element_type=jnp.float32)


def _split_bf16(x):
    hi = x.astype(jnp.bfloat16)
    return hi, (x - hi.astype(jnp.float32)).astype(jnp.bfloat16)


def _dot3(x, y, dims=_NN):
    xh, xl = x if isinstance(x, tuple) else _split_bf16(x)
    yh, yl = y if isinstance(y, tuple) else _split_bf16(y)
    d = lambda p, q: lax.dot_general(p, q, dims, preferred_element_type=jnp.float32)
    return d(xh, yh) + (d(xh, yl) + d(xl, yh))


def _dot1(x, y, dims=_NN):
    return lax.dot_general(x.astype(jnp.bfloat16), y.astype(jnp.bfloat16), dims, preferred_element_type=jnp.float32)


def _unit_lower_solve(l_mats, rhs, size):
    xs = list(rhs)
    for step in range(max(1, int(size - 1).bit_length())):
        if step:
            l_mats = [_dot3(sp, sp) for sp in splits]
        splits = [_split_bf16(l) for l in l_mats]
        xs = [x + _dot3(sp, x) for sp, x in zip(splits, xs)]
    return xs


def _rwkv_kernel(r_ref, lw_ref, k_ref, v_ref, a_ref, b_ref, s0_ref, y_ref, s_ref):
    C = r_ref.shape[1]
    HG, N = s_ref.shape[1], s_ref.shape[2]
    row = lax.broadcasted_iota(jnp.int32, (C, C), 0)
    col = lax.broadcasted_iota(jnp.int32, (C, C), 1)
    incl = row >= col
    strict = row > col
    tri = jnp.where(incl, 1.0, 0.0)
    incl2 = (lax.broadcasted_iota(jnp.int32, (C, 2 * C), 0)
             >= (lax.broadcasted_iota(jnp.int32, (C, 2 * C), 1) & (C - 1)))

    @pl.when(pl.program_id(2) == 0)
    def _():
        s_ref[...] = s0_ref[...]

    heads = range(HG)
    head = lambda x, h: x[:, h * N:(h + 1) * N]
    lw = lw_ref[0]
    log_p = _dot_f32(tri, lw)
    inv_p = jnp.exp(-log_p)
    to_end = jnp.exp(log_p[C - 1:C, :] - log_p)
    p_end = jnp.exp(log_p[C - 1:C, :])
    a_t, r_t = a_ref[0] * jnp.exp(log_p - lw), r_ref[0] * jnp.exp(log_p)
    b_t, k_t = b_ref[0] * inv_p, k_ref[0] * inv_p
    b_end, k_end = b_ref[0] * to_end, k_ref[0] * to_end
    v_all = v_ref[0]
    v = [head(v_all, h) for h in heads]
    s0 = [s_ref[0, h] for h in heads]
    ar = [_split_bf16(jnp.concatenate([head(a_t, h), head(r_t, h)], axis=0)) for h in heads]
    g = [_dot3(ar[h], jnp.concatenate([head(b_t, h), head(k_t, h)], axis=0), _NT) for h in heads]
    ars = [_dot3(ar[h], s0[h], _NT) for h in heads]
    rhs = [ars[h][:C] + _dot3(jnp.where(strict, g[h][:C, C:], 0.0), v[h]) for h in heads]
    u = _unit_lower_solve([jnp.where(strict, g[h][:C, :C], 0.0) for h in heads], rhs, C)
    uv = [jnp.concatenate([u[h], v[h]], axis=0) for h in heads]
    for h in heads:
        y_ref[0, :, h * N:(h + 1) * N] = ars[h][C:] + _dot1(jnp.where(incl2, g[h][C:], 0.0), uv[h])
    for h in heads:
        bk_end = jnp.concatenate([head(b_end, h), head(k_end, h)], axis=0)
        s_ref[0, h] = s0[h] * head(p_end, h) + _dot3(uv[h], bk_end, _TN)


def rwkv7_scan(r, w_log, k, v, a, b, s0):
    B, T, H, N = r.shape
    C = min(RWKV_CHUNK, _round_up(T, 8))
    assert C & (C - 1) == 0
    Tp = _round_up(T, C)

    def prep(x):
        x = x.astype(jnp.float32).reshape(B, T, H * N)
        return jnp.pad(x, ((0, 0), (0, Tp - T), (0, 0))) if Tp != T else x

    hg = RWKV_HEAD_GROUP
    seq = pl.BlockSpec((1, C, hg * N), lambda bi, gi, ci: (bi, ci, gi))
    st = pl.BlockSpec((1, hg, N, N), lambda bi, gi, ci: (bi, gi, 0, 0))
    y, s_new = pl.pallas_call(
        _rwkv_kernel,
        grid=(B, H // hg, Tp // C),
        in_specs=[seq] * 6 + [st],
        out_specs=[seq, st],
        out_shape=[jax.ShapeDtypeStruct((B, Tp, H * N), jnp.float32),
                   jax.ShapeDtypeStruct((B, H, N, N), jnp.float32)],
        compiler_params=pltpu.CompilerParams(dimension_semantics=("parallel", "parallel", "arbitrary"),
                                             vmem_limit_bytes=VMEM_LIMIT_BYTES),
        name="rwkv7_scan",
    )(prep(r), prep(w_log), prep(k), prep(v), prep(a), prep(b), s0.astype(jnp.float32))
    return y[:, :T].reshape(B, T, H, N), s_new


def _gdn_kernel(q_ref, k_ref, v_ref, b_ref, g_ref, gr_ref, s0_ref, o_ref, s_ref):
    C = q_ref.shape[1]
    HG = s_ref.shape[1]
    DK, DV = s_ref.shape[2], s_ref.shape[3]
    row = lax.broadcasted_iota(jnp.int32, (C, C), 0)
    col = lax.broadcasted_iota(jnp.int32, (C, C), 1)
    incl = row >= col
    strict = row > col
    tri = jnp.where(incl, 1.0, 0.0)
    tri_t = jnp.where(row <= col, 1.0, 0.0)

    @pl.when(pl.program_id(2) == 0)
    def _():
        s_ref[...] = s0_ref[...]

    def unit(x):
        return x * lax.rsqrt(jnp.sum(x * x, axis=-1, keepdims=True) + EPS)

    heads = range(HG)
    lanes = lambda ref, h: jnp.broadcast_to(ref[0, 0, :, h:h + 1], (C, LANES))
    beta = [lanes(b_ref, h) for h in heads]
    k = [unit(k_ref[0, :, h * DK:(h + 1) * DK]) for h in heads]
    s0 = [s_ref[0, h] for h in heads]
    gc = [_dot_f32(tri, lanes(g_ref, h)) for h in heads]
    gc_row = [_dot_f32(gr_ref[0, h, 0], tri_t)[0:1, :] for h in heads]
    decay = [jnp.exp(jnp.where(incl, gc[h][:, :C] - gc_row[h], NEG_INF)) for h in heads]
    kb = [k[h] * beta[h] for h in heads]
    a_low = [jnp.where(strict, _dot3(kb[h], k[h], _NT) * decay[h], 0.0) for h in heads]
    rhs = [jnp.concatenate([v_ref[0, :, h * DV:(h + 1) * DV] * beta[h], kb[h] * jnp.exp(gc[h])], axis=1)
           for h in heads]
    sol = _unit_lower_solve([-a for a in a_low], rhs, C)
    v_new = [sol[h][:, :DV] - _dot3(sol[h][:, DV:], s0[h]) for h in heads]
    for h in heads:
        q = unit(q_ref[0, :, h * DK:(h + 1) * DK]) * (DK ** -0.5)
        qk = _dot1(q, k[h], _NT) * decay[h]
        o_ref[0, :, h * DV:(h + 1) * DV] = _dot1(q * jnp.exp(gc[h]), s0[h]) + _dot1(qk, v_new[h])
    for h in heads:
        g_last = gc[h][C - 1:C, :]
        s_ref[0, h] = s0[h] * jnp.exp(g_last) + _dot3(k[h] * jnp.exp(g_last - gc[h]), v_new[h], _TN)


def gated_delta_rule(qkv, beta, g, s0):
    B, T, _ = qkv.shape
    _, H, DK, DV = s0.shape
    assert DK == LANES and DV == LANES and qkv.shape[2] == (2 * DK + DV) * H
    C = min(GDN_CHUNK, _round_up(T, 8))
    Tp = _round_up(T, C)
    nc = Tp // C
    hg = GDN_HEAD_GROUP
    ng = H // hg
    pad_t = lambda x: jnp.pad(x, ((0, 0), (0, Tp - T), (0, 0))) if Tp != T else x
    qkv = pad_t(qkv.astype(jnp.float32))
    per_group = lambda x: jnp.transpose(pad_t(x.astype(jnp.float32)).reshape(B, Tp, ng, hg), (0, 2, 1, 3))
    g_rows = jnp.broadcast_to(jnp.moveaxis(pad_t(g.astype(jnp.float32)), 2, 1).reshape(B, H, nc, 1, C),
                              (B, H, nc, 8, C))
    cols = lambda first, d: pl.BlockSpec((1, C, hg * d), functools.partial(
        lambda bi, gi, ci, first: (bi, ci, first + gi), first=first))
    tok = pl.BlockSpec((1, 1, C, hg), lambda bi, gi, ci: (bi, gi, ci, 0))
    st = pl.BlockSpec((1, hg, DK, DV), lambda bi, gi, ci: (bi, gi, 0, 0))
    o, s_new = pl.pallas_call(
        _gdn_kernel,
        grid=(B, ng, nc),
        in_specs=[cols(0, DK), cols(ng, DK), cols(2 * ng, DV), tok, tok,
                  pl.BlockSpec((1, hg, 1, 8, C), lambda bi, gi, ci: (bi, gi, ci, 0, 0)), st],
        out_specs=[cols(0, DV), st],
        out_shape=[jax.ShapeDtypeStruct((B, Tp, H * DV), jnp.float32),
                   jax.ShapeDtypeStruct((B, H, DK, DV), jnp.float32)],
        compiler_params=pltpu.CompilerParams(dimension_semantics=("parallel", "parallel", "arbitrary"),
                                             vmem_limit_bytes=VMEM_LIMIT_BYTES),
        name="gated_delta_rule",
    )(qkv, qkv, qkv, per_group(beta), per_group(g), g_rows, s0.astype(jnp.float32))
    return o[:, :T], s_new


def _split(a, sizes, axis=-1):
    ends = [int(e) for e in np.cumsum(sizes)]
    return [lax.slice_in_dim(a, e - int(s), e, axis=axis % a.ndim) for s, e in zip(sizes, ends)]


def rmsnorm(x, gain):
    xf = x.astype(jnp.float32)
    y = xf * lax.rsqrt(jnp.mean(xf * xf, axis=-1, keepdims=True) + EPS)
    return (y * gain.astype(jnp.float32)).astype(x.dtype)


def l2norm(x):
    xf = x.astype(jnp.float32)
    return xf * lax.rsqrt(jnp.sum(xf * xf, axis=-1, keepdims=True) + EPS)


def partial_rope(x, pos):
    rd = x.shape[-1] // ROPE_DIV
    half = rd // 2
    inv_freq = ROPE_THETA ** (-jnp.arange(half, dtype=jnp.float32) * 2.0 / rd)
    ang = pos.astype(jnp.float32)[:, None] * inv_freq[None, :]
    cos = jnp.cos(ang)[:, None, :]
    sin = jnp.sin(ang)[:, None, :]
    xf = x[..., :rd].astype(jnp.float32)
    x1, x2 = xf[..., :half], xf[..., half:]
    rot = jnp.concatenate([x1 * cos - x2 * sin, x2 * cos + x1 * sin], axis=-1)
    return jnp.concatenate([rot.astype(x.dtype), x[..., rd:]], axis=-1)


def causal_conv(x, buf, w):
    T = x.shape[1]
    xp = jnp.concatenate([buf.astype(x.dtype), x], axis=1)
    y = xp[:, 0:T] * w[0]
    for i in range(1, CONV_W):
        y = y + xp[:, i:i + T] * w[i]
    return jax.nn.silu(y), xp[:, T:]


def _to_chunks(a, C):
    B, T = a.shape[:2]
    pad = (-T) % C
    a = jnp.pad(a.astype(jnp.float32), [(0, 0), (0, pad)] + [(0, 0)] * (a.ndim - 2))
    n = (T + pad) // C
    a = a.reshape((B, n, C) + a.shape[2:])
    return jnp.transpose(a, (1, 0, 3, 2) + tuple(range(4, a.ndim)))


def _from_chunks(o, T):
    n, B, H, C, X = o.shape
    return jnp.transpose(o, (1, 0, 3, 2, 4)).reshape(B, n * C, H, X)[:, :T]


def gla_chunked(q, k, v, log_a, s0):
    T = q.shape[1]
    C = min(GLA_CHUNK, T)
    qc, kc, vc = _to_chunks(q, C), _to_chunks(k, C), _to_chunks(v, C)
    bc = jnp.cumsum(_to_chunks(log_a, C), axis=-2)
    tri = jnp.tril(jnp.ones((C, C), bool))[..., None]

    def step(S, xs):
        q_n, k_n, v_n, b_n = xs
        rel = jnp.exp(jnp.where(tri, b_n[..., :, None, :] - b_n[..., None, :, :], -jnp.inf))
        att = jnp.einsum('bhik,bhijk,bhjk->bhij', q_n, rel, k_n)
        o = jnp.einsum('bhik,bhkv->bhiv', q_n * jnp.exp(b_n), S) + jnp.einsum('bhij,bhjv->bhiv', att, v_n)
        b_last = b_n[..., -1:, :]
        S = S * jnp.exp(b_last)[..., 0, :, None] + jnp.einsum('bhck,bhcv->bhkv', k_n * jnp.exp(b_last - b_n), v_n)
        return S, o

    S, o = lax.scan(step, s0.astype(jnp.float32), (qc, kc, vc, bc))
    return _from_chunks(o, T), S


def head_group_norm(y, w, b):
    B, T, H, N = y.shape
    mu = jnp.mean(y, axis=-1, keepdims=True)
    var = jnp.mean(jnp.square(y - mu), axis=-1, keepdims=True)
    yn = ((y - mu) * lax.rsqrt(var + RWKV_GN_EPS)).reshape(B, T, H * N)
    return yn * w.astype(jnp.float32) + b.astype(jnp.float32)


def indexer_scores(qi, wi, ki):
    dots = jnp.einsum('bqhd,bsd->bqhs', qi, ki).astype(jnp.float32)
    return jnp.einsum('bqh,bqhs->bqs', wi.astype(jnp.float32), jax.nn.relu(dots))


def select_keys(scores, qpos, ktop):
    S = scores.shape[-1]
    allowed = jnp.arange(S)[None, None, :] <= qpos[None, :, None]
    _, idx = lax.top_k(jnp.where(allowed, scores, -jnp.inf), ktop)
    return idx, idx <= qpos[None, :, None]


def sparse_attend(q, k_sel, v_sel, valid):
    B, Q, HQ, HD = q.shape
    qg = q.reshape(B, Q, ATT_KV_HEADS, HQ // ATT_KV_HEADS, HD)
    logits = jnp.einsum('bqhgd,bqkhd->bqhgk', qg, k_sel).astype(jnp.float32) * (HD ** -0.5)
    logits = jnp.where(valid[:, :, None, None, :], logits, -jnp.inf)
    p = jax.nn.softmax(logits, axis=-1).astype(v_sel.dtype)
    return jnp.einsum('bqhgk,bqkhd->bqhgd', p, v_sel).reshape(B, Q, HQ, HD)


def dsa_sample(q, k, v, qi, ki, wi, cache_k, cache_v, cache_ki, page_table):
    DB, DS = q.shape[:2]
    past = page_table.shape[1] * PAGE_SIZE
    ktop = min(TOPK_MAX, (past + DS) // 4)
    ki_past = cache_ki[page_table].reshape(DB, past, IDX_HD).astype(ki.dtype)
    ki_all = jnp.concatenate([ki_past, ki], axis=1)
    qpos = past + jnp.arange(DS)
    idx, valid = select_keys(indexer_scores(qi, wi, ki_all), qpos, ktop)
    is_new = idx >= past
    pidx = jnp.minimum(idx, past - 1)
    phys = jax.vmap(lambda pt, i: pt[i])(page_table, pidx // PAGE_SIZE)
    off = pidx % PAGE_SIZE
    nidx = jnp.clip(idx - past, 0, DS - 1)
    gather = jax.vmap(lambda rows, i: rows[i])
    sel = lambda cache, new: jnp.where(is_new[..., None, None], gather(new, nidx), cache[phys, off].astype(new.dtype))
    return sparse_attend(q, sel(cache_k, k), sel(cache_v, v), valid)


def gdn_branch(p, conv_buf, s0, lp):
    B, T, _ = p.shape
    conv_in, z, b_raw, a_raw = _split(p, (GDN_CONV_COLS, GDN_V, GDN_HEADS, GDN_HEADS))
    conv_out, new_buf = causal_conv(conv_in, conv_buf, lp['gdn_conv_w'])
    beta = jax.nn.sigmoid(b_raw.astype(jnp.float32))
    g = -jnp.exp(lp['gdn_A_log'].astype(jnp.float32)) * jax.nn.softplus(a_raw.astype(jnp.float32) + lp['gdn_dt_bias'].astype(jnp.float32))
    o, s_new = gated_delta_rule(conv_out, beta, g, s0)
    o = o.reshape(B, T, GDN_HEADS, GDN_DV)
    o = rmsnorm(o, lp['gdn_norm']) * jax.nn.silu(z.reshape(B, T, GDN_HEADS, GDN_DV).astype(jnp.float32))
    return o.reshape(B, T, GDN_V), new_buf, s_new


def gla_branch(p, s0, lp):
    B, T, _ = p.shape
    q, k, v, r, gd = _split(p, (GLA_QK, GLA_QK, GLA_V, GLA_V, GLA_RANK))
    hs = (B, T, GLA_HEADS)
    log_a = jax.nn.log_sigmoid((gd @ lp['gla_gate_up'] + lp['gla_gate_bias']).astype(jnp.float32)) / GLA_TAU
    o, s_new = gla_chunked(q.reshape(hs + (GLA_DK,)) * (GLA_DK ** -0.5), k.reshape(hs + (GLA_DK,)),
                           v.reshape(hs + (GLA_DV,)), log_a.reshape(hs + (GLA_DK,)), s0)
    o = rmsnorm(o, lp['gla_norm']) * jax.nn.silu(r.reshape(hs + (GLA_DV,)).astype(jnp.float32))
    return o.reshape(B, T, GLA_V), s_new


def dsa_branch(p, pos, attend):
    B, T, _ = p.shape
    q, k, v, qi, ki, wi = _split(p, (ATT_Q, ATT_KV, ATT_KV, IDX_HEADS * IDX_HD, IDX_HD, IDX_HEADS))
    q = partial_rope(q.reshape(B, T, ATT_HEADS, ATT_HD), pos)
    k = partial_rope(k.reshape(B, T, ATT_KV_HEADS, ATT_HD), pos)
    v = v.reshape(B, T, ATT_KV_HEADS, ATT_HD)
    qi = partial_rope(qi.reshape(B, T, IDX_HEADS, IDX_HD), pos)
    ki = partial_rope(ki.reshape(B, T, 1, IDX_HD), pos)[:, :, 0]
    o = attend(q, k, v, qi, ki, wi)
    return o.reshape(B, T, ATT_Q), (k, v, ki)


def rwkv_branch(p, shift_buf, s0, lp):
    B, T, width = p.shape
    pad_cols = lambda x: jnp.pad(x, [(0, 0)] * (x.ndim - 1) + [(0, width - RWKV_COLS)])
    prev = jnp.concatenate([pad_cols(shift_buf.astype(p.dtype)), p[:, :-1]], axis=1)
    pm = p + (prev - p) * pad_cols(lp['rwkv_mu'])
    r, wd, k, v, ad, gd = _split(pm, (RWKV_W, RWKV_W_RANK, RWKV_W, RWKV_W, RWKV_A_RANK, RWKV_G_RANK))
    hs = (B, T, RWKV_HEADS, RWKV_HD)
    w_log = -jax.nn.softplus(-(lp['rwkv_w0'] + jnp.tanh(wd) @ lp['rwkv_w2']).astype(jnp.float32)) - 0.5
    log_decay = -jnp.exp(w_log)
    a = jax.nn.sigmoid((lp['rwkv_a0'] + ad @ lp['rwkv_a2']).astype(jnp.float32))
    g = (jax.nn.sigmoid(gd) @ lp['rwkv_g2']).astype(jnp.float32)
    kf = k.astype(jnp.float32)
    kk = l2norm((kf * lp['rwkv_kk']).reshape(hs))
    kf = (kf * (1.0 + (a - 1.0) * lp['rwkv_ka'])).reshape(hs)
    rf = r.astype(jnp.float32).reshape(hs)
    vf = v.astype(jnp.float32).reshape(hs)
    y, s_new = rwkv7_scan(rf, log_decay.reshape(hs), kf, vf, -kk, kk * a.reshape(hs), s0)
    y = head_group_norm(y, lp['rwkv_ln_w'], lp['rwkv_ln_b'])
    bonus = (jnp.sum(rf * kf * lp['rwkv_rk'], axis=-1, keepdims=True) * vf).reshape(B, T, RWKV_W)
    return (y + bonus) * g, p[:, -1:, :RWKV_COLS], s_new


def token_mixers(h, pos, st, lp, attend):
    gdn_s, gdn_buf, gla_s, rwkv_s, rwkv_buf = st
    B, T, _ = h.shape
    hb = h.astype(jnp.bfloat16)
    p_gdn, p_gla, p_att, p_rwkv, p_gate = [_mm3(hb, w) for w in lp['w_in_groups']]
    o_gdn, gdn_buf, gdn_s = gdn_branch(p_gdn, gdn_buf, gdn_s, lp)
    o_gla, gla_s = gla_branch(p_gla, gla_s, lp)
    o_att, rows = dsa_branch(p_att, pos, attend)
    o_rwkv, rwkv_buf, rwkv_s = rwkv_branch(p_rwkv, rwkv_buf, rwkv_s, lp)
    assert len(set(BRANCH_SPLITS)) == 1
    merged = _merge_branches([o.reshape(B * T, -1) for o in (o_gdn, o_gla, o_att, o_rwkv)],
                             lp['w_branch'], p_gate.reshape(B * T, GATE_COLS))
    out = _mm(merged, lp['w_out']).reshape(B, T, D_MODEL)
    return out, rows + (gdn_s, gdn_buf, gla_s, rwkv_s, rwkv_buf)


def trunk_layer(x, mod, pos, st, lp, attend):
    B = x.shape[0]
    mod = mod.reshape(B, N_MOD, 1, D_MODEL)
    shift1, scale1, gate1, shift2, scale2, gate2 = [mod[:, i] for i in range(N_MOD)]
    h = rmsnorm(x, lp['norm1']) * (1.0 + scale1) + shift1
    mix, new_st = token_mixers(h, pos, st, lp, attend)
    x = x + gate1 * mix
    h = rmsnorm(x, lp['norm2']) * (1.0 + scale2) + shift2
    x = x + gate2 * peer_ffn(h, lp['peer_wq'], lp['peer_subkeys'], lp['peer_u_all'], lp['peer_v_all'], lp['layer'])
    return x, new_st


def kernel(x_prompt, x_sample, cache_k, cache_v, cache_kidx, state_gdn, state_gdn_conv, state_gla, state_rwkv, state_rwkv_shift, page_table, c_prompt, c_sample, w_ada, b_ada, norm1, norm2, w_in, gdn_conv_w, gdn_A_log, gdn_dt_bias, gdn_norm, gla_gate_up, gla_gate_bias, gla_norm, rwkv_mu, rwkv_w0, rwkv_w2, rwkv_a0, rwkv_a2, rwkv_g2, rwkv_kk, rwkv_ka, rwkv_rk, rwkv_ln_w, rwkv_ln_b, w_branch, w_out, peer_wq, peer_subkeys, peer_u, peer_v, final_norm):
    Bp, T, _ = x_prompt.shape
    DS = x_sample.shape[1]
    depth = w_in.shape[0]
    past = page_table.shape[1] * PAGE_SIZE
    pos_p = jnp.arange(T, dtype=jnp.int32)
    pos_s = past + jnp.arange(DS, dtype=jnp.int32)
    st_p0 = (jnp.zeros((Bp, GDN_HEADS, GDN_DK, GDN_DV), jnp.float32),
             jnp.zeros((Bp, CONV_W - 1, GDN_CONV_COLS), x_prompt.dtype),
             jnp.zeros((Bp, GLA_HEADS, GLA_DK, GLA_DV), jnp.float32),
             jnp.zeros((Bp, RWKV_HEADS, RWKV_HD, RWKV_HD), jnp.float32),
             jnp.zeros((Bp, 1, RWKV_COLS), x_prompt.dtype))
    weights = {'w_ada': w_ada, 'b_ada': b_ada, 'norm1': norm1, 'norm2': norm2, 'w_in': w_in,
               'gdn_conv_w': gdn_conv_w, 'gdn_A_log': gdn_A_log, 'gdn_dt_bias': gdn_dt_bias, 'gdn_norm': gdn_norm,
               'gla_gate_up': gla_gate_up, 'gla_gate_bias': gla_gate_bias, 'gla_norm': gla_norm,
               'rwkv_mu': rwkv_mu, 'rwkv_w0': rwkv_w0, 'rwkv_w2': rwkv_w2, 'rwkv_a0': rwkv_a0, 'rwkv_a2': rwkv_a2,
               'rwkv_g2': rwkv_g2, 'rwkv_kk': rwkv_kk, 'rwkv_ka': rwkv_ka, 'rwkv_rk': rwkv_rk,
               'rwkv_ln_w': rwkv_ln_w, 'rwkv_ln_b': rwkv_ln_b, 'w_branch': w_branch, 'w_out': w_out,
               'peer_wq': peer_wq, 'peer_subkeys': peer_subkeys, 'peer_u': peer_u, 'peer_v': peer_v}
    xp, xs = x_prompt, x_sample
    out_p, out_s = [], []
    for l in range(depth):
        lp = {name: arr[l] for name, arr in weights.items()}
        lp.update(layer=l, peer_u_all=peer_u, peer_v_all=peer_v)
        for name in ('w_branch', 'w_out', 'peer_wq'):
            lp[name] = lp[name].astype(jnp.bfloat16)
        lp['w_in_groups'] = [_pad_cols_bf16(w) for w in _split(lp['w_in'], IN_SPLITS)]
        mod = _mm_few_rows_f32w(jax.nn.silu(jnp.concatenate([c_prompt, c_sample], axis=0)), w_ada, l) + lp['b_ada']
        xp, new_p = trunk_layer(xp, mod[:Bp], pos_p, st_p0, lp, dsa_prompt)
        attend_s = functools.partial(dsa_sample, cache_k=cache_k[l], cache_v=cache_v[l],
                                     cache_ki=cache_kidx[l], page_table=page_table)
        st_s0 = (state_gdn[l], state_gdn_conv[l], state_gla[l], state_rwkv[l], state_rwkv_shift[l])
        xs, new_s = trunk_layer(xs, mod[Bp:], pos_s, st_s0, lp, attend_s)
        out_p.append(new_p)
        out_s.append(new_s)
    y_prompt = rmsnorm(xp, final_norm)
    y_sample = rmsnorm(xs, final_norm)
    k_p, v_p, kidx_p, gdn_p, gdn_conv_p, gla_p, rwkv_p, rwkv_shift_p = [jnp.stack(z) for z in zip(*out_p)]
    k_s, v_s, kidx_s, gdn_s, gdn_conv_s, gla_s, rwkv_s, rwkv_shift_s = [jnp.stack(z) for z in zip(*out_s)]
    return (y_prompt, y_sample, k_p, v_p, kidx_p, gdn_p, gdn_conv_p, gla_p, rwkv_p, rwkv_shift_p,
            k_s, v_s, kidx_s, gdn_s, gdn_conv_s, gla_s, rwkv_s, rwkv_shift_s)
```
